```python
import math
import jax, jax.numpy as jnp
from jax import lax
import numpy as np

D_MODEL = 2048
BATCH = 8
SEQ = 2048
DEPTH = 4

MEM_LEN = 256
HEAD_DIM = 128
A_HEADS = 6
IDX_HEADS = 16
IDX_DIM = 64
TOPK_MAX = 256
B_HEADS = 6
Q_LORA = 512
KV_LORA = 512
NOPE_DIM = 128
ROPE_DIM = 64
V_DIM = 128
ROPE_THETA = 10000.0
C_HEADS = 4
N_BRANCH = 3
REL_BUCKETS = 32
REL_MAX_DIST = 128
D_FF = 5632
Q_BLOCK = 128
LN_EPS = 1e-5
RMS_EPS = 1e-6
DEEPNORM_ALPHA = (2 * DEPTH) ** 0.25
DEEPNORM_BETA = (8 * DEPTH) ** -0.25
A_WIDTH = A_HEADS * HEAD_DIM
B_WIDTH = B_HEADS * V_DIM
C_WIDTH = C_HEADS * HEAD_DIM
MIX_WIDTH = A_WIDTH + B_WIDTH + C_WIDTH
IN_SIZES = (A_WIDTH, HEAD_DIM, HEAD_DIM,
            IDX_HEADS * IDX_DIM, IDX_DIM, IDX_HEADS,
            Q_LORA, KV_LORA, ROPE_DIM,
            C_WIDTH,
            N_BRANCH * D_MODEL)
IN_COLS = sum(IN_SIZES)

kernel_name = 'hybrid_dsa_mla_memory_macaron_deepnorm'


def layer_norm(x, g, b):
    xf = x.astype(jnp.float32)
    mu = jnp.mean(xf, axis=-1, keepdims=True)
    var = jnp.mean(jnp.square(xf - mu), axis=-1, keepdims=True)
    return ((xf - mu) * lax.rsqrt(var + LN_EPS) * g + b).astype(x.dtype)


def rms_norm(x, g):
    xf = x.astype(jnp.float32)
    return (xf * lax.rsqrt(jnp.mean(jnp.square(xf), axis=-1, keepdims=True) + RMS_EPS) * g).astype(x.dtype)


def swiglu(x, w_up, w_down):
    gate, up = jnp.split(x @ w_up, 2, axis=-1)
    return (jax.nn.silu(gate) * up) @ w_down


def rope(x, cos, sin):
    x1, x2 = jnp.split(x, 2, axis=-1)
    cos = cos.astype(x.dtype)
    sin = sin.astype(x.dtype)
    return jnp.concatenate([x1 * cos - x2 * sin, x1 * sin + x2 * cos], axis=-1)


def rel_bucket(dist):
    n = jnp.maximum(dist, 0)
    max_exact = REL_BUCKETS // 2
    nf = jnp.maximum(n, 1).astype(jnp.float32)
    large = max_exact + (jnp.log(nf / max_exact) / math.log(REL_MAX_DIST / max_exact)
                         * (REL_BUCKETS - max_exact)).astype(jnp.int32)
    large = jnp.minimum(large, REL_BUCKETS - 1)
    return jnp.where(n < max_exact, n, large)


_gather_rows = jax.vmap(lambda arr, idx: arr[idx])


def dsa_attention(qa, ka, va, iq, ik, iw, pos, rel_bias):
    B, S = qa.shape[0], qa.shape[1]
    k_sel = min(TOPK_MAX, S // 4)
    n_blk = S // Q_BLOCK
    key_idx = jnp.arange(S)
    scale = HEAD_DIM ** -0.5

    def block(i):
        t0 = i * Q_BLOCK
        q = lax.dynamic_slice_in_dim(qa, t0, Q_BLOCK, axis=1)
        qi = lax.dynamic_slice_in_dim(iq, t0, Q_BLOCK, axis=1)
        wi = lax.dynamic_slice_in_dim(iw, t0, Q_BLOCK, axis=1)
        pq = lax.dynamic_slice_in_dim(pos, t0, Q_BLOCK, axis=1)
        tq = t0 + jnp.arange(Q_BLOCK)
        dots = jnp.einsum('bthd,bsd->bths', qi, ik)
        score = jnp.einsum('bth,bths->bts', wi, jax.nn.relu(dots)).astype(jnp.float32)
        causal = key_idx[None, :] <= tq[:, None]
        score = jnp.where(causal[None], score, -jnp.inf)
        _, sel = lax.top_k(score, k_sel)
        valid = sel <= tq[None, :, None]
        k_g = _gather_rows(ka, sel)
        v_g = _gather_rows(va, sel)
        p_g = _gather_rows(pos, sel)
        logits = jnp.einsum('bthd,btkd->bhtk', q, k_g).astype(jnp.float32) * scale
        bias = rel_bias[rel_bucket(pq[:, :, None] - p_g)]
        logits = logits + jnp.moveaxis(bias, -1, 1).astype(jnp.float32)
        logits = jnp.where(valid[:, None], logits, -jnp.inf)
        p = jax.nn.softmax(logits, axis=-1).astype(v_g.dtype)
        return jnp.einsum('bhtk,btkd->bthd', p, v_g)

    out = lax.map(block, jnp.arange(n_blk))
    return jnp.transpose(out, (1, 0, 2, 3, 4)).reshape(B, S, A_HEADS * HEAD_DIM)


def causal_block_attention(q, k, v, scale):
    B, S, H = q.shape[0], q.shape[1], q.shape[2]
    n_blk = S // Q_BLOCK
    key_idx = jnp.arange(S)

    def block(i):
        t0 = i * Q_BLOCK
        qb = lax.dynamic_slice_in_dim(q, t0, Q_BLOCK, axis=1)
        tq = t0 + jnp.arange(Q_BLOCK)
        logits = jnp.einsum('bthd,bshd->bhts', qb, k).astype(jnp.float32) * scale
        logits = jnp.where((key_idx[None, :] <= tq[:, None])[None, None], logits, -jnp.inf)
        p = jax.nn.softmax(logits, axis=-1).astype(v.dtype)
        return jnp.einsum('bhts,bshd->bthd', p, v)

    out = lax.map(block, jnp.arange(n_blk))
    return jnp.transpose(out, (1, 0, 2, 3, 4)).reshape(B, S, H * v.shape[-1])


def token_mixing(h, mem, pos, cos, sin, rel_bias, w_in, q_norm, kv_norm, w_uq, w_ukv,
                 w_mem_kv, w_branch, w_out):
    B, S, D = h.shape
    splits = [int(c) for c in np.cumsum(IN_SIZES)[:-1]]
    (a_q, a_k, a_v, i_q, i_k, i_w, b_cq, b_ckv, b_kr, c_q, gates) = jnp.split(h @ w_in, splits, axis=-1)

    o_a = dsa_attention(a_q.reshape(B, S, A_HEADS, HEAD_DIM), a_k, a_v,
                        i_q.reshape(B, S, IDX_HEADS, IDX_DIM), i_k,
                        i_w * (IDX_HEADS * IDX_DIM) ** -0.5, pos, rel_bias)

    q = (rms_norm(b_cq, q_norm) @ w_uq).reshape(B, S, B_HEADS, NOPE_DIM + ROPE_DIM)
    q_nope, q_rope = jnp.split(q, [NOPE_DIM], axis=-1)
    q_rope = rope(q_rope, cos[:, :, None, :], sin[:, :, None, :])
    kv = (rms_norm(b_ckv, kv_norm) @ w_ukv).reshape(B, S, B_HEADS, NOPE_DIM + V_DIM)
    k_nope, v_b = jnp.split(kv, [NOPE_DIM], axis=-1)
    k_rope = rope(b_kr, cos, sin)
    k_rope = jnp.broadcast_to(k_rope[:, :, None, :], (B, S, B_HEADS, ROPE_DIM))
    o_b = causal_block_attention(jnp.concatenate([q_nope, q_rope], axis=-1),
                                 jnp.concatenate([k_nope, k_rope], axis=-1),
                                 v_b, (NOPE_DIM + ROPE_DIM) ** -0.5)

    mk, mv = jnp.split((mem @ w_mem_kv).reshape(B, mem.shape[1], 2, C_HEADS, HEAD_DIM), 2, axis=2)
    mk, mv = mk[:, :, 0], mv[:, :, 0]
    logits = jnp.einsum('bthd,bmhd->bhtm', c_q.reshape(B, S, C_HEADS, HEAD_DIM), mk).astype(jnp.float32)
    p = jax.nn.softmax(logits * HEAD_DIM ** -0.5, axis=-1).astype(mv.dtype)
    o_c = jnp.einsum('bhtm,bmhd->bthd', p, mv).reshape(B, S, C_WIDTH)

    y = jnp.stack([o_a @ w_branch[:A_WIDTH],
                   o_b @ w_branch[A_WIDTH:A_WIDTH + B_WIDTH],
                   o_c @ w_branch[A_WIDTH + B_WIDTH:]], axis=-2)
    g = jax.nn.sigmoid(gates.reshape(B, S, N_BRANCH, D))
    return jnp.sum(g * y, axis=-2) @ w_out


def setup_inputs(seed: int = 0) -> dict:
    key = jax.random.key(seed)
    ks = jax.random.split(key, 20)

    def nrm(k, shape, scale):
        return jax.random.normal(k, shape, jnp.float32) * scale

    offset = jax.random.randint(ks[2], (BATCH, 1), 0, 1024, dtype=jnp.int32)
    positions = offset + jnp.arange(SEQ, dtype=jnp.int32)[None, :]
    beta = DEEPNORM_BETA
    return {
        'x': nrm(ks[0], (BATCH, SEQ, D_MODEL), 1.0),
        'mem': nrm(ks[1], (BATCH, MEM_LEN, D_MODEL), 1.0),
        'positions': positions,
        'rel_bias': nrm(ks[3], (REL_BUCKETS, A_HEADS), 0.5),
        'ln_g': 1.0 + nrm(ks[4], (DEPTH, 3, D_MODEL), 0.02),
        'ln_b': nrm(ks[5], (DEPTH, 3, D_MODEL), 0.02),
        'ffn1_up': nrm(ks[6], (DEPTH, D_MODEL, 2 * D_FF), D_MODEL ** -0.5),
        'ffn1_down': nrm(ks[7], (DEPTH, D_FF, D_MODEL), beta * D_FF ** -0.5),
        'w_in': nrm(ks[8], (DEPTH, D_MODEL, IN_COLS), D_MODEL ** -0.5),
        'q_norm': 1.0 + nrm(ks[9], (DEPTH, Q_LORA), 0.02),
        'kv_norm': 1.0 + nrm(ks[10], (DEPTH, KV_LORA), 0.02),
        'w_uq': nrm(ks[11], (DEPTH, Q_LORA, B_HEADS * (NOPE_DIM + ROPE_DIM)), Q_LORA ** -0.5),
        'w_ukv': nrm(ks[12], (DEPTH, KV_LORA, B_HEADS * (NOPE_DIM + V_DIM)), KV_LORA ** -0.5),
        'w_mem_kv': nrm(ks[13], (DEPTH, D_MODEL, 2 * C_WIDTH), D_MODEL ** -0.5),
        'w_branch': nrm(ks[14], (DEPTH, MIX_WIDTH, D_MODEL), beta * HEAD_DIM ** -0.5 * 0.5),
        'w_out': nrm(ks[15], (DEPTH, D_MODEL, D_MODEL), beta * D_MODEL ** -0.5),
        'ffn2_up': nrm(ks[16], (DEPTH, D_MODEL, 2 * D_FF), D_MODEL ** -0.5),
        'ffn2_down': nrm(ks[17], (DEPTH, D_FF, D_MODEL), beta * D_FF ** -0.5),
    }


def reference(x, mem, positions, rel_bias, ln_g, ln_b, ffn1_up, ffn1_down, w_in, q_norm, kv_norm,
              w_uq, w_ukv, w_mem_kv, w_branch, w_out, ffn2_up, ffn2_down):
    inv_freq = ROPE_THETA ** (-jnp.arange(0, ROPE_DIM, 2, dtype=jnp.float32) / ROPE_DIM)
    ang = positions.astype(jnp.float32)[..., None] * inv_freq
    cos, sin = jnp.cos(ang), jnp.sin(ang)
    for l in range(DEPTH):
        x = layer_norm(DEEPNORM_ALPHA * x + 0.5 * swiglu(x, ffn1_up[l], ffn1_down[l]), ln_g[l, 0], ln_b[l, 0])
        mix = token_mixing(x, mem, positions, cos, sin, rel_bias, w_in[l], q_norm[l], kv_norm[l],
                           w_uq[l], w_ukv[l], w_mem_kv[l], w_branch[l], w_out[l])
        x = layer_norm(DEEPNORM_ALPHA * x + mix, ln_g[l, 1], ln_b[l, 1])
        x = layer_norm(DEEPNORM_ALPHA * x + 0.5 * swiglu(x, ffn2_up[l], ffn2_down[l]), ln_g[l, 2], ln_b[l, 2])
    return x
```

```python
import functools
import math

import jax
import jax.numpy as jnp
from jax import lax
from jax.experimental import pallas as pl
from jax.experimental.pallas import tpu as pltpu

F32 = jnp.float32
BF16 = jnp.bfloat16
I32 = jnp.int32

D_MODEL = 2048
D_FF = 5632
HEAD_DIM = 128
A_HEADS = 6
IDX_HEADS = 16
IDX_DIM = 64
TOPK_MAX = 256
B_HEADS = 6
Q_LORA = 512
KV_LORA = 512
NOPE_DIM = 128
ROPE_DIM = 64
V_DIM = 128
ROPE_THETA = 10000.0
C_HEADS = 4
REL_BUCKETS = 32
REL_MAX_DIST = 128
LN_EPS = 1e-5
RMS_EPS = 1e-6
A_WIDTH = A_HEADS * HEAD_DIM
B_WIDTH = B_HEADS * V_DIM
C_WIDTH = C_HEADS * HEAD_DIM
IN_SIZES = (A_WIDTH, HEAD_DIM, HEAD_DIM, IDX_HEADS * IDX_DIM, IDX_DIM, IDX_HEADS,
            Q_LORA, KV_LORA, ROPE_DIM, C_WIDTH, 3 * D_MODEL)

LANES = 128
MLA_QK = 2 * LANES
INT_MIN = -2 ** 31

PROJ_COLS = 10240
BLK_AK, BLK_AV = 6, 7
BLK_IK, BLK_IW, BLK_KRA, BLK_KRB = 16, 17, 18, 19
BLK512_CQ, BLK512_CKV, BLK512_C = 5, 6, 7
BLK2048_GATE0 = 2

VMEM_LIMIT = 56 * 1024 * 1024


def _params(sem, vmem=VMEM_LIMIT):
    return pltpu.CompilerParams(dimension_semantics=sem, vmem_limit_bytes=vmem)


def _layer_norm_rows(y, g, b):
    mu = jnp.mean(y, axis=-1, keepdims=True)
    d = y - mu
    var = jnp.mean(d * d, axis=-1, keepdims=True)
    return d * lax.rsqrt(var + LN_EPS) * g + b


def _ffn_kernel(xb_ref, x_ref, wg_ref, wu_ref, wd_ref, g_ref, b_ref, o_ref, ob_ref, acc_ref, *, alpha):
    j = pl.program_id(1)

    @pl.when(j == 0)
    def _():
        acc_ref[...] = jnp.zeros_like(acc_ref)

    xb = xb_ref[...]
    gate = jnp.dot(xb, wg_ref[...], preferred_element_type=F32)
    up = jnp.dot(xb, wu_ref[...], preferred_element_type=F32)
    h = (gate * (1.0 / (1.0 + jnp.exp(-gate))) * up).astype(BF16)
    acc_ref[...] += jnp.dot(h, wd_ref[...], preferred_element_type=F32)

    @pl.when(j == pl.num_programs(1) - 1)
    def _():
        y = alpha * x_ref[...] + 0.5 * acc_ref[...]
        out = _layer_norm_rows(y, g_ref[...], b_ref[...])
        o_ref[...] = out
        ob_ref[...] = out.astype(BF16)


def _ffn(xb, x, w_up, w_down, g, b, alpha, tm=512, tf=512):
    n = x.shape[0]
    tm = min(tm, n)
    nf = D_FF // tf
    return pl.pallas_call(
        functools.partial(_ffn_kernel, alpha=alpha),
        grid=(n // tm, nf),
        in_specs=[
            pl.BlockSpec((tm, D_MODEL), lambda i, j: (i, 0)),
            pl.BlockSpec((tm, D_MODEL), lambda i, j: (i, 0)),
            pl.BlockSpec((D_MODEL, tf), lambda i, j: (0, j)),
            pl.BlockSpec((D_MODEL, tf), lambda i, j: (0, j + nf)),
            pl.BlockSpec((tf, D_MODEL), lambda i, j: (j, 0)),
            pl.BlockSpec((1, D_MODEL), lambda i, j: (0, 0)),
            pl.BlockSpec((1, D_MODEL), lambda i, j: (0, 0)),
        ],
        out_specs=[
            pl.BlockSpec((tm, D_MODEL), lambda i, j: (i, 0)),
            pl.BlockSpec((tm, D_MODEL), lambda i, j: (i, 0)),
        ],
        out_shape=[jax.ShapeDtypeStruct((n, D_MODEL), F32),
                   jax.ShapeDtypeStruct((n, D_MODEL), BF16)],
        scratch_shapes=[pltpu.VMEM((tm, D_MODEL), F32)],
        compiler_params=_params(("parallel", "arbitrary")),
        name="ffn",
    )(xb, x, w_up, w_up, w_down, g, b)


def _matmul_kernel(x_ref, w_ref, o_ref):
    o_ref[...] = jnp.dot(x_ref[...], w_ref[...], preferred_element_type=F32).astype(o_ref.dtype)


def _matmul(xb, w, tm, tn, name):
    n, k = xb.shape
    tm = min(tm, n)
    cols = w.shape[1]
    return pl.pallas_call(
        _matmul_kernel,
        grid=(n // tm, cols // tn),
        in_specs=[pl.BlockSpec((tm, k), lambda i, j: (i, 0)),
                  pl.BlockSpec((k, tn), lambda i, j: (0, j))],
        out_specs=pl.BlockSpec((tm, tn), lambda i, j: (i, j)),
        out_shape=jax.ShapeDtypeStruct((n, cols), BF16),
        compiler_params=_params(("parallel", "arbitrary")),
        name=name,
    )(xb, w)


def _transpose_tile_bf16(t):
    return t.astype(F32).T.astype(BF16)


def _dsa_kernel(aq_ref, iq_ref, iw_ref, ik_ref, ak_ref, av_ref, bias_ref, o_ref,
                skey_ref, ika_ref, ikb_ref, vat_ref, m_ref, l_ref, acc_ref, *, k_sel, seq):
    i = pl.program_id(1)
    tq = LANES
    n_chunks = seq // LANES

    @pl.when(i == 0)
    def _():
        ik = ik_ref[0]
        lane = lax.broadcasted_iota(I32, ik.shape, 1)
        zero = jnp.zeros_like(ik)
        ika_ref[...] = jnp.where(lane < IDX_DIM, ik, zero)
        ikb_ref[...] = jnp.where(lane >= IDX_DIM, ik, zero)
        for c in range(n_chunks):
            vat_ref[c] = _transpose_tile_bf16(av_ref[0, c * LANES:(c + 1) * LANES, :])

    iqt = jnp.concatenate(
        [_transpose_tile_bf16(iq_ref[0, :, j * LANES:(j + 1) * LANES]) for j in range(IDX_HEADS // 2)], axis=1)
    wt = iw_ref[0].astype(F32).T * (IDX_HEADS * IDX_DIM) ** -0.5

    srow = lax.broadcasted_iota(I32, (LANES, tq), 0)
    tcol = lax.broadcasted_iota(I32, (LANES, tq), 1)

    def score_body(c, carry):
        s0 = pl.multiple_of(c * LANES, LANES)
        d_even = jnp.dot(ika_ref[pl.ds(s0, LANES), :], iqt, preferred_element_type=F32)
        d_odd = jnp.dot(ikb_ref[pl.ds(s0, LANES), :], iqt, preferred_element_type=F32)
        sc = jnp.zeros((LANES, tq), F32)
        for j in range(IDX_HEADS // 2):
            sc = sc + wt[2 * j:2 * j + 1, :] * jnp.maximum(d_even[:, j * LANES:(j + 1) * LANES], 0.0)
            sc = sc + wt[2 * j + 1:2 * j + 2, :] * jnp.maximum(d_odd[:, j * LANES:(j + 1) * LANES], 0.0)
        bits = pltpu.bitcast(sc + 0.0, I32)
        key = bits ^ ((bits >> 31) & jnp.int32(0x7FFFFFFF))
        causal = srow <= tcol + jnp.where(c < i, jnp.int32(LANES), jnp.int32(0))
        skey_ref[pl.ds(s0, LANES), :] = jnp.where(causal, key, jnp.int32(INT_MIN))
        return carry

    lax.fori_loop(0, i + 1, score_body, 0)

    def bit_body(it, u):
        bit = jnp.left_shift(jnp.int32(1), jnp.int32(31) - it)
        trial_u = u | bit
        trial = trial_u ^ jnp.int32(INT_MIN)

        def cnt_body(c, cnt):
            s0 = pl.multiple_of(c * LANES, LANES)
            ge = skey_ref[pl.ds(s0, LANES), :] >= trial
            return cnt + jnp.sum(jnp.where(ge, 1.0, 0.0), axis=0, keepdims=True)

        cnt = lax.fori_loop(0, i + 1, cnt_body, jnp.zeros((1, tq), F32))
        return jnp.where(cnt >= float(k_sel), trial_u, u)

    thr = lax.fori_loop(0, 32, bit_body, jnp.zeros((1, tq), I32)) ^ jnp.int32(INT_MIN)

    qt = jnp.concatenate(
        [_transpose_tile_bf16(aq_ref[0, :, h * LANES:(h + 1) * LANES]) for h in range(A_HEADS)], axis=1)

    m_ref[...] = jnp.full_like(m_ref, -jnp.inf)
    l_ref[...] = jnp.zeros_like(l_ref)
    acc_ref[...] = jnp.zeros_like(acc_ref)
    scale = HEAD_DIM ** -0.5

    def att_body(c, carry):
        s0 = pl.multiple_of(c * LANES, LANES)
        kind = jnp.clip(c - i + 2, 0, 2)
        lg = jnp.dot(ak_ref[0, pl.ds(s0, LANES), :], qt, preferred_element_type=F32) * scale + bias_ref[kind]
        sel = jnp.where(skey_ref[pl.ds(s0, LANES), :] >= thr, 0.0, -jnp.inf).astype(F32)
        lg = lg + jnp.concatenate([sel] * A_HEADS, axis=1)
        m_old = m_ref[...]
        m_new = jnp.maximum(m_old, jnp.max(lg, axis=0, keepdims=True))
        m_use = jnp.where(m_new == -jnp.inf, 0.0, m_new)
        alpha = jnp.exp(m_old - m_use)
        p = jnp.exp(lg - m_use)
        l_ref[...] = l_ref[...] * alpha + jnp.sum(p, axis=0, keepdims=True)
        acc_ref[...] = acc_ref[...] * alpha + jnp.dot(vat_ref[c], p.astype(BF16), preferred_element_type=F32)
        m_ref[...] = m_new
        return carry

    lax.fori_loop(0, i + 1, att_body, 0)

    out_t = acc_ref[...] / l_ref[...]
    for h in range(A_HEADS):
        o_ref[0, :, h * LANES:(h + 1) * LANES] = out_t[:, h * LANES:(h + 1) * LANES].T.astype(BF16)


def _dsa(proj3, bias_tab, k_sel):
    bsz, seq, _ = proj3.shape
    tq = LANES
    return pl.pallas_call(
        functools.partial(_dsa_kernel, k_sel=k_sel, seq=seq),
        grid=(bsz, seq // tq),
        in_specs=[
            pl.BlockSpec((1, tq, A_WIDTH), lambda b, i: (b, i, 0)),
            pl.BlockSpec((1, tq, IDX_HEADS * IDX_DIM), lambda b, i: (b, i, 1)),
            pl.BlockSpec((1, tq, LANES), lambda b, i: (b, i, BLK_IW)),
            pl.BlockSpec((1, seq, LANES), lambda b, i: (b, 0, BLK_IK)),
            pl.BlockSpec((1, seq, LANES), lambda b, i: (b, 0, BLK_AK)),
            pl.BlockSpec((1, seq, LANES), lambda b, i: (b, 0, BLK_AV)),
            pl.BlockSpec((3, LANES, A_WIDTH), lambda b, i: (0, 0, 0)),
        ],
        out_specs=pl.BlockSpec((1, tq, A_WIDTH), lambda b, i: (b, i, 0)),
        out_shape=jax.ShapeDtypeStruct((bsz, seq, A_WIDTH), BF16),
        scratch_shapes=[
            pltpu.VMEM((seq, tq), I32),
            pltpu.VMEM((seq, LANES), BF16),
            pltpu.VMEM((seq, LANES), BF16),
            pltpu.VMEM((seq // LANES, LANES, LANES), BF16),
            pltpu.VMEM((1, A_WIDTH), F32),
            pltpu.VMEM((1, A_WIDTH), F32),
            pltpu.VMEM((HEAD_DIM, A_WIDTH), F32),
        ],
        compiler_params=_params(("parallel", "arbitrary")),
        name="dsa",
    )(proj3, proj3, proj3, proj3, proj3, proj3, bias_tab)


def _rms_rows(x, g):
    return x * lax.rsqrt(jnp.mean(x * x, axis=-1, keepdims=True) + RMS_EPS) * g


def _mla_proj_kernel(cq_ref, ckv_ref, kra_ref, krb_ref, cc_ref, ss_ref, qn_ref, kvn_ref, wq_ref, wkv_ref,
                     q_ref, k_ref, v_ref):
    cc = cc_ref[...]
    ss = ss_ref[...]
    cq = _rms_rows(cq_ref[...].astype(F32), qn_ref[...]).astype(BF16)
    q3 = jnp.dot(cq, wq_ref[...], preferred_element_type=F32)
    for h in range(B_HEADS):
        base = 3 * LANES * h
        q_ref[:, MLA_QK * h:MLA_QK * h + LANES] = q3[:, base:base + LANES].astype(BF16)
        rot = q3[:, base + LANES:base + 2 * LANES] * cc + q3[:, base + 2 * LANES:base + 3 * LANES] * ss
        q_ref[:, MLA_QK * h + LANES:MLA_QK * (h + 1)] = rot.astype(BF16)
    ckv = _rms_rows(ckv_ref[...].astype(F32), kvn_ref[...]).astype(BF16)
    kv = jnp.dot(ckv, wkv_ref[...], preferred_element_type=F32)
    krot = (kra_ref[...].astype(F32) * cc + krb_ref[...].astype(F32) * ss).astype(BF16)
    for h in range(B_HEADS):
        k_ref[:, MLA_QK * h:MLA_QK * h + LANES] = kv[:, h * LANES:(h + 1) * LANES].astype(BF16)
        k_ref[:, MLA_QK * h + LANES:MLA_QK * (h + 1)] = krot
    v_ref[...] = kv[:, B_HEADS * NOPE_DIM:].astype(BF16)


def _mla_proj(proj, cc, ss, qn, kvn, wq, wkv, tm=512):
    n = proj.shape[0]
    const = lambda i: (0, 0)
    return pl.pallas_call(
        _mla_proj_kernel,
        grid=(n // tm,),
        in_specs=[
            pl.BlockSpec((tm, Q_LORA), lambda i: (i, BLK512_CQ)),
            pl.BlockSpec((tm, KV_LORA), lambda i: (i, BLK512_CKV)),
            pl.BlockSpec((tm, LANES), lambda i: (i, BLK_KRA)),
            pl.BlockSpec((tm, LANES), lambda i: (i, BLK_KRB)),
            pl.BlockSpec((tm, LANES), lambda i: (i, 0)),
            pl.BlockSpec((tm, LANES), lambda i: (i, 0)),
            pl.BlockSpec((1, Q_LORA), const),
            pl.BlockSpec((1, KV_LORA), const),
            pl.BlockSpec(wq.shape, const),
            pl.BlockSpec(wkv.shape, const),
        ],
        out_specs=[
            pl.BlockSpec((tm, B_HEADS * MLA_QK), lambda i: (i, 0)),
            pl.BlockSpec((tm, B_HEADS * MLA_QK), lambda i: (i, 0)),
            pl.BlockSpec((tm, B_WIDTH), lambda i: (i, 0)),
        ],
        out_shape=[jax.ShapeDtypeStruct((n, B_HEADS * MLA_QK), BF16),
                   jax.ShapeDtypeStruct((n, B_HEADS * MLA_QK), BF16),
                   jax.ShapeDtypeStruct((n, B_WIDTH), BF16)],
        compiler_params=_params(("parallel",)),
        name="mla_proj",
    )(proj, proj, proj, proj, cc, ss, qn, kvn, wq, wkv)


def _mla_attn_kernel(q_ref, k_ref, v_ref, o_ref, *, tq):
    qi = pl.program_id(2)
    q = q_ref[0]
    scale = (NOPE_DIM + ROPE_DIM) ** -0.5
    row = lax.broadcasted_iota(I32, (tq, tq), 0)
    col = lax.broadcasted_iota(I32, (tq, tq), 1)

    def step(c, carry, diagonal):
        m, l, acc = carry
        s0 = pl.multiple_of(c * tq, tq)
        k = k_ref[0, pl.ds(s0, tq), :]
        v = v_ref[0, pl.ds(s0, tq), :]
        s = lax.dot_general(q, k, (((1,), (1,)), ((), ())), preferred_element_type=F32) * scale
        if diagonal:
            s = jnp.where(col <= row, s, -jnp.inf)
        m_new = jnp.maximum(m, jnp.max(s, axis=-1, keepdims=True))
        alpha = jnp.exp(m - m_new)
        p = jnp.exp(s - m_new)
        l = alpha * l + jnp.sum(p, axis=-1, keepdims=True)
        acc = alpha * acc + jnp.dot(p.astype(BF16), v, preferred_element_type=F32)
        return m_new, l, acc

    init = (jnp.full((tq, 1), -jnp.inf, F32), jnp.zeros((tq, 1), F32), jnp.zeros((tq, V_DIM), F32))
    carry = lax.fori_loop(0, qi, lambda c, cr: step(c, cr, False), init)
    _, l, acc = step(qi, carry, True)
    o_ref[0] = (acc / l).astype(BF16)


def _mla_attn(q3, k3, v3, tq=256):
    bsz, seq, _ = q3.shape
    return pl.pallas_call(
        functools.partial(_mla_attn_kernel, tq=tq),
        grid=(bsz, B_HEADS, seq // tq),
        in_specs=[
            pl.BlockSpec((1, tq, MLA_QK), lambda b, h, i: (b, i, h)),
            pl.BlockSpec((1, seq, MLA_QK), lambda b, h, i: (b, 0, h)),
            pl.BlockSpec((1, seq, V_DIM), lambda b, h, i: (b, 0, h)),
        ],
        out_specs=pl.BlockSpec((1, tq, V_DIM), lambda b, h, i: (b, i, h)),
        out_shape=jax.ShapeDtypeStruct((bsz, seq, B_WIDTH), BF16),
        compiler_params=_params(("parallel", "parallel", "arbitrary")),
        name="mla_attn",
    )(q3, k3, v3)


def _mem_attn_kernel(q_ref, kv_ref, o_ref):
    scale = HEAD_DIM ** -0.5
    for h in range(C_HEADS):
        q = q_ref[0, :, h * LANES:(h + 1) * LANES]
        k = kv_ref[0, :, h * LANES:(h + 1) * LANES]
        v = kv_ref[0, :, C_WIDTH + h * LANES:C_WIDTH + (h + 1) * LANES]
        s = lax.dot_general(q, k, (((1,), (1,)), ((), ())), preferred_element_type=F32) * scale
        p = jnp.exp(s - jnp.max(s, axis=-1, keepdims=True))
        l = jnp.sum(p, axis=-1, keepdims=True)
        o = jnp.dot(p.astype(BF16), v, preferred_element_type=F32) / l
        o_ref[0, :, h * LANES:(h + 1) * LANES] = o.astype(BF16)


def _mem_attn(proj3, mkv3, tq=512):
    bsz, seq, _ = proj3.shape
    mlen = mkv3.shape[1]
    return pl.pallas_call(
        _mem_attn_kernel,
        grid=(bsz, seq // tq),
        in_specs=[pl.BlockSpec((1, tq, C_WIDTH), lambda b, i: (b, i, BLK512_C)),
                  pl.BlockSpec((1, mlen, 2 * C_WIDTH), lambda b, i: (b, 0, 0))],
        out_specs=pl.BlockSpec((1, tq, C_WIDTH), lambda b, i: (b, i, 0)),
        out_shape=jax.ShapeDtypeStruct((bsz, seq, C_WIDTH), BF16),
        compiler_params=_params(("parallel", "arbitrary")),
        name="mem_attn",
    )(proj3, mkv3)


def _mix_out_kernel(oa_ref, ob_ref, oc_ref, ga_ref, gb_ref, gc_ref, x_ref, wb_ref, wo_ref, g_ref, b_ref,
                    o_ref, obf_ref, *, alpha):
    def gated(o_blk, gate_ref, r0, r1):
        y = jnp.dot(o_blk, wb_ref[r0:r1, :], preferred_element_type=F32)
        return y * (1.0 / (1.0 + jnp.exp(-gate_ref[...].astype(F32))))

    y = gated(oa_ref[...], ga_ref, 0, A_WIDTH)
    y = y + gated(ob_ref[...], gb_ref, A_WIDTH, A_WIDTH + B_WIDTH)
    y = y + gated(oc_ref[...], gc_ref, A_WIDTH + B_WIDTH, A_WIDTH + B_WIDTH + C_WIDTH)
    mix = jnp.dot(y.astype(BF16), wo_ref[...], preferred_element_type=F32)
    out = _layer_norm_rows(alpha * x_ref[...] + mix, g_ref[...], b_ref[...])
    o_ref[...] = out
    obf_ref[...] = out.astype(BF16)


def _mix_out(oa, ob, oc, proj, x, wb, wo, g, b, alpha, tm=256):
    n = x.shape[0]
    const = lambda i: (0, 0)
    return pl.pallas_call(
        functools.partial(_mix_out_kernel, alpha=alpha),
        grid=(n // tm,),
        in_specs=[
            pl.BlockSpec((tm, A_WIDTH), lambda i: (i, 0)),
            pl.BlockSpec((tm, B_WIDTH), lambda i: (i, 0)),
            pl.BlockSpec((tm, C_WIDTH), lambda i: (i, 0)),
            pl.BlockSpec((tm, D_MODEL), lambda i: (i, BLK2048_GATE0)),
            pl.BlockSpec((tm, D_MODEL), lambda i: (i, BLK2048_GATE0 + 1)),
            pl.BlockSpec((tm, D_MODEL), lambda i: (i, BLK2048_GATE0 + 2)),
            pl.BlockSpec((tm, D_MODEL), lambda i: (i, 0)),
            pl.BlockSpec(wb.shape, const),
            pl.BlockSpec(wo.shape, const),
            pl.BlockSpec((1, D_MODEL), const),
            pl.BlockSpec((1, D_MODEL), const),
        ],
        out_specs=[pl.BlockSpec((tm, D_MODEL), lambda i: (i, 0)),
                   pl.BlockSpec((tm, D_MODEL), lambda i: (i, 0))],
        out_shape=[jax.ShapeDtypeStruct((n, D_MODEL), F32),
                   jax.ShapeDtypeStruct((n, D_MODEL), BF16)],
        compiler_params=_params(("parallel",)),
        name="mix_out",
    )(oa, ob, oc, proj, proj, proj, x, wb, wo, g, b)


def _pack_w_in(w):
    offs = [0]
    for s in IN_SIZES:
        offs.append(offs[-1] + s)
    seg = lambda k: w[:, offs[k]:offs[k + 1]]
    d = w.shape[0]
    zeros = lambda c: jnp.zeros((d, c), w.dtype)
    ik, iw, kr = seg(4), seg(5), seg(8)
    half = ROPE_DIM // 2
    packed = jnp.concatenate([
        w[:, :offs[4]],
        ik, ik,
        iw, zeros(LANES - IDX_HEADS),
        kr, zeros(LANES - ROPE_DIM),
        kr[:, half:], kr[:, :half], zeros(LANES - ROPE_DIM),
        seg(6), seg(7), seg(9), seg(10),
    ], axis=1)
    assert packed.shape[1] == PROJ_COLS
    return packed.astype(BF16)


def _pack_w_uq(w):
    half = ROPE_DIM // 2
    zeros = jnp.zeros((w.shape[0], LANES - ROPE_DIM), w.dtype)
    cols = []
    for h in range(B_HEADS):
        base = h * (NOPE_DIM + ROPE_DIM)
        r = w[:, base + NOPE_DIM:base + NOPE_DIM + ROPE_DIM]
        cols += [w[:, base:base + NOPE_DIM], r, zeros, r[:, half:], r[:, :half], zeros]
    return jnp.concatenate(cols, axis=1).astype(BF16)


def _pack_w_ukv(w):
    step = NOPE_DIM + V_DIM
    ks = [w[:, h * step:h * step + NOPE_DIM] for h in range(B_HEADS)]
    vs = [w[:, h * step + NOPE_DIM:(h + 1) * step] for h in range(B_HEADS)]
    return jnp.concatenate(ks + vs, axis=1).astype(BF16)


def _rel_bucket(dist):
    n = jnp.maximum(dist, 0)
    max_exact = REL_BUCKETS // 2
    nf = jnp.maximum(n, 1).astype(F32)
    large = max_exact + (jnp.log(nf / max_exact) / math.log(REL_MAX_DIST / max_exact)
                         * (REL_BUCKETS - max_exact)).astype(I32)
    large = jnp.minimum(large, REL_BUCKETS - 1)
    return jnp.where(n < max_exact, n, large)


def _bias_tables(rel_bias):
    s = jnp.arange(LANES)[:, None]
    t = jnp.arange(LANES)[None, :]
    tiles = []
    for block_gap, causal in ((2, False), (1, False), (0, True)):
        dist = t - s + block_gap * LANES
        tile = rel_bias[_rel_bucket(dist)]
        if causal:
            tile = jnp.where((dist >= 0)[:, :, None], tile, -jnp.inf)
        tiles.append(jnp.transpose(tile, (0, 2, 1)).reshape(LANES, A_WIDTH))
    return jnp.stack(tiles).astype(F32)


def kernel(x, mem, positions, rel_bias, ln_g, ln_b, ffn1_up, ffn1_down, w_in, q_norm, kv_norm, w_uq, w_ukv,
           w_mem_kv, w_branch, w_out, ffn2_up, ffn2_down):
    bsz, seq, d = x.shape
    depth = ffn1_up.shape[0]
    n = bsz * seq
    alpha = (2 * depth) ** 0.25
    k_sel = min(TOPK_MAX, seq // 4)
    assert d == D_MODEL and seq % 256 == 0 and seq >= 2 * LANES

    inv_freq = ROPE_THETA ** (-jnp.arange(0, ROPE_DIM, 2, dtype=F32) / ROPE_DIM)
    ang = positions.astype(F32)[..., None] * inv_freq
    cos, sin = jnp.cos(ang).reshape(n, -1), jnp.sin(ang).reshape(n, -1)
    pad = jnp.zeros((n, LANES - ROPE_DIM), F32)
    cc = jnp.concatenate([cos, cos, pad], axis=1)
    ss = jnp.concatenate([-sin, sin, pad], axis=1)
    bias_tab = _bias_tables(rel_bias)

    xf = x.reshape(n, d)
    xb = xf.astype(BF16)
    memb = mem.reshape(bsz * mem.shape[1], d).astype(BF16)

    for l in range(depth):
        row = lambda v: v.reshape(1, -1)
        xf, xb = _ffn(xb, xf, ffn1_up[l].astype(BF16), ffn1_down[l].astype(BF16),
                      row(ln_g[l, 0]), row(ln_b[l, 0]), alpha)
        proj = _matmul(xb, _pack_w_in(w_in[l]), 1024, 1024, "in_proj")
        proj3 = proj.reshape(bsz, seq, PROJ_COLS)
        o_a = _dsa(proj3, bias_tab, k_sel).reshape(n, A_WIDTH)
        q, k, v = _mla_proj(proj, cc, ss, row(q_norm[l]), row(kv_norm[l]), _pack_w_uq(w_uq[l]), _pack_w_ukv(w_ukv[l]))
        o_b = _mla_attn(q.reshape(bsz, seq, -1), k.reshape(bsz, seq, -1), v.reshape(bsz, seq, -1)).reshape(n, B_WIDTH)
        mkv = _matmul(memb, w_mem_kv[l].astype(BF16), 512, 1024, "mem_proj")
        o_c = _mem_attn(proj3, mkv.reshape(bsz, mem.shape[1], 2 * C_WIDTH)).reshape(n, C_WIDTH)
        xf, xb = _mix_out(o_a, o_b, o_c, proj, xf, w_branch[l].astype(BF16), w_out[l].astype(BF16),
                          row(ln_g[l, 1]), row(ln_b[l, 1]), alpha)
        xf, xb = _ffn(xb, xf, ffn2_up[l].astype(BF16), ffn2_down[l].astype(BF16),
                      row(ln_g[l, 2]), row(ln_b[l, 2]), alpha)
    return xf.reshape(bsz, seq, d)
```

```python
import functools
import math

import jax
import jax.numpy as jnp
from jax import lax
from jax.experimental import pallas as pl
from jax.experimental.pallas import tpu as pltpu

F32 = jnp.float32
BF16 = jnp.bfloat16
I32 = jnp.int32

D_MODEL = 2048
D_FF = 5632
HEAD_DIM = 128
A_HEADS = 6
IDX_HEADS = 16
IDX_DIM = 64
TOPK_MAX = 256
B_HEADS = 6
Q_LORA = 512
KV_LORA = 512
NOPE_DIM = 128
ROPE_DIM = 64
V_DIM = 128
ROPE_THETA = 10000.0
C_HEADS = 4
REL_BUCKETS = 32
REL_MAX_DIST = 128
LN_EPS = 1e-5
RMS_EPS = 1e-6
A_WIDTH = A_HEADS * HEAD_DIM
B_WIDTH = B_HEADS * V_DIM
C_WIDTH = C_HEADS * HEAD_DIM
IN_SIZES = (A_WIDTH, HEAD_DIM, HEAD_DIM, IDX_HEADS * IDX_DIM, IDX_DIM, IDX_HEADS,
            Q_LORA, KV_LORA, ROPE_DIM, C_WIDTH, 3 * D_MODEL)

LANES = 128
SUBLANES = 8
BLOCK = 256
MLA_QK = 2 * LANES
INT_MIN = -2 ** 31
LOG2E = 1.4426950408889634

PROJ_COLS = 10240
BLK_AK, BLK_AV = 6, 7
BLK_IK, BLK_IW, BLK_KRA, BLK_KRB = 16, 17, 18, 19
BLK512_CQ, BLK512_CKV, BLK512_C = 5, 6, 7
BLK2048_GATE0 = 2

VMEM_LIMIT = 56 * 1024 * 1024


def _params(sem, vmem=VMEM_LIMIT):
    return pltpu.CompilerParams(dimension_semantics=sem, vmem_limit_bytes=vmem)


def _layer_norm_rows(y, g, b):
    mu = jnp.mean(y, axis=-1, keepdims=True)
    d = y - mu
    var = jnp.mean(d * d, axis=-1, keepdims=True)
    return d * lax.rsqrt(var + LN_EPS) * g + b


def _ffn_kernel(xb_ref, x_ref, wg_ref, wu_ref, wd_ref, g_ref, b_ref, o_ref, ob_ref, acc_ref, *, alpha):
    j = pl.program_id(1)

    @pl.when(j == 0)
    def _():
        acc_ref[...] = jnp.zeros_like(acc_ref)

    xb = xb_ref[...]
    gate = jnp.dot(xb, wg_ref[...], preferred_element_type=F32)
    up = jnp.dot(xb, wu_ref[...], preferred_element_type=F32)
    h = (gate * (1.0 / (1.0 + jnp.exp(-gate))) * up).astype(BF16)
    acc_ref[...] += jnp.dot(h, wd_ref[...], preferred_element_type=F32)

    @pl.when(j == pl.num_programs(1) - 1)
    def _():
        y = alpha * x_ref[...] + 0.5 * acc_ref[...]
        out = _layer_norm_rows(y, g_ref[...], b_ref[...])
        o_ref[...] = out
        ob_ref[...] = out.astype(BF16)


def _ffn(xb, x, w_up, w_down, g, b, alpha, tm=512, tf=512):
    n = x.shape[0]
    tm = min(tm, n)
    nf = D_FF // tf
    return pl.pallas_call(
        functools.partial(_ffn_kernel, alpha=alpha),
        grid=(n // tm, nf),
        in_specs=[
            pl.BlockSpec((tm, D_MODEL), lambda i, j: (i, 0)),
            pl.BlockSpec((tm, D_MODEL), lambda i, j: (i, 0)),
            pl.BlockSpec((D_MODEL, tf), lambda i, j: (0, j)),
            pl.BlockSpec((D_MODEL, tf), lambda i, j: (0, j + nf)),
            pl.BlockSpec((tf, D_MODEL), lambda i, j: (j, 0)),
            pl.BlockSpec((1, D_MODEL), lambda i, j: (0, 0)),
            pl.BlockSpec((1, D_MODEL), lambda i, j: (0, 0)),
        ],
        out_specs=[
            pl.BlockSpec((tm, D_MODEL), lambda i, j: (i, 0)),
            pl.BlockSpec((tm, D_MODEL), lambda i, j: (i, 0)),
        ],
        out_shape=[jax.ShapeDtypeStruct((n, D_MODEL), F32),
                   jax.ShapeDtypeStruct((n, D_MODEL), BF16)],
        scratch_shapes=[pltpu.VMEM((tm, D_MODEL), F32)],
        compiler_params=_params(("parallel", "arbitrary")),
        name="ffn",
    )(xb, x, w_up, w_up, w_down, g, b)


def _matmul_kernel(x_ref, w_ref, o_ref):
    o_ref[...] = jnp.dot(x_ref[...], w_ref[...], preferred_element_type=F32).astype(o_ref.dtype)


def _matmul(xb, w, tm, tn, name):
    n, k = xb.shape
    tm = min(tm, n)
    cols = w.shape[1]
    return pl.pallas_call(
        _matmul_kernel,
        grid=(n // tm, cols // tn),
        in_specs=[pl.BlockSpec((tm, k), lambda i, j: (i, 0)),
                  pl.BlockSpec((k, tn), lambda i, j: (0, j))],
        out_specs=pl.BlockSpec((tm, tn), lambda i, j: (i, j)),
        out_shape=jax.ShapeDtypeStruct((n, cols), BF16),
        compiler_params=_params(("parallel", "arbitrary")),
        name=name,
    )(xb, w)


def _dsa_kernel(aq_ref, iq_ref, iw_ref, ik_ref, ak_ref, av_ref, bias_ref, o_ref,
                skey_ref, ika_ref, ikb_ref, vat_ref, m_ref, l_ref, acc_ref, lg_ref, *, k_sel, seq):
    i = pl.program_id(1)
    blk = BLOCK

    @pl.when(i == 0)
    def _():
        ik = ik_ref[0]
        lane = lax.broadcasted_iota(I32, ik.shape, 1)
        zero = jnp.zeros_like(ik)
        ika_ref[...] = jnp.where(lane < IDX_DIM, ik, zero)
        ikb_ref[...] = jnp.where(lane >= IDX_DIM, ik, zero)
        for c in range(seq // blk):
            vat_ref[c] = av_ref[0, c * blk:(c + 1) * blk, :].astype(F32).T.astype(BF16)

    iqt = iq_ref[0].astype(F32).T.astype(BF16)
    wt = iw_ref[0].astype(F32).T * (IDX_HEADS * IDX_DIM) ** -0.5

    srow = lax.broadcasted_iota(I32, (blk, blk), 0)
    tcol = lax.broadcasted_iota(I32, (blk, blk), 1)

    def score_body(c, carry):
        s0 = pl.multiple_of(c * blk, blk)
        ka = ika_ref[pl.ds(s0, blk), :]
        kb = ikb_ref[pl.ds(s0, blk), :]
        sc = jnp.zeros((blk, blk), F32)
        for j in range(IDX_HEADS // 2):
            rhs = iqt[j * LANES:(j + 1) * LANES, :]
            sc = sc + wt[2 * j:2 * j + 1, :] * jnp.maximum(jnp.dot(ka, rhs, preferred_element_type=F32), 0.0)
            sc = sc + wt[2 * j + 1:2 * j + 2, :] * jnp.maximum(jnp.dot(kb, rhs, preferred_element_type=F32), 0.0)
        bits = pltpu.bitcast(sc, I32)
        key = bits ^ ((bits >> 31) & jnp.int32(0x7FFFFFFF))
        causal = srow <= tcol + jnp.where(c < i, jnp.int32(blk), jnp.int32(0))
        skey_ref[pl.ds(s0, blk), :] = jnp.where(causal, key, jnp.int32(INT_MIN))
        return carry

    lax.fori_loop(0, i + 1, score_body, 0)

    n_acc = 4

    def bit_body(it, u):
        bit = jnp.left_shift(jnp.int32(1), jnp.int32(31) - it)
        trial_u = u | bit
        trial = jnp.broadcast_to(trial_u ^ jnp.int32(INT_MIN), (SUBLANES, blk))

        def cnt_body(c, accs):
            keys = skey_ref[pl.ds(pl.multiple_of(c * blk, blk), blk), :]
            accs = list(accs)
            for r in range(blk // SUBLANES):
                rows = keys[r * SUBLANES:(r + 1) * SUBLANES, :]
                accs[r % n_acc] = accs[r % n_acc] + jnp.where(rows >= trial, 1.0, 0.0)
            return tuple(accs)

        zero = jnp.zeros((SUBLANES, blk), F32)
        accs = lax.fori_loop(0, i + 1, cnt_body, (zero,) * n_acc)
        cnt = jnp.sum((accs[0] + accs[1]) + (accs[2] + accs[3]), axis=0, keepdims=True)
        return jnp.where(cnt >= float(k_sel), trial_u, u)

    thr = lax.fori_loop(0, 32, bit_body, jnp.zeros((1, blk), I32)) ^ jnp.int32(INT_MIN)

    qt = (aq_ref[0].astype(F32) * (HEAD_DIM ** -0.5 * LOG2E)).T.astype(BF16)

    m_ref[...] = jnp.full_like(m_ref, -jnp.inf)
    l_ref[...] = jnp.zeros_like(l_ref)
    acc_ref[...] = jnp.zeros_like(acc_ref)

    def logits(c):
        ka = ak_ref[0, pl.ds(pl.multiple_of(c * blk, blk), blk), :]
        return [jnp.dot(ka, qt[h * LANES:(h + 1) * LANES, :], preferred_element_type=F32) for h in range(A_HEADS)]

    def att_step(c, bias_idx, has_next):
        slot = c % 2
        nxt = logits(c + 1) if has_next else None
        sel = skey_ref[pl.ds(pl.multiple_of(c * blk, blk), blk), :] >= thr
        vt = vat_ref[c]
        for h in range(A_HEADS):
            lg = lg_ref[slot, h]
            if bias_idx is not None:
                lg = lg + bias_ref[bias_idx, h]
            lg = jnp.where(sel, lg, -jnp.inf)
            m_old = m_ref[h]
            m_new = jnp.maximum(m_old, jnp.max(lg, axis=0, keepdims=True))
            m_use = jnp.where(m_new == -jnp.inf, 0.0, m_new)
            alpha = jnp.exp2(m_old - m_use)
            p = jnp.exp2(lg - m_use)
            l_ref[h] = l_ref[h] * alpha + jnp.sum(p, axis=0, keepdims=True)
            acc_ref[h] = acc_ref[h] * alpha + jnp.dot(vt, p.astype(BF16), preferred_element_type=F32)
            m_ref[h] = m_new
        if has_next:
            for h in range(A_HEADS):
                lg_ref[1 - slot, h] = nxt[h]

    first = logits(0)
    for h in range(A_HEADS):
        lg_ref[0, h] = first[h]

    def far_body(c, carry):
        att_step(c, None, True)
        return carry

    lax.fori_loop(0, jnp.maximum(i - 1, 0), far_body, 0)

    @pl.when(i >= 1)
    def _():
        att_step(i - 1, 0, True)

    att_step(i, 1, False)

    for h in range(A_HEADS):
        o_ref[0, :, h * LANES:(h + 1) * LANES] = (acc_ref[h] / l_ref[h]).T.astype(BF16)


def _dsa(proj3, bias_tab, k_sel):
    bsz, seq, _ = proj3.shape
    blk = BLOCK
    return pl.pallas_call(
        functools.partial(_dsa_kernel, k_sel=k_sel, seq=seq),
        grid=(bsz, seq // blk),
        in_specs=[
            pl.BlockSpec((1, blk, A_WIDTH), lambda b, i: (b, i, 0)),
            pl.BlockSpec((1, blk, IDX_HEADS * IDX_DIM), lambda b, i: (b, i, 1)),
            pl.BlockSpec((1, blk, LANES), lambda b, i: (b, i, BLK_IW)),
            pl.BlockSpec((1, seq, LANES), lambda b, i: (b, 0, BLK_IK)),
            pl.BlockSpec((1, seq, LANES), lambda b, i: (b, 0, BLK_AK)),
            pl.BlockSpec((1, seq, LANES), lambda b, i: (b, 0, BLK_AV)),
            pl.BlockSpec(bias_tab.shape, lambda b, i: (0, 0, 0, 0)),
        ],
        out_specs=pl.BlockSpec((1, blk, A_WIDTH), lambda b, i: (b, i, 0)),
        out_shape=jax.ShapeDtypeStruct((bsz, seq, A_WIDTH), BF16),
        scratch_shapes=[
            pltpu.VMEM((seq, blk), I32),
            pltpu.VMEM((seq, LANES), BF16),
            pltpu.VMEM((seq, LANES), BF16),
            pltpu.VMEM((seq // blk, HEAD_DIM, blk), BF16),
            pltpu.VMEM((A_HEADS, 1, blk), F32),
            pltpu.VMEM((A_HEADS, 1, blk), F32),
            pltpu.VMEM((A_HEADS, HEAD_DIM, blk), F32),
            pltpu.VMEM((2, A_HEADS, blk, blk), F32),
        ],
        compiler_params=_params(("parallel", "arbitrary")),
        name="dsa",
    )(proj3, proj3, proj3, proj3, proj3, proj3, bias_tab)


def _rms_rows(x, g):
    return x * lax.rsqrt(jnp.mean(x * x, axis=-1, keepdims=True) + RMS_EPS) * g


def _mla_proj_kernel(cq_ref, ckv_ref, kra_ref, krb_ref, cc_ref, ss_ref, qn_ref, kvn_ref, wq_ref, wkv_ref,
                     qt_ref, k_ref, vt_ref):
    tm = cq_ref.shape[0]
    cc = cc_ref[...]
    ss = ss_ref[...]
    qscale = (NOPE_DIM + ROPE_DIM) ** -0.5 * LOG2E
    cq = _rms_rows(cq_ref[...].astype(F32), qn_ref[...]).astype(BF16)
    q3 = jnp.dot(cq, wq_ref[...], preferred_element_type=F32)
    for h in range(B_HEADS):
        base = 3 * LANES * h
        rot = q3[:, base + LANES:base + 2 * LANES] * cc + q3[:, base + 2 * LANES:base + 3 * LANES] * ss
        qt_ref[h, 0:LANES, :] = (q3[:, base:base + LANES] * qscale).T.astype(BF16)
        qt_ref[h, LANES:MLA_QK, :] = (rot * qscale).T.astype(BF16)
    ckv = _rms_rows(ckv_ref[...].astype(F32), kvn_ref[...]).astype(BF16)
    kv = jnp.dot(ckv, wkv_ref[...], preferred_element_type=F32)
    krot = (kra_ref[...].astype(F32) * cc + krb_ref[...].astype(F32) * ss).astype(BF16)
    for h in range(B_HEADS):
        k_ref[:, MLA_QK * h:MLA_QK * h + LANES] = kv[:, h * LANES:(h + 1) * LANES].astype(BF16)
        k_ref[:, MLA_QK * h + LANES:MLA_QK * (h + 1)] = krot
        v_h = kv[:, B_HEADS * NOPE_DIM + h * V_DIM:B_HEADS * NOPE_DIM + (h + 1) * V_DIM]
        for c in range(tm // BLOCK):
            vt_ref[h, c] = v_h[c * BLOCK:(c + 1) * BLOCK, :].T.astype(BF16)


def _mla_proj(proj, cc, ss, qn, kvn, wq, wkv, tm=512):
    n = proj.shape[0]
    const = lambda i: (0, 0)
    return pl.pallas_call(
        _mla_proj_kernel,
        grid=(n // tm,),
        in_specs=[
            pl.BlockSpec((tm, Q_LORA), lambda i: (i, BLK512_CQ)),
            pl.BlockSpec((tm, KV_LORA), lambda i: (i, BLK512_CKV)),
            pl.BlockSpec((tm, LANES), lambda i: (i, BLK_KRA)),
            pl.BlockSpec((tm, LANES), lambda i: (i, BLK_KRB)),
            pl.BlockSpec((tm, LANES), lambda i: (i, 0)),
            pl.BlockSpec((tm, LANES), lambda i: (i, 0)),
            pl.BlockSpec((1, Q_LORA), const),
            pl.BlockSpec((1, KV_LORA), const),
            pl.BlockSpec(wq.shape, const),
            pl.BlockSpec(wkv.shape, const),
        ],
        out_specs=[
            pl.BlockSpec((B_HEADS, MLA_QK, tm), lambda i: (0, 0, i)),
            pl.BlockSpec((tm, B_HEADS * MLA_QK), lambda i: (i, 0)),
            pl.BlockSpec((B_HEADS, tm // BLOCK, V_DIM, BLOCK), lambda i: (0, i, 0, 0)),
        ],
        out_shape=[jax.ShapeDtypeStruct((B_HEADS, MLA_QK, n), BF16),
                   jax.ShapeDtypeStruct((n, B_HEADS * MLA_QK), BF16),
                   jax.ShapeDtypeStruct((B_HEADS, n // BLOCK, V_DIM, BLOCK), BF16)],
        compiler_params=_params(("parallel",)),
        name="mla_proj",
    )(proj, proj, proj, proj, cc, ss, qn, kvn, wq, wkv)


MLA_HEADS_PER_STEP = 2


def _mla_attn_kernel(qt_ref, k_ref, vt_ref, o_ref, lg_ref, m_ref, l_ref, acc_ref):
    qi = pl.program_id(2)
    blk = BLOCK
    heads = MLA_HEADS_PER_STEP
    srow = lax.broadcasted_iota(I32, (blk, blk), 0)
    tcol = lax.broadcasted_iota(I32, (blk, blk), 1)

    def logits(c):
        s0 = pl.multiple_of(c * blk, blk)
        return [jnp.dot(k_ref[0, pl.ds(s0, blk), h * MLA_QK:(h + 1) * MLA_QK], qt_ref[h],
                        preferred_element_type=F32) for h in range(heads)]

    def step(c, diagonal, has_next):
        slot = c % 2
        nxt = logits(c + 1) if has_next else None
        for h in range(heads):
            lg = lg_ref[slot, h]
            if diagonal:
                lg = jnp.where(srow <= tcol, lg, -jnp.inf)
            m_old = m_ref[h]
            m_new = jnp.maximum(m_old, jnp.max(lg, axis=0, keepdims=True))
            alpha = jnp.exp2(m_old - m_new)
            p = jnp.exp2(lg - m_new)
            l_ref[h] = l_ref[h] * alpha + jnp.sum(p, axis=0, keepdims=True)
            acc_ref[h] = acc_ref[h] * alpha + jnp.dot(vt_ref[h, c], p.astype(BF16), preferred_element_type=F32)
            m_ref[h] = m_new
        if has_next:
            for h in range(heads):
                lg_ref[1 - slot, h] = nxt[h]

    m_ref[...] = jnp.full_like(m_ref, -jnp.inf)
    l_ref[...] = jnp.zeros_like(l_ref)
    acc_ref[...] = jnp.zeros_like(acc_ref)
    first = logits(0)
    for h in range(heads):
        lg_ref[0, h] = first[h]

    def body(c, carry):
        step(c, False, True)
        return carry

    lax.fori_loop(0, qi, body, 0)
    step(qi, True, False)
    for h in range(heads):
        o_ref[0, :, h * V_DIM:(h + 1) * V_DIM] = (acc_ref[h] / l_ref[h]).T.astype(BF16)


def _mla_attn(qt, k3, vt):
    bsz, seq, _ = k3.shape
    blk = BLOCK
    nq = seq // blk
    heads = MLA_HEADS_PER_STEP
    return pl.pallas_call(
        _mla_attn_kernel,
        grid=(bsz, B_HEADS // heads, nq),
        in_specs=[
            pl.BlockSpec((heads, MLA_QK, blk), lambda b, h, i: (h, 0, b * nq + i)),
            pl.BlockSpec((1, seq, heads * MLA_QK), lambda b, h, i: (b, 0, h)),
            pl.BlockSpec((heads, nq, V_DIM, blk), lambda b, h, i: (h, b, 0, 0)),
        ],
        out_specs=pl.BlockSpec((1, blk, heads * V_DIM), lambda b, h, i: (b, i, h)),
        out_shape=jax.ShapeDtypeStruct((bsz, seq, B_WIDTH), BF16),
        scratch_shapes=[
            pltpu.VMEM((2, heads, blk, blk), F32),
            pltpu.VMEM((heads, 1, blk), F32),
            pltpu.VMEM((heads, 1, blk), F32),
            pltpu.VMEM((heads, V_DIM, blk), F32),
        ],
        compiler_params=_params(("parallel", "parallel", "arbitrary")),
        name="mla_attn",
    )(qt, k3, vt)


def _mem_attn_kernel(q_ref, kv_ref, o_ref):
    scale = HEAD_DIM ** -0.5
    for h in range(C_HEADS):
        q = q_ref[0, :, h * LANES:(h + 1) * LANES]
        k = kv_ref[0, :, h * LANES:(h + 1) * LANES]
        v = kv_ref[0, :, C_WIDTH + h * LANES:C_WIDTH + (h + 1) * LANES]
        s = lax.dot_general(q, k, (((1,), (1,)), ((), ())), preferred_element_type=F32) * scale
        p = jnp.exp(s - jnp.max(s, axis=-1, keepdims=True))
        l = jnp.sum(p, axis=-1, keepdims=True)
        o = jnp.dot(p.astype(BF16), v, preferred_element_type=F32) / l
        o_ref[0, :, h * LANES:(h + 1) * LANES] = o.astype(BF16)


def _mem_attn(proj3, mkv3, tq=512):
    bsz, seq, _ = proj3.shape
    mlen = mkv3.shape[1]
    return pl.pallas_call(
        _mem_attn_kernel,
        grid=(bsz, seq // tq),
        in_specs=[pl.BlockSpec((1, tq, C_WIDTH), lambda b, i: (b, i, BLK512_C)),
                  pl.BlockSpec((1, mlen, 2 * C_WIDTH), lambda b, i: (b, 0, 0))],
        out_specs=pl.BlockSpec((1, tq, C_WIDTH), lambda b, i: (b, i, 0)),
        out_shape=jax.ShapeDtypeStruct((bsz, seq, C_WIDTH), BF16),
        compiler_params=_params(("parallel", "arbitrary")),
        name="mem_attn",
    )(proj3, mkv3)


def _mix_out_kernel(oa_ref, ob_ref, oc_ref, ga_ref, gb_ref, gc_ref, x_ref, wb_ref, wo_ref, g_ref, b_ref,
                    o_ref, obf_ref, *, alpha):
    def gated(o_blk, gate_ref, r0, r1):
        y = jnp.dot(o_blk, wb_ref[r0:r1, :], preferred_element_type=F32)
        return y * (1.0 / (1.0 + jnp.exp(-gate_ref[...].astype(F32))))

    y = gated(oa_ref[...], ga_ref, 0, A_WIDTH)
    y = y + gated(ob_ref[...], gb_ref, A_WIDTH, A_WIDTH + B_WIDTH)
    y = y + gated(oc_ref[...], gc_ref, A_WIDTH + B_WIDTH, A_WIDTH + B_WIDTH + C_WIDTH)
    mix = jnp.dot(y.astype(BF16), wo_ref[...], preferred_element_type=F32)
    out = _layer_norm_rows(alpha * x_ref[...] + mix, g_ref[...], b_ref[...])
    o_ref[...] = out
    obf_ref[...] = out.astype(BF16)


def _mix_out(oa, ob, oc, proj, x, wb, wo, g, b, alpha, tm=256):
    n = x.shape[0]
    const = lambda i: (0, 0)
    return pl.pallas_call(
        functools.partial(_mix_out_kernel, alpha=alpha),
        grid=(n // tm,),
        in_specs=[
            pl.BlockSpec((tm, A_WIDTH), lambda i: (i, 0)),
            pl.BlockSpec((tm, B_WIDTH), lambda i: (i, 0)),
            pl.BlockSpec((tm, C_WIDTH), lambda i: (i, 0)),
            pl.BlockSpec((tm, D_MODEL), lambda i: (i, BLK2048_GATE0)),
            pl.BlockSpec((tm, D_MODEL), lambda i: (i, BLK2048_GATE0 + 1)),
            pl.BlockSpec((tm, D_MODEL), lambda i: (i, BLK2048_GATE0 + 2)),
            pl.BlockSpec((tm, D_MODEL), lambda i: (i, 0)),
            pl.BlockSpec(wb.shape, const),
            pl.BlockSpec(wo.shape, const),
            pl.BlockSpec((1, D_MODEL), const),
            pl.BlockSpec((1, D_MODEL), const),
        ],
        out_specs=[pl.BlockSpec((tm, D_MODEL), lambda i: (i, 0)),
                   pl.BlockSpec((tm, D_MODEL), lambda i: (i, 0))],
        out_shape=[jax.ShapeDtypeStruct((n, D_MODEL), F32),
                   jax.ShapeDtypeStruct((n, D_MODEL), BF16)],
        compiler_params=_params(("parallel",)),
        name="mix_out",
    )(oa, ob, oc, proj, proj, proj, x, wb, wo, g, b)


def _pack_w_in(w):
    offs = [0]
    for s in IN_SIZES:
        offs.append(offs[-1] + s)
    seg = lambda k: w[:, offs[k]:offs[k + 1]]
    d = w.shape[0]
    zeros = lambda c: jnp.zeros((d, c), w.dtype)
    ik, iw, kr = seg(4), seg(5), seg(8)
    half = ROPE_DIM // 2
    packed = jnp.concatenate([
        w[:, :offs[4]],
        ik, ik,
        iw, zeros(LANES - IDX_HEADS),
        kr, zeros(LANES - ROPE_DIM),
        kr[:, half:], kr[:, :half], zeros(LANES - ROPE_DIM),
        seg(6), seg(7), seg(9), seg(10),
    ], axis=1)
    assert packed.shape[1] == PROJ_COLS
    return packed.astype(BF16)


def _pack_w_uq(w):
    half = ROPE_DIM // 2
    zeros = jnp.zeros((w.shape[0], LANES - ROPE_DIM), w.dtype)
    cols = []
    for h in range(B_HEADS):
        base = h * (NOPE_DIM + ROPE_DIM)
        r = w[:, base + NOPE_DIM:base + NOPE_DIM + ROPE_DIM]
        cols += [w[:, base:base + NOPE_DIM], r, zeros, r[:, half:], r[:, :half], zeros]
    return jnp.concatenate(cols, axis=1).astype(BF16)


def _pack_w_ukv(w):
    step = NOPE_DIM + V_DIM
    ks = [w[:, h * step:h * step + NOPE_DIM] for h in range(B_HEADS)]
    vs = [w[:, h * step + NOPE_DIM:(h + 1) * step] for h in range(B_HEADS)]
    return jnp.concatenate(ks + vs, axis=1).astype(BF16)


def _rel_bucket(dist):
    n = jnp.maximum(dist, 0)
    max_exact = REL_BUCKETS // 2
    nf = jnp.maximum(n, 1).astype(F32)
    large = max_exact + (jnp.log(nf / max_exact) / math.log(REL_MAX_DIST / max_exact)
                         * (REL_BUCKETS - max_exact)).astype(I32)
    large = jnp.minimum(large, REL_BUCKETS - 1)
    return jnp.where(n < max_exact, n, large)


def _bias_tables(rel_bias):
    assert BLOCK + 1 >= REL_MAX_DIST
    s = jnp.arange(BLOCK)[:, None]
    t = jnp.arange(BLOCK)[None, :]
    tiles = []
    for block_gap in (1, 0):
        dist = t - s + block_gap * BLOCK
        tile = (rel_bias[_rel_bucket(dist)] - rel_bias[REL_BUCKETS - 1]) * LOG2E
        tile = jnp.where((dist >= 0)[:, :, None], tile, -jnp.inf)
        tiles.append(jnp.transpose(tile, (2, 0, 1)))
    return jnp.stack(tiles).astype(F32)


def kernel(x, mem, positions, rel_bias, ln_g, ln_b, ffn1_up, ffn1_down, w_in, q_norm, kv_norm, w_uq, w_ukv,
           w_mem_kv, w_branch, w_out, ffn2_up, ffn2_down):
    bsz, seq, d = x.shape
    depth = ffn1_up.shape[0]
    n = bsz * seq
    alpha = (2 * depth) ** 0.25
    k_sel = min(TOPK_MAX, seq // 4)
    assert d == D_MODEL and seq % (2 * BLOCK) == 0

    inv_freq = ROPE_THETA ** (-jnp.arange(0, ROPE_DIM, 2, dtype=F32) / ROPE_DIM)
    ang = positions.astype(F32)[..., None] * inv_freq
    cos, sin = jnp.cos(ang).reshape(n, -1), jnp.sin(ang).reshape(n, -1)
    pad = jnp.zeros((n, LANES - ROPE_DIM), F32)
    cc = jnp.concatenate([cos, cos, pad], axis=1)
    ss = jnp.concatenate([-sin, sin, pad], axis=1)
    bias_tab = _bias_tables(rel_bias)

    xf = x.reshape(n, d)
    xb = xf.astype(BF16)
    memb = mem.reshape(bsz * mem.shape[1], d).astype(BF16)

    for l in range(depth):
        row = lambda v: v.reshape(1, -1)
        xf, xb = _ffn(xb, xf, ffn1_up[l].astype(BF16), ffn1_down[l].astype(BF16),
                      row(ln_g[l, 0]), row(ln_b[l, 0]), alpha)
        proj = _matmul(xb, _pack_w_in(w_in[l]), 1024, 1024, "in_proj")
        proj3 = proj.reshape(bsz, seq, PROJ_COLS)
        o_a = _dsa(proj3, bias_tab, k_sel).reshape(n, A_WIDTH)
        qt, k, vt = _mla_proj(proj, cc, ss, row(q_norm[l]), row(kv_norm[l]), _pack_w_uq(w_uq[l]), _pack_w_ukv(w_ukv[l]))
        o_b = _mla_attn(qt, k.reshape(bsz, seq, -1), vt).reshape(n, B_WIDTH)
        mkv = _matmul(memb, w_mem_kv[l].astype(BF16), 512, 1024, "mem_proj")
        o_c = _mem_attn(proj3, mkv.reshape(bsz, mem.shape[1], 2 * C_WIDTH)).reshape(n, C_WIDTH)
        xf, xb = _mix_out(o_a, o_b, o_c, proj, xf, w_branch[l].astype(BF16), w_out[l].astype(BF16),
                          row(ln_g[l, 1]), row(ln_b[l, 1]), alpha)
        xf, xb = _ffn(xb, xf, ffn2_up[l].astype(BF16), ffn2_down[l].astype(BF16),
                      row(ln_g[l, 2]), row(ln_b[l, 2]), alpha)
    return xf.reshape(bsz, seq, d)
```

```python
import functools
import math

import jax
import jax.numpy as jnp
from jax import lax
from jax.experimental import pallas as pl
from jax.experimental.pallas import tpu as pltpu

F32 = jnp.float32
BF16 = jnp.bfloat16
I32 = jnp.int32

D_MODEL = 2048
D_FF = 5632
HEAD_DIM = 128
A_HEADS = 6
IDX_HEADS = 16
IDX_DIM = 64
TOPK_MAX = 256
B_HEADS = 6
Q_LORA = 512
KV_LORA = 512
NOPE_DIM = 128
ROPE_DIM = 64
V_DIM = 128
ROPE_THETA = 10000.0
C_HEADS = 4
REL_BUCKETS = 32
REL_MAX_DIST = 128
LN_EPS = 1e-5
RMS_EPS = 1e-6
A_WIDTH = A_HEADS * HEAD_DIM
B_WIDTH = B_HEADS * V_DIM
C_WIDTH = C_HEADS * HEAD_DIM
IN_SIZES = (A_WIDTH, HEAD_DIM, HEAD_DIM, IDX_HEADS * IDX_DIM, IDX_DIM, IDX_HEADS,
            Q_LORA, KV_LORA, ROPE_DIM, C_WIDTH, 3 * D_MODEL)

LANES = 128
SUBLANES = 8
BLOCK = 256
MLA_QK = 2 * LANES
INT_MIN = -2 ** 31
LOG2E = 1.4426950408889634

PROJ_COLS = 10240
BLK_AK, BLK_AV = 6, 7
BLK_IK, BLK_IW, BLK_KRA, BLK_KRB = 16, 17, 18, 19
BLK512_CQ, BLK512_CKV, BLK512_C = 5, 6, 7
BLK2048_GATE0 = 2

VMEM_LIMIT = 56 * 1024 * 1024
FFN_VMEM_LIMIT = 60 * 1024 * 1024


def _params(sem, vmem=VMEM_LIMIT):
    return pltpu.CompilerParams(dimension_semantics=sem, vmem_limit_bytes=vmem)


def _layer_norm_rows(y, g, b):
    mu = jnp.mean(y, axis=-1, keepdims=True)
    d = y - mu
    var = jnp.mean(d * d, axis=-1, keepdims=True)
    return d * lax.rsqrt(var + LN_EPS) * g + b


def _ffn_kernel(x_ref, wg_ref, wu_ref, wd_ref, g_ref, b_ref, o_ref, xb_ref, *, alpha):
    j = pl.program_id(1)

    @pl.when(j == 0)
    def _():
        xb_ref[...] = x_ref[...].astype(BF16)
        o_ref[...] = jnp.zeros_like(o_ref)

    xb = xb_ref[...]
    gate = jnp.dot(xb, wg_ref[0].astype(BF16), preferred_element_type=F32)
    up = jnp.dot(xb, wu_ref[0].astype(BF16), preferred_element_type=F32)
    h = (gate * (1.0 / (1.0 + jnp.exp(-gate))) * up).astype(BF16)
    o_ref[...] += jnp.dot(h, wd_ref[0].astype(BF16), preferred_element_type=F32)

    @pl.when(j == pl.num_programs(1) - 1)
    def _():
        y = alpha * x_ref[...] + 0.5 * o_ref[...]
        o_ref[...] = _layer_norm_rows(y, g_ref[0], b_ref[0])


def _ffn(x, w_up, w_down, ln_g, ln_b, layer, which, alpha, tm=1024, tf=256):
    n = x.shape[0]
    tm = min(tm, n)
    nf = D_FF // tf
    return pl.pallas_call(
        functools.partial(_ffn_kernel, alpha=alpha),
        grid=(n // tm, nf),
        in_specs=[
            pl.BlockSpec((tm, D_MODEL), lambda i, j: (i, 0)),
            pl.BlockSpec((1, D_MODEL, tf), lambda i, j: (layer, 0, j)),
            pl.BlockSpec((1, D_MODEL, tf), lambda i, j: (layer, 0, j + nf)),
            pl.BlockSpec((1, tf, D_MODEL), lambda i, j: (layer, j, 0)),
            pl.BlockSpec((1, 1, D_MODEL), lambda i, j: (3 * layer + which, 0, 0)),
            pl.BlockSpec((1, 1, D_MODEL), lambda i, j: (3 * layer + which, 0, 0)),
        ],
        out_specs=pl.BlockSpec((tm, D_MODEL), lambda i, j: (i, 0)),
        out_shape=jax.ShapeDtypeStruct((n, D_MODEL), F32),
        scratch_shapes=[pltpu.VMEM((tm, D_MODEL), BF16)],
        compiler_params=_params(("parallel", "arbitrary"), FFN_VMEM_LIMIT),
        name="ffn",
    )(x, w_up, w_up, w_down, ln_g, ln_b)


def _proj_kernel(x_ref, w_ref, o_ref, xb_ref):
    @pl.when(pl.program_id(1) == 0)
    def _():
        xb_ref[...] = x_ref[...].astype(BF16)

    o_ref[...] = jnp.dot(xb_ref[...], w_ref[0].astype(BF16), preferred_element_type=F32).astype(o_ref.dtype)


def _proj(x, w, layer, tm, tn, name):
    n, k = x.shape
    tm = min(tm, n)
    cols = w.shape[2]
    return pl.pallas_call(
        _proj_kernel,
        grid=(n // tm, cols // tn),
        in_specs=[pl.BlockSpec((tm, k), lambda i, j: (i, 0)),
                  pl.BlockSpec((1, k, tn), lambda i, j: (layer, 0, j))],
        out_specs=pl.BlockSpec((tm, tn), lambda i, j: (i, j)),
        out_shape=jax.ShapeDtypeStruct((n, cols), BF16),
        scratch_shapes=[pltpu.VMEM((tm, k), BF16)],
        compiler_params=_params(("parallel", "arbitrary")),
        name=name,
    )(x, w)


def _dsa_kernel(aq_ref, iq_ref, iw_ref, ik_ref, ak_ref, av_ref, bias_ref, o_ref,
                skey_ref, ika_ref, ikb_ref, vat_ref, m_ref, l_ref, acc_ref, lg_ref, *, k_sel, seq):
    i = pl.program_id(1)
    blk = BLOCK

    @pl.when(i == 0)
    def _():
        ik = ik_ref[0]
        lane = lax.broadcasted_iota(I32, ik.shape, 1)
        zero = jnp.zeros_like(ik)
        ika_ref[...] = jnp.where(lane < IDX_DIM, ik, zero)
        ikb_ref[...] = jnp.where(lane >= IDX_DIM, ik, zero)
        for c in range(seq // blk):
            vat_ref[c] = av_ref[0, c * blk:(c + 1) * blk, :].astype(F32).T.astype(BF16)

    iqt = iq_ref[0].astype(F32).T.astype(BF16)
    wt = iw_ref[0].astype(F32).T * (IDX_HEADS * IDX_DIM) ** -0.5

    srow = lax.broadcasted_iota(I32, (blk, blk), 0)
    tcol = lax.broadcasted_iota(I32, (blk, blk), 1)

    def score_body(c, carry):
        s0 = pl.multiple_of(c * blk, blk)
        ka = ika_ref[pl.ds(s0, blk), :]
        kb = ikb_ref[pl.ds(s0, blk), :]
        sc = jnp.zeros((blk, blk), F32)
        for j in range(IDX_HEADS // 2):
            rhs = iqt[j * LANES:(j + 1) * LANES, :]
            sc = sc + wt[2 * j:2 * j + 1, :] * jnp.maximum(jnp.dot(ka, rhs, preferred_element_type=F32), 0.0)
            sc = sc + wt[2 * j + 1:2 * j + 2, :] * jnp.maximum(jnp.dot(kb, rhs, preferred_element_type=F32), 0.0)
        bits = pltpu.bitcast(sc, I32)
        key = bits ^ ((bits >> 31) & jnp.int32(0x7FFFFFFF))
        causal = srow <= tcol + jnp.where(c < i, jnp.int32(blk), jnp.int32(0))
        skey_ref[pl.ds(s0, blk), :] = jnp.where(causal, key, jnp.int32(INT_MIN))
        return carry

    lax.fori_loop(0, i + 1, score_body, 0)

    n_acc = 4

    def bit_body(it, u):
        bit = jnp.left_shift(jnp.int32(1), jnp.int32(31) - it)
        trial_u = u | bit
        trial = jnp.broadcast_to(trial_u ^ jnp.int32(INT_MIN), (SUBLANES, blk))

        def cnt_body(c, accs):
            keys = skey_ref[pl.ds(pl.multiple_of(c * blk, blk), blk), :]
            accs = list(accs)
            for r in range(blk // SUBLANES):
                rows = keys[r * SUBLANES:(r + 1) * SUBLANES, :]
                accs[r % n_acc] = accs[r % n_acc] + jnp.where(rows >= trial, 1.0, 0.0)
            return tuple(accs)

        zero = jnp.zeros((SUBLANES, blk), F32)
        accs = lax.fori_loop(0, i + 1, cnt_body, (zero,) * n_acc)
        cnt = jnp.sum((accs[0] + accs[1]) + (accs[2] + accs[3]), axis=0, keepdims=True)
        return jnp.where(cnt >= float(k_sel), trial_u, u)

    thr = lax.fori_loop(0, 32, bit_body, jnp.zeros((1, blk), I32)) ^ jnp.int32(INT_MIN)

    qt = (aq_ref[0].astype(F32) * (HEAD_DIM ** -0.5 * LOG2E)).T.astype(BF16)

    m_ref[...] = jnp.full_like(m_ref, -jnp.inf)
    l_ref[...] = jnp.zeros_like(l_ref)
    acc_ref[...] = jnp.zeros_like(acc_ref)

    def logits(c):
        ka = ak_ref[0, pl.ds(pl.multiple_of(c * blk, blk), blk), :]
        return [jnp.dot(ka, qt[h * LANES:(h + 1) * LANES, :], preferred_element_type=F32) for h in range(A_HEADS)]

    def att_step(c, bias_idx, has_next):
        slot = c % 2
        nxt = logits(c + 1) if has_next else None
        sel = skey_ref[pl.ds(pl.multiple_of(c * blk, blk), blk), :] >= thr
        vt = vat_ref[c]
        for h in range(A_HEADS):
            lg = lg_ref[slot, h]
            if bias_idx is not None:
                lg = lg + bias_ref[bias_idx, h]
            lg = jnp.where(sel, lg, -jnp.inf)
            m_old = m_ref[h]
            m_new = jnp.maximum(m_old, jnp.max(lg, axis=0, keepdims=True))
            m_use = jnp.where(m_new == -jnp.inf, 0.0, m_new)
            alpha = jnp.exp2(m_old - m_use)
            p = jnp.exp2(lg - m_use)
            l_ref[h] = l_ref[h] * alpha + jnp.sum(p, axis=0, keepdims=True)
            acc_ref[h] = acc_ref[h] * alpha + jnp.dot(vt, p.astype(BF16), preferred_element_type=F32)
            m_ref[h] = m_new
        if has_next:
            for h in range(A_HEADS):
                lg_ref[1 - slot, h] = nxt[h]

    first = logits(0)
    for h in range(A_HEADS):
        lg_ref[0, h] = first[h]

    def far_body(c, carry):
        att_step(c, None, True)
        return carry

    lax.fori_loop(0, jnp.maximum(i - 1, 0), far_body, 0)

    @pl.when(i >= 1)
    def _():
        att_step(i - 1, 0, True)

    att_step(i, 1, False)

    for h in range(A_HEADS):
        o_ref[0, :, h * LANES:(h + 1) * LANES] = (acc_ref[h] / l_ref[h]).T.astype(BF16)


def _dsa(proj3, bias_tab, k_sel):
    bsz, seq, _ = proj3.shape
    blk = BLOCK
    return pl.pallas_call(
        functools.partial(_dsa_kernel, k_sel=k_sel, seq=seq),
        grid=(bsz, seq // blk),
        in_specs=[
            pl.BlockSpec((1, blk, A_WIDTH), lambda b, i: (b, i, 0)),
            pl.BlockSpec((1, blk, IDX_HEADS * IDX_DIM), lambda b, i: (b, i, 1)),
            pl.BlockSpec((1, blk, LANES), lambda b, i: (b, i, BLK_IW)),
            pl.BlockSpec((1, seq, LANES), lambda b, i: (b, 0, BLK_IK)),
            pl.BlockSpec((1, seq, LANES), lambda b, i: (b, 0, BLK_AK)),
            pl.BlockSpec((1, seq, LANES), lambda b, i: (b, 0, BLK_AV)),
            pl.BlockSpec(bias_tab.shape, lambda b, i: (0, 0, 0, 0)),
        ],
        out_specs=pl.BlockSpec((1, blk, A_WIDTH), lambda b, i: (b, i, 0)),
        out_shape=jax.ShapeDtypeStruct((bsz, seq, A_WIDTH), BF16),
        scratch_shapes=[
            pltpu.VMEM((seq, blk), I32),
            pltpu.VMEM((seq, LANES), BF16),
            pltpu.VMEM((seq, LANES), BF16),
            pltpu.VMEM((seq // blk, HEAD_DIM, blk), BF16),
            pltpu.VMEM((A_HEADS, 1, blk), F32),
            pltpu.VMEM((A_HEADS, 1, blk), F32),
            pltpu.VMEM((A_HEADS, HEAD_DIM, blk), F32),
            pltpu.VMEM((2, A_HEADS, blk, blk), F32),
        ],
        compiler_params=_params(("parallel", "arbitrary")),
        name="dsa",
    )(proj3, proj3, proj3, proj3, proj3, proj3, bias_tab)


def _rms_rows(x, g):
    return x * lax.rsqrt(jnp.mean(x * x, axis=-1, keepdims=True) + RMS_EPS) * g


def _mla_proj_kernel(cq_ref, ckv_ref, kra_ref, krb_ref, cc_ref, ss_ref, qn_ref, kvn_ref, wq_ref, wkv_ref,
                     qt_ref, k_ref, vt_ref):
    tm = cq_ref.shape[0]
    cc = cc_ref[...]
    ss = ss_ref[...]
    qscale = (NOPE_DIM + ROPE_DIM) ** -0.5 * LOG2E
    cq = _rms_rows(cq_ref[...].astype(F32), qn_ref[0]).astype(BF16)
    q3 = jnp.dot(cq, wq_ref[0], preferred_element_type=F32)
    for h in range(B_HEADS):
        base = 3 * LANES * h
        rot = q3[:, base + LANES:base + 2 * LANES] * cc + q3[:, base + 2 * LANES:base + 3 * LANES] * ss
        qt_ref[h, 0:LANES, :] = (q3[:, base:base + LANES] * qscale).T.astype(BF16)
        qt_ref[h, LANES:MLA_QK, :] = (rot * qscale).T.astype(BF16)
    ckv = _rms_rows(ckv_ref[...].astype(F32), kvn_ref[0]).astype(BF16)
    kv = jnp.dot(ckv, wkv_ref[0], preferred_element_type=F32)
    krot = (kra_ref[...].astype(F32) * cc + krb_ref[...].astype(F32) * ss).astype(BF16)
    for h in range(B_HEADS):
        k_ref[:, MLA_QK * h:MLA_QK * h + LANES] = kv[:, h * LANES:(h + 1) * LANES].astype(BF16)
        k_ref[:, MLA_QK * h + LANES:MLA_QK * (h + 1)] = krot
        v_h = kv[:, B_HEADS * NOPE_DIM + h * V_DIM:B_HEADS * NOPE_DIM + (h + 1) * V_DIM]
        for c in range(tm // BLOCK):
            vt_ref[h, c] = v_h[c * BLOCK:(c + 1) * BLOCK, :].T.astype(BF16)


def _mla_proj(proj, cc, ss, qn, kvn, wq, wkv, layer, tm=512):
    n = proj.shape[0]
    per_layer = lambda i: (layer, 0, 0)
    return pl.pallas_call(
        _mla_proj_kernel,
        grid=(n // tm,),
        in_specs=[
            pl.BlockSpec((tm, Q_LORA), lambda i: (i, BLK512_CQ)),
            pl.BlockSpec((tm, KV_LORA), lambda i: (i, BLK512_CKV)),
            pl.BlockSpec((tm, LANES), lambda i: (i, BLK_KRA)),
            pl.BlockSpec((tm, LANES), lambda i: (i, BLK_KRB)),
            pl.BlockSpec((tm, LANES), lambda i: (i, 0)),
            pl.BlockSpec((tm, LANES), lambda i: (i, 0)),
            pl.BlockSpec((1, 1, Q_LORA), per_layer),
            pl.BlockSpec((1, 1, KV_LORA), per_layer),
            pl.BlockSpec((1,) + wq.shape[1:], per_layer),
            pl.BlockSpec((1,) + wkv.shape[1:], per_layer),
        ],
        out_specs=[
            pl.BlockSpec((B_HEADS, MLA_QK, tm), lambda i: (0, 0, i)),
            pl.BlockSpec((tm, B_HEADS * MLA_QK), lambda i: (i, 0)),
            pl.BlockSpec((B_HEADS, tm // BLOCK, V_DIM, BLOCK), lambda i: (0, i, 0, 0)),
        ],
        out_shape=[jax.ShapeDtypeStruct((B_HEADS, MLA_QK, n), BF16),
                   jax.ShapeDtypeStruct((n, B_HEADS * MLA_QK), BF16),
                   jax.ShapeDtypeStruct((B_HEADS, n // BLOCK, V_DIM, BLOCK), BF16)],
        compiler_params=_params(("parallel",)),
        name="mla_proj",
    )(proj, proj, proj, proj, cc, ss, qn, kvn, wq, wkv)


MLA_HEADS_PER_STEP = 2


def _mla_attn_kernel(qt_ref, k_ref, vt_ref, o_ref, lg_ref, m_ref, l_ref, acc_ref):
    qi = pl.program_id(2)
    blk = BLOCK
    heads = MLA_HEADS_PER_STEP
    srow = lax.broadcasted_iota(I32, (blk, blk), 0)
    tcol = lax.broadcasted_iota(I32, (blk, blk), 1)

    def logits(c):
        s0 = pl.multiple_of(c * blk, blk)
        return [jnp.dot(k_ref[0, pl.ds(s0, blk), h * MLA_QK:(h + 1) * MLA_QK], qt_ref[h],
                        preferred_element_type=F32) for h in range(heads)]

    def step(c, diagonal, has_next):
        slot = c % 2
        nxt = logits(c + 1) if has_next else None
        for h in range(heads):
            lg = lg_ref[slot, h]
            if diagonal:
                lg = jnp.where(srow <= tcol, lg, -jnp.inf)
            m_old = m_ref[h]
            m_new = jnp.maximum(m_old, jnp.max(lg, axis=0, keepdims=True))
            alpha = jnp.exp2(m_old - m_new)
            p = jnp.exp2(lg - m_new)
            l_ref[h] = l_ref[h] * alpha + jnp.sum(p, axis=0, keepdims=True)
            acc_ref[h] = acc_ref[h] * alpha + jnp.dot(vt_ref[h, c], p.astype(BF16), preferred_element_type=F32)
            m_ref[h] = m_new
        if has_next:
            for h in range(heads):
                lg_ref[1 - slot, h] = nxt[h]

    m_ref[...] = jnp.full_like(m_ref, -jnp.inf)
    l_ref[...] = jnp.zeros_like(l_ref)
    acc_ref[...] = jnp.zeros_like(acc_ref)
    first = logits(0)
    for h in range(heads):
        lg_ref[0, h] = first[h]

    def body(c, carry):
        step(c, False, True)
        return carry

    lax.fori_loop(0, qi, body, 0)
    step(qi, True, False)
    for h in range(heads):
        o_ref[0, :, h * V_DIM:(h + 1) * V_DIM] = (acc_ref[h] / l_ref[h]).T.astype(BF16)


def _mla_attn(qt, k3, vt):
    bsz, seq, _ = k3.shape
    blk = BLOCK
    nq = seq // blk
    heads = MLA_HEADS_PER_STEP
    return pl.pallas_call(
        _mla_attn_kernel,
        grid=(bsz, B_HEADS // heads, nq),
        in_specs=[
            pl.BlockSpec((heads, MLA_QK, blk), lambda b, h, i: (h, 0, b * nq + i)),
            pl.BlockSpec((1, seq, heads * MLA_QK), lambda b, h, i: (b, 0, h)),
            pl.BlockSpec((heads, nq, V_DIM, blk), lambda b, h, i: (h, b, 0, 0)),
        ],
        out_specs=pl.BlockSpec((1, blk, heads * V_DIM), lambda b, h, i: (b, i, h)),
        out_shape=jax.ShapeDtypeStruct((bsz, seq, B_WIDTH), BF16),
        scratch_shapes=[
            pltpu.VMEM((2, heads, blk, blk), F32),
            pltpu.VMEM((heads, 1, blk), F32),
            pltpu.VMEM((heads, 1, blk), F32),
            pltpu.VMEM((heads, V_DIM, blk), F32),
        ],
        compiler_params=_params(("parallel", "parallel", "arbitrary")),
        name="mla_attn",
    )(qt, k3, vt)


def _mem_attn_kernel(q_ref, kv_ref, o_ref):
    scale = HEAD_DIM ** -0.5
    for h in range(C_HEADS):
        q = q_ref[0, :, h * LANES:(h + 1) * LANES]
        k = kv_ref[0, :, h * LANES:(h + 1) * LANES]
        v = kv_ref[0, :, C_WIDTH + h * LANES:C_WIDTH + (h + 1) * LANES]
        s = lax.dot_general(q, k, (((1,), (1,)), ((), ())), preferred_element_type=F32) * scale
        p = jnp.exp(s - jnp.max(s, axis=-1, keepdims=True))
        l = jnp.sum(p, axis=-1, keepdims=True)
        o = jnp.dot(p.astype(BF16), v, preferred_element_type=F32) / l
        o_ref[0, :, h * LANES:(h + 1) * LANES] = o.astype(BF16)


def _mem_attn(proj3, mkv3, tq=512):
    bsz, seq, _ = proj3.shape
    mlen = mkv3.shape[1]
    return pl.pallas_call(
        _mem_attn_kernel,
        grid=(bsz, seq // tq),
        in_specs=[pl.BlockSpec((1, tq, C_WIDTH), lambda b, i: (b, i, BLK512_C)),
                  pl.BlockSpec((1, mlen, 2 * C_WIDTH), lambda b, i: (b, 0, 0))],
        out_specs=pl.BlockSpec((1, tq, C_WIDTH), lambda b, i: (b, i, 0)),
        out_shape=jax.ShapeDtypeStruct((bsz, seq, C_WIDTH), BF16),
        compiler_params=_params(("parallel", "arbitrary")),
        name="mem_attn",
    )(proj3, mkv3)


def _mix_out_kernel(oa_ref, ob_ref, oc_ref, ga_ref, gb_ref, gc_ref, x_ref, wb_ref, wo_ref, g_ref, b_ref,
                    o_ref, *, alpha):
    def gated(o_blk, gate_ref, r0, r1):
        y = jnp.dot(o_blk, wb_ref[0, r0:r1, :], preferred_element_type=F32)
        return y * (1.0 / (1.0 + jnp.exp(-gate_ref[...].astype(F32))))

    y = gated(oa_ref[...], ga_ref, 0, A_WIDTH)
    y = y + gated(ob_ref[...], gb_ref, A_WIDTH, A_WIDTH + B_WIDTH)
    y = y + gated(oc_ref[...], gc_ref, A_WIDTH + B_WIDTH, A_WIDTH + B_WIDTH + C_WIDTH)
    mix = jnp.dot(y.astype(BF16), wo_ref[0], preferred_element_type=F32)
    o_ref[...] = _layer_norm_rows(alpha * x_ref[...] + mix, g_ref[0], b_ref[0])


def _mix_out(oa, ob, oc, proj, x, wb, wo, ln_g, ln_b, layer, alpha, tm=256):
    n = x.shape[0]
    per_layer = lambda i: (layer, 0, 0)
    ln_row = lambda i: (3 * layer + 1, 0, 0)
    return pl.pallas_call(
        functools.partial(_mix_out_kernel, alpha=alpha),
        grid=(n // tm,),
        in_specs=[
            pl.BlockSpec((tm, A_WIDTH), lambda i: (i, 0)),
            pl.BlockSpec((tm, B_WIDTH), lambda i: (i, 0)),
            pl.BlockSpec((tm, C_WIDTH), lambda i: (i, 0)),
            pl.BlockSpec((tm, D_MODEL), lambda i: (i, BLK2048_GATE0)),
            pl.BlockSpec((tm, D_MODEL), lambda i: (i, BLK2048_GATE0 + 1)),
            pl.BlockSpec((tm, D_MODEL), lambda i: (i, BLK2048_GATE0 + 2)),
            pl.BlockSpec((tm, D_MODEL), lambda i: (i, 0)),
            pl.BlockSpec((1,) + wb.shape[1:], per_layer),
            pl.BlockSpec((1,) + wo.shape[1:], per_layer),
            pl.BlockSpec((1, 1, D_MODEL), ln_row),
            pl.BlockSpec((1, 1, D_MODEL), ln_row),
        ],
        out_specs=pl.BlockSpec((tm, D_MODEL), lambda i: (i, 0)),
        out_shape=jax.ShapeDtypeStruct((n, D_MODEL), F32),
        compiler_params=_params(("parallel",)),
        name="mix_out",
    )(oa, ob, oc, proj, proj, proj, x, wb, wo, ln_g, ln_b)


def _pack_w_in(w):
    offs = [0]
    for s in IN_SIZES:
        offs.append(offs[-1] + s)
    seg = lambda k: w[..., offs[k]:offs[k + 1]]
    zeros = lambda c: jnp.zeros(w.shape[:-1] + (c,), w.dtype)
    ik, iw, kr = seg(4), seg(5), seg(8)
    half = ROPE_DIM // 2
    packed = jnp.concatenate([
        w[..., :offs[4]],
        ik, ik,
        iw, zeros(LANES - IDX_HEADS),
        kr, zeros(LANES - ROPE_DIM),
        kr[..., half:], kr[..., :half], zeros(LANES - ROPE_DIM),
        seg(6), seg(7), seg(9), seg(10),
    ], axis=-1)
    assert packed.shape[-1] == PROJ_COLS
    return packed.astype(BF16)


def _pack_w_uq(w):
    half = ROPE_DIM // 2
    zeros = jnp.zeros(w.shape[:-1] + (LANES - ROPE_DIM,), w.dtype)
    cols = []
    for h in range(B_HEADS):
        base = h * (NOPE_DIM + ROPE_DIM)
        r = w[..., base + NOPE_DIM:base + NOPE_DIM + ROPE_DIM]
        cols += [w[..., base:base + NOPE_DIM], r, zeros, r[..., half:], r[..., :half], zeros]
    return jnp.concatenate(cols, axis=-1).astype(BF16)


def _pack_w_ukv(w):
    step = NOPE_DIM + V_DIM
    ks = [w[..., h * step:h * step + NOPE_DIM] for h in range(B_HEADS)]
    vs = [w[..., h * step + NOPE_DIM:(h + 1) * step] for h in range(B_HEADS)]
    return jnp.concatenate(ks + vs, axis=-1).astype(BF16)


def _rel_bucket(dist):
    n = jnp.maximum(dist, 0)
    max_exact = REL_BUCKETS // 2
    nf = jnp.maximum(n, 1).astype(F32)
    large = max_exact + (jnp.log(nf / max_exact) / math.log(REL_MAX_DIST / max_exact)
                         * (REL_BUCKETS - max_exact)).astype(I32)
    large = jnp.minimum(large, REL_BUCKETS - 1)
    return jnp.where(n < max_exact, n, large)


def _bias_tables(rel_bias):
    assert BLOCK + 1 >= REL_MAX_DIST
    s = jnp.arange(BLOCK)[:, None]
    t = jnp.arange(BLOCK)[None, :]
    tiles = []
    for block_gap in (1, 0):
        dist = t - s + block_gap * BLOCK
        onehot = (_rel_bucket(dist)[:, :, None] == jnp.arange(REL_BUCKETS)).astype(F32)
        looked_up = jnp.einsum("stb,bh->sth", onehot, rel_bias, precision=lax.Precision.HIGHEST)
        tile = (looked_up - rel_bias[REL_BUCKETS - 1]) * LOG2E
        tile = jnp.where((dist >= 0)[:, :, None], tile, -jnp.inf)
        tiles.append(jnp.transpose(tile, (2, 0, 1)))
    return jnp.stack(tiles).astype(F32)


def kernel(x, mem, positions, rel_bias, ln_g, ln_b, ffn1_up, ffn1_down, w_in, q_norm, kv_norm, w_uq, w_ukv,
           w_mem_kv, w_branch, w_out, ffn2_up, ffn2_down):
    bsz, seq, d = x.shape
    depth = ffn1_up.shape[0]
    n = bsz * seq
    alpha = (2 * depth) ** 0.25
    k_sel = min(TOPK_MAX, seq // 4)
    assert d == D_MODEL and seq % (2 * BLOCK) == 0

    inv_freq = ROPE_THETA ** (-jnp.arange(0, ROPE_DIM, 2, dtype=F32) / ROPE_DIM)
    ang = positions.astype(F32)[..., None] * inv_freq
    cos, sin = jnp.cos(ang).reshape(n, -1), jnp.sin(ang).reshape(n, -1)
    pad = jnp.zeros((n, LANES - ROPE_DIM), F32)
    cc = jnp.concatenate([cos, cos, pad], axis=1)
    ss = jnp.concatenate([-sin, sin, pad], axis=1)
    bias_tab = _bias_tables(rel_bias)

    xf = x.reshape(n, d)
    memf = mem.reshape(bsz * mem.shape[1], d)
    ln_g3 = ln_g.reshape(depth * 3, 1, d)
    ln_b3 = ln_b.reshape(depth * 3, 1, d)
    qn3 = q_norm.reshape(depth, 1, Q_LORA)
    kvn3 = kv_norm.reshape(depth, 1, KV_LORA)
    w_in_p = _pack_w_in(w_in)
    w_uq_p = _pack_w_uq(w_uq)
    w_ukv_p = _pack_w_ukv(w_ukv)
    w_branch_b = w_branch.astype(BF16)
    w_out_b = w_out.astype(BF16)

    for l in range(depth):
        xf = _ffn(xf, ffn1_up, ffn1_down, ln_g3, ln_b3, l, 0, alpha)
        proj = _proj(xf, w_in_p, l, 1024, 1024, "in_proj")
        proj3 = proj.reshape(bsz, seq, PROJ_COLS)
        o_a = _dsa(proj3, bias_tab, k_sel).reshape(n, A_WIDTH)
        qt, k, vt = _mla_proj(proj, cc, ss, qn3, kvn3, w_uq_p, w_ukv_p, l)
        o_b = _mla_attn(qt, k.reshape(bsz, seq, -1), vt).reshape(n, B_WIDTH)
        mkv = _proj(memf, w_mem_kv, l, 512, 1024, "mem_proj")
        o_c = _mem_attn(proj3, mkv.reshape(bsz, mem.shape[1], 2 * C_WIDTH)).reshape(n, C_WIDTH)
        xf = _mix_out(o_a, o_b, o_c, proj, xf, w_branch_b, w_out_b, ln_g3, ln_b3, l, alpha)
        xf = _ffn(xf, ffn2_up, ffn2_down, ln_g3, ln_b3, l, 2, alpha)
    return xf.reshape(bsz, seq, d)
```

```python
import functools
import math

import jax
import jax.numpy as jnp
from jax import lax
from jax.experimental import pallas as pl
from jax.experimental.pallas import tpu as pltpu

F32 = jnp.float32
BF16 = jnp.bfloat16
I32 = jnp.int32

D_MODEL = 2048
D_FF = 5632
HEAD_DIM = 128
A_HEADS = 6
IDX_HEADS = 16
IDX_DIM = 64
TOPK_MAX = 256
B_HEADS = 6
Q_LORA = 512
KV_LORA = 512
NOPE_DIM = 128
ROPE_DIM = 64
V_DIM = 128
ROPE_THETA = 10000.0
C_HEADS = 4
REL_BUCKETS = 32
REL_MAX_DIST = 128
LN_EPS = 1e-5
RMS_EPS = 1e-6
A_WIDTH = A_HEADS * HEAD_DIM
B_WIDTH = B_HEADS * V_DIM
C_WIDTH = C_HEADS * HEAD_DIM
IN_SIZES = (A_WIDTH, HEAD_DIM, HEAD_DIM, IDX_HEADS * IDX_DIM, IDX_DIM, IDX_HEADS,
            Q_LORA, KV_LORA, ROPE_DIM, C_WIDTH, 3 * D_MODEL)

LANES = 128
SUBLANES = 8
BLOCK = 256
MLA_QK = 2 * LANES
INT_MIN = -2 ** 31
LOG2E = 1.4426950408889634

PROJ_COLS = 10240
BLK_AK, BLK_AV = 6, 7
BLK_IK, BLK_IW, BLK_KRA, BLK_KRB = 16, 17, 18, 19
BLK512_CQ, BLK512_CKV, BLK512_C = 5, 6, 7
BLK2048_GATE0 = 2

VMEM_LIMIT = 56 * 1024 * 1024
FFN_VMEM_LIMIT = 60 * 1024 * 1024


def _params(sem, vmem=VMEM_LIMIT):
    return pltpu.CompilerParams(dimension_semantics=sem, vmem_limit_bytes=vmem)


def _layer_norm_rows(y, g, b):
    mu = jnp.mean(y, axis=-1, keepdims=True)
    d = y - mu
    var = jnp.mean(d * d, axis=-1, keepdims=True)
    return d * lax.rsqrt(var + LN_EPS) * g + b


def _ffn_kernel(x_ref, wg_ref, wu_ref, wd_ref, g_ref, b_ref, o_ref, xb_ref, *, alpha):
    j = pl.program_id(1)

    @pl.when(j == 0)
    def _():
        xb_ref[...] = x_ref[...].astype(BF16)
        o_ref[...] = jnp.zeros_like(o_ref)

    xb = xb_ref[...]
    gate = jnp.dot(xb, wg_ref[0].astype(BF16), preferred_element_type=F32)
    up = jnp.dot(xb, wu_ref[0].astype(BF16), preferred_element_type=F32)
    h = (gate * (1.0 / (1.0 + jnp.exp(-gate))) * up).astype(BF16)
    o_ref[...] += jnp.dot(h, wd_ref[0].astype(BF16), preferred_element_type=F32)

    @pl.when(j == pl.num_programs(1) - 1)
    def _():
        y = alpha * x_ref[...] + 0.5 * o_ref[...]
        o_ref[...] = _layer_norm_rows(y, g_ref[0], b_ref[0])


def _ffn(x, w_up, w_down, ln_g, ln_b, layer, which, alpha, tm=1024, tf=256):
    n = x.shape[0]
    tm = min(tm, n)
    nf = D_FF // tf
    return pl.pallas_call(
        functools.partial(_ffn_kernel, alpha=alpha),
        grid=(n // tm, nf),
        in_specs=[
            pl.BlockSpec((tm, D_MODEL), lambda i, j: (i, 0)),
            pl.BlockSpec((1, D_MODEL, tf), lambda i, j: (layer, 0, j)),
            pl.BlockSpec((1, D_MODEL, tf), lambda i, j: (layer, 0, j + nf)),
            pl.BlockSpec((1, tf, D_MODEL), lambda i, j: (layer, j, 0)),
            pl.BlockSpec((1, 1, D_MODEL), lambda i, j: (3 * layer + which, 0, 0)),
            pl.BlockSpec((1, 1, D_MODEL), lambda i, j: (3 * layer + which, 0, 0)),
        ],
        out_specs=pl.BlockSpec((tm, D_MODEL), lambda i, j: (i, 0)),
        out_shape=jax.ShapeDtypeStruct((n, D_MODEL), F32),
        scratch_shapes=[pltpu.VMEM((tm, D_MODEL), BF16)],
        compiler_params=_params(("parallel", "arbitrary"), FFN_VMEM_LIMIT),
        name="ffn",
    )(x, w_up, w_up, w_down, ln_g, ln_b)


def _proj_kernel(x_ref, w_ref, o_ref, xb_ref):
    @pl.when(pl.program_id(1) == 0)
    def _():
        xb_ref[...] = x_ref[...].astype(BF16)

    o_ref[...] = jnp.dot(xb_ref[...], w_ref[0].astype(BF16), preferred_element_type=F32).astype(o_ref.dtype)


def _proj(x, w, layer, tm, tn, name):
    n, k = x.shape
    tm = min(tm, n)
    cols = w.shape[2]
    return pl.pallas_call(
        _proj_kernel,
        grid=(n // tm, cols // tn),
        in_specs=[pl.BlockSpec((tm, k), lambda i, j: (i, 0)),
                  pl.BlockSpec((1, k, tn), lambda i, j: (layer, 0, j))],
        out_specs=pl.BlockSpec((tm, tn), lambda i, j: (i, j)),
        out_shape=jax.ShapeDtypeStruct((n, cols), BF16),
        scratch_shapes=[pltpu.VMEM((tm, k), BF16)],
        compiler_params=_params(("parallel", "arbitrary")),
        name=name,
    )(x, w)


def _dsa_kernel(aq_ref, iq_ref, iw_ref, ik_ref, ak_ref, av_ref, bias_ref, o_ref,
                skey_ref, ika_ref, ikb_ref, vat_ref, m_ref, l_ref, acc_ref, lg_ref, *, k_sel, seq):
    i = pl.program_id(1)
    blk = BLOCK

    @pl.when(i == 0)
    def _():
        ik = ik_ref[0]
        lane = lax.broadcasted_iota(I32, ik.shape, 1)
        zero = jnp.zeros_like(ik)
        ika_ref[...] = jnp.where(lane < IDX_DIM, ik, zero)
        ikb_ref[...] = jnp.where(lane >= IDX_DIM, ik, zero)
        for c in range(seq // blk):
            vat_ref[c] = av_ref[0, c * blk:(c + 1) * blk, :].astype(F32).T.astype(BF16)

    iqt = iq_ref[0].astype(F32).T.astype(BF16)
    wt = iw_ref[0].astype(F32).T * (IDX_HEADS * IDX_DIM) ** -0.5

    srow = lax.broadcasted_iota(I32, (blk, blk), 0)
    tcol = lax.broadcasted_iota(I32, (blk, blk), 1)

    def score_body(c, carry):
        s0 = pl.multiple_of(c * blk, blk)
        ka = ika_ref[pl.ds(s0, blk), :]
        kb = ikb_ref[pl.ds(s0, blk), :]
        sc = jnp.zeros((blk, blk), F32)
        for j in range(IDX_HEADS // 2):
            rhs = iqt[j * LANES:(j + 1) * LANES, :]
            sc = sc + wt[2 * j:2 * j + 1, :] * jnp.maximum(jnp.dot(ka, rhs, preferred_element_type=F32), 0.0)
            sc = sc + wt[2 * j + 1:2 * j + 2, :] * jnp.maximum(jnp.dot(kb, rhs, preferred_element_type=F32), 0.0)
        bits = pltpu.bitcast(sc, I32)
        key = bits ^ ((bits >> 31) & jnp.int32(0x7FFFFFFF))
        causal = srow <= tcol + jnp.where(c < i, jnp.int32(blk), jnp.int32(0))
        skey_ref[pl.ds(s0, blk), :] = jnp.where(causal, key, jnp.int32(INT_MIN))
        return carry

    lax.fori_loop(0, i + 1, score_body, 0)

    n_acc = 4

    def bit_body(it, u):
        bit = jnp.left_shift(jnp.int32(1), jnp.int32(31) - it)
        trial_u = u | bit
        trial = jnp.broadcast_to(trial_u ^ jnp.int32(INT_MIN), (SUBLANES, blk))

        def cnt_body(c, accs):
            keys = skey_ref[pl.ds(pl.multiple_of(c * blk, blk), blk), :]
            accs = list(accs)
            for r in range(blk // SUBLANES):
                rows = keys[r * SUBLANES:(r + 1) * SUBLANES, :]
                accs[r % n_acc] = jnp.where(rows >= trial, accs[r % n_acc] + 1.0, accs[r % n_acc])
            return tuple(accs)

        zero = jnp.zeros((SUBLANES, blk), F32)
        accs = lax.fori_loop(0, i + 1, cnt_body, (zero,) * n_acc)
        cnt = jnp.sum((accs[0] + accs[1]) + (accs[2] + accs[3]), axis=0, keepdims=True)
        return jnp.where(cnt >= float(k_sel), trial_u, u)

    thr = lax.fori_loop(0, 32, bit_body, jnp.zeros((1, blk), I32)) ^ jnp.int32(INT_MIN)

    qt = (aq_ref[0].astype(F32) * (HEAD_DIM ** -0.5 * LOG2E)).T.astype(BF16)

    m_ref[...] = jnp.full_like(m_ref, -jnp.inf)
    l_ref[...] = jnp.zeros_like(l_ref)
    acc_ref[...] = jnp.zeros_like(acc_ref)

    def logits(c):
        ka = ak_ref[0, pl.ds(pl.multiple_of(c * blk, blk), blk), :]
        return [jnp.dot(ka, qt[h * LANES:(h + 1) * LANES, :], preferred_element_type=F32) for h in range(A_HEADS)]

    def att_step(c, bias_idx, has_next):
        slot = c % 2
        nxt = logits(c + 1) if has_next else None
        sel = skey_ref[pl.ds(pl.multiple_of(c * blk, blk), blk), :] >= thr
        vt = vat_ref[c]
        for h in range(A_HEADS):
            lg = lg_ref[slot, h]
            if bias_idx is not None:
                lg = lg + bias_ref[bias_idx, h]
            lg = jnp.where(sel, lg, -jnp.inf)
            m_old = m_ref[h]
            m_new = jnp.maximum(m_old, jnp.max(lg, axis=0, keepdims=True))
            m_use = jnp.where(m_new == -jnp.inf, 0.0, m_new)
            alpha = jnp.exp2(m_old - m_use)
            p = jnp.exp2(lg - m_use)
            l_ref[h] = l_ref[h] * alpha + jnp.sum(p, axis=0, keepdims=True)
            acc_ref[h] = acc_ref[h] * alpha + jnp.dot(vt, p.astype(BF16), preferred_element_type=F32)
            m_ref[h] = m_new
        if has_next:
            for h in range(A_HEADS):
                lg_ref[1 - slot, h] = nxt[h]

    first = logits(0)
    for h in range(A_HEADS):
        lg_ref[0, h] = first[h]

    def far_body(c, carry):
        att_step(c, None, True)
        return carry

    lax.fori_loop(0, jnp.maximum(i - 1, 0), far_body, 0)

    @pl.when(i >= 1)
    def _():
        att_step(i - 1, 0, True)

    att_step(i, 1, False)

    for h in range(A_HEADS):
        o_ref[0, :, h * LANES:(h + 1) * LANES] = (acc_ref[h] / l_ref[h]).T.astype(BF16)


def _dsa(proj3, bias_tab, k_sel):
    bsz, seq, _ = proj3.shape
    blk = BLOCK
    return pl.pallas_call(
        functools.partial(_dsa_kernel, k_sel=k_sel, seq=seq),
        grid=(bsz, seq // blk),
        in_specs=[
            pl.BlockSpec((1, blk, A_WIDTH), lambda b, i: (b, i, 0)),
            pl.BlockSpec((1, blk, IDX_HEADS * IDX_DIM), lambda b, i: (b, i, 1)),
            pl.BlockSpec((1, blk, LANES), lambda b, i: (b, i, BLK_IW)),
            pl.BlockSpec((1, seq, LANES), lambda b, i: (b, 0, BLK_IK)),
            pl.BlockSpec((1, seq, LANES), lambda b, i: (b, 0, BLK_AK)),
            pl.BlockSpec((1, seq, LANES), lambda b, i: (b, 0, BLK_AV)),
            pl.BlockSpec(bias_tab.shape, lambda b, i: (0, 0, 0, 0)),
        ],
        out_specs=pl.BlockSpec((1, blk, A_WIDTH), lambda b, i: (b, i, 0)),
        out_shape=jax.ShapeDtypeStruct((bsz, seq, A_WIDTH), BF16),
        scratch_shapes=[
            pltpu.VMEM((seq, blk), I32),
            pltpu.VMEM((seq, LANES), BF16),
            pltpu.VMEM((seq, LANES), BF16),
            pltpu.VMEM((seq // blk, HEAD_DIM, blk), BF16),
            pltpu.VMEM((A_HEADS, 1, blk), F32),
            pltpu.VMEM((A_HEADS, 1, blk), F32),
            pltpu.VMEM((A_HEADS, HEAD_DIM, blk), F32),
            pltpu.VMEM((2, A_HEADS, blk, blk), F32),
        ],
        compiler_params=_params(("parallel", "arbitrary")),
        name="dsa",
    )(proj3, proj3, proj3, proj3, proj3, proj3, bias_tab)


def _rms_rows(x, g):
    return x * lax.rsqrt(jnp.mean(x * x, axis=-1, keepdims=True) + RMS_EPS) * g


def _mla_proj_kernel(cq_ref, ckv_ref, kra_ref, krb_ref, cc_ref, ss_ref, qn_ref, kvn_ref, wq_ref, wkv_ref,
                     qt_ref, k_ref, vt_ref):
    tm = cq_ref.shape[0]
    cc = cc_ref[...]
    ss = ss_ref[...]
    qscale = (NOPE_DIM + ROPE_DIM) ** -0.5 * LOG2E
    cq = _rms_rows(cq_ref[...].astype(F32), qn_ref[0]).astype(BF16)
    q3 = jnp.dot(cq, wq_ref[0], preferred_element_type=F32)
    for h in range(B_HEADS):
        base = 3 * LANES * h
        rot = q3[:, base + LANES:base + 2 * LANES] * cc + q3[:, base + 2 * LANES:base + 3 * LANES] * ss
        qt_ref[h, 0:LANES, :] = (q3[:, base:base + LANES] * qscale).T.astype(BF16)
        qt_ref[h, LANES:MLA_QK, :] = (rot * qscale).T.astype(BF16)
    ckv = _rms_rows(ckv_ref[...].astype(F32), kvn_ref[0]).astype(BF16)
    kv = jnp.dot(ckv, wkv_ref[0], preferred_element_type=F32)
    krot = (kra_ref[...].astype(F32) * cc + krb_ref[...].astype(F32) * ss).astype(BF16)
    for h in range(B_HEADS):
        k_ref[:, MLA_QK * h:MLA_QK * h + LANES] = kv[:, h * LANES:(h + 1) * LANES].astype(BF16)
        k_ref[:, MLA_QK * h + LANES:MLA_QK * (h + 1)] = krot
        v_h = kv[:, B_HEADS * NOPE_DIM + h * V_DIM:B_HEADS * NOPE_DIM + (h + 1) * V_DIM]
        for c in range(tm // BLOCK):
            vt_ref[h, c] = v_h[c * BLOCK:(c + 1) * BLOCK, :].T.astype(BF16)


def _mla_proj(proj, cc, ss, qn, kvn, wq, wkv, layer, tm=512):
    n = proj.shape[0]
    per_layer = lambda i: (layer, 0, 0)
    return pl.pallas_call(
        _mla_proj_kernel,
        grid=(n // tm,),
        in_specs=[
            pl.BlockSpec((tm, Q_LORA), lambda i: (i, BLK512_CQ)),
            pl.BlockSpec((tm, KV_LORA), lambda i: (i, BLK512_CKV)),
            pl.BlockSpec((tm, LANES), lambda i: (i, BLK_KRA)),
            pl.BlockSpec((tm, LANES), lambda i: (i, BLK_KRB)),
            pl.BlockSpec((tm, LANES), lambda i: (i, 0)),
            pl.BlockSpec((tm, LANES), lambda i: (i, 0)),
            pl.BlockSpec((1, 1, Q_LORA), per_layer),
            pl.BlockSpec((1, 1, KV_LORA), per_layer),
            pl.BlockSpec((1,) + wq.shape[1:], per_layer),
            pl.BlockSpec((1,) + wkv.shape[1:], per_layer),
        ],
        out_specs=[
            pl.BlockSpec((B_HEADS, MLA_QK, tm), lambda i: (0, 0, i)),
            pl.BlockSpec((tm, B_HEADS * MLA_QK), lambda i: (i, 0)),
            pl.BlockSpec((B_HEADS, tm // BLOCK, V_DIM, BLOCK), lambda i: (0, i, 0, 0)),
        ],
        out_shape=[jax.ShapeDtypeStruct((B_HEADS, MLA_QK, n), BF16),
                   jax.ShapeDtypeStruct((n, B_HEADS * MLA_QK), BF16),
                   jax.ShapeDtypeStruct((B_HEADS, n // BLOCK, V_DIM, BLOCK), BF16)],
        compiler_params=_params(("parallel",)),
        name="mla_proj",
    )(proj, proj, proj, proj, cc, ss, qn, kvn, wq, wkv)


def _mla_attn_kernel(qt_ref, k_ref, vt_ref, o_ref, lg_ref, m_ref, l_ref, acc_ref):
    qi = pl.program_id(1)
    blk = BLOCK
    srow = lax.broadcasted_iota(I32, (blk, blk), 0)
    tcol = lax.broadcasted_iota(I32, (blk, blk), 1)
    causal_bias = jnp.where(srow <= tcol, 0.0, -jnp.inf).astype(F32)

    def logits(c):
        s0 = pl.multiple_of(c * blk, blk)
        return [jnp.dot(k_ref[0, pl.ds(s0, blk), h * MLA_QK:(h + 1) * MLA_QK], qt_ref[h],
                        preferred_element_type=F32) for h in range(B_HEADS)]

    m_ref[...] = jnp.full_like(m_ref, -jnp.inf)
    l_ref[...] = jnp.zeros_like(l_ref)
    acc_ref[...] = jnp.zeros_like(acc_ref)
    first = logits(0)
    for h in range(B_HEADS):
        lg_ref[0, h] = first[h]

    def body(c, carry):
        slot = c % 2
        nxt = logits(jnp.minimum(c + 1, qi))
        mask = jnp.where(c == qi, causal_bias, 0.0)
        for h in range(B_HEADS):
            lg = lg_ref[slot, h] + mask
            m_old = m_ref[h]
            m_new = jnp.maximum(m_old, jnp.max(lg, axis=0, keepdims=True))
            alpha = jnp.exp2(m_old - m_new)
            p = jnp.exp2(lg - m_new)
            l_ref[h] = l_ref[h] * alpha + jnp.sum(p, axis=0, keepdims=True)
            acc_ref[h] = acc_ref[h] * alpha + jnp.dot(vt_ref[h, c], p.astype(BF16), preferred_element_type=F32)
            m_ref[h] = m_new
        for h in range(B_HEADS):
            lg_ref[1 - slot, h] = nxt[h]
        return carry

    lax.fori_loop(0, qi + 1, body, 0)
    for h in range(B_HEADS):
        o_ref[0, :, h * V_DIM:(h + 1) * V_DIM] = (acc_ref[h] / l_ref[h]).T.astype(BF16)


def _mla_attn(qt, k3, vt):
    bsz, seq, _ = k3.shape
    blk = BLOCK
    nq = seq // blk
    return pl.pallas_call(
        _mla_attn_kernel,
        grid=(bsz, nq),
        in_specs=[
            pl.BlockSpec((B_HEADS, MLA_QK, blk), lambda b, i: (0, 0, b * nq + i)),
            pl.BlockSpec((1, seq, B_HEADS * MLA_QK), lambda b, i: (b, 0, 0)),
            pl.BlockSpec((B_HEADS, nq, V_DIM, blk), lambda b, i: (0, b, 0, 0)),
        ],
        out_specs=pl.BlockSpec((1, blk, B_WIDTH), lambda b, i: (b, i, 0)),
        out_shape=jax.ShapeDtypeStruct((bsz, seq, B_WIDTH), BF16),
        scratch_shapes=[
            pltpu.VMEM((2, B_HEADS, blk, blk), F32),
            pltpu.VMEM((B_HEADS, 1, blk), F32),
            pltpu.VMEM((B_HEADS, 1, blk), F32),
            pltpu.VMEM((B_HEADS, V_DIM, blk), F32),
        ],
        compiler_params=_params(("parallel", "arbitrary")),
        name="mla_attn",
    )(qt, k3, vt)


def _mem_attn_kernel(q_ref, kv_ref, o_ref):
    scale = HEAD_DIM ** -0.5
    for h in range(C_HEADS):
        q = q_ref[0, :, h * LANES:(h + 1) * LANES]
        k = kv_ref[0, :, h * LANES:(h + 1) * LANES]
        v = kv_ref[0, :, C_WIDTH + h * LANES:C_WIDTH + (h + 1) * LANES]
        s = lax.dot_general(q, k, (((1,), (1,)), ((), ())), preferred_element_type=F32) * scale
        p = jnp.exp(s - jnp.max(s, axis=-1, keepdims=True))
        l = jnp.sum(p, axis=-1, keepdims=True)
        o = jnp.dot(p.astype(BF16), v, preferred_element_type=F32) / l
        o_ref[0, :, h * LANES:(h + 1) * LANES] = o.astype(BF16)


def _mem_attn(proj3, mkv3, tq=512):
    bsz, seq, _ = proj3.shape
    mlen = mkv3.shape[1]
    return pl.pallas_call(
        _mem_attn_kernel,
        grid=(bsz, seq // tq),
        in_specs=[pl.BlockSpec((1, tq, C_WIDTH), lambda b, i: (b, i, BLK512_C)),
                  pl.BlockSpec((1, mlen, 2 * C_WIDTH), lambda b, i: (b, 0, 0))],
        out_specs=pl.BlockSpec((1, tq, C_WIDTH), lambda b, i: (b, i, 0)),
        out_shape=jax.ShapeDtypeStruct((bsz, seq, C_WIDTH), BF16),
        compiler_params=_params(("parallel", "arbitrary")),
        name="mem_attn",
    )(proj3, mkv3)


def _mix_out_kernel(oa_ref, ob_ref, oc_ref, ga_ref, gb_ref, gc_ref, x_ref, wb_ref, wo_ref, g_ref, b_ref,
                    o_ref, *, alpha):
    def gated(o_blk, gate_ref, r0, r1):
        y = jnp.dot(o_blk, wb_ref[0, r0:r1, :], preferred_element_type=F32)
        return y * (1.0 / (1.0 + jnp.exp(-gate_ref[...].astype(F32))))

    y = gated(oa_ref[...], ga_ref, 0, A_WIDTH)
    y = y + gated(ob_ref[...], gb_ref, A_WIDTH, A_WIDTH + B_WIDTH)
    y = y + gated(oc_ref[...], gc_ref, A_WIDTH + B_WIDTH, A_WIDTH + B_WIDTH + C_WIDTH)
    mix = jnp.dot(y.astype(BF16), wo_ref[0], preferred_element_type=F32)
    o_ref[...] = _layer_norm_rows(alpha * x_ref[...] + mix, g_ref[0], b_ref[0])


def _mix_out(oa, ob, oc, proj, x, wb, wo, ln_g, ln_b, layer, alpha, tm=256):
    n = x.shape[0]
    per_layer = lambda i: (layer, 0, 0)
    ln_row = lambda i: (3 * layer + 1, 0, 0)
    return pl.pallas_call(
        functools.partial(_mix_out_kernel, alpha=alpha),
        grid=(n // tm,),
        in_specs=[
            pl.BlockSpec((tm, A_WIDTH), lambda i: (i, 0)),
            pl.BlockSpec((tm, B_WIDTH), lambda i: (i, 0)),
            pl.BlockSpec((tm, C_WIDTH), lambda i: (i, 0)),
            pl.BlockSpec((tm, D_MODEL), lambda i: (i, BLK2048_GATE0)),
            pl.BlockSpec((tm, D_MODEL), lambda i: (i, BLK2048_GATE0 + 1)),
            pl.BlockSpec((tm, D_MODEL), lambda i: (i, BLK2048_GATE0 + 2)),
            pl.BlockSpec((tm, D_MODEL), lambda i: (i, 0)),
            pl.BlockSpec((1,) + wb.shape[1:], per_layer),
            pl.BlockSpec((1,) + wo.shape[1:], per_layer),
            pl.BlockSpec((1, 1, D_MODEL), ln_row),
            pl.BlockSpec((1, 1, D_MODEL), ln_row),
        ],
        out_specs=pl.BlockSpec((tm, D_MODEL), lambda i: (i, 0)),
        out_shape=jax.ShapeDtypeStruct((n, D_MODEL), F32),
        compiler_params=_params(("parallel",)),
        name="mix_out",
    )(oa, ob, oc, proj, proj, proj, x, wb, wo, ln_g, ln_b)


def _pack_w_in(w):
    offs = [0]
    for s in IN_SIZES:
        offs.append(offs[-1] + s)
    seg = lambda k: w[..., offs[k]:offs[k + 1]]
    zeros = lambda c: jnp.zeros(w.shape[:-1] + (c,), w.dtype)
    ik, iw, kr = seg(4), seg(5), seg(8)
    half = ROPE_DIM // 2
    packed = jnp.concatenate([
        w[..., :offs[4]],
        ik, ik,
        iw, zeros(LANES - IDX_HEADS),
        kr, zeros(LANES - ROPE_DIM),
        kr[..., half:], kr[..., :half], zeros(LANES - ROPE_DIM),
        seg(6), seg(7), seg(9), seg(10),
    ], axis=-1)
    assert packed.shape[-1] == PROJ_COLS
    return packed.astype(BF16)


def _pack_w_uq(w):
    half = ROPE_DIM // 2
    zeros = jnp.zeros(w.shape[:-1] + (LANES - ROPE_DIM,), w.dtype)
    cols = []
    for h in range(B_HEADS):
        base = h * (NOPE_DIM + ROPE_DIM)
        r = w[..., base + NOPE_DIM:base + NOPE_DIM + ROPE_DIM]
        cols += [w[..., base:base + NOPE_DIM], r, zeros, r[..., half:], r[..., :half], zeros]
    return jnp.concatenate(cols, axis=-1).astype(BF16)


def _pack_w_ukv(w):
    step = NOPE_DIM + V_DIM
    ks = [w[..., h * step:h * step + NOPE_DIM] for h in range(B_HEADS)]
    vs = [w[..., h * step + NOPE_DIM:(h + 1) * step] for h in range(B_HEADS)]
    return jnp.concatenate(ks + vs, axis=-1).astype(BF16)


def _rel_bucket(dist):
    n = jnp.maximum(dist, 0)
    max_exact = REL_BUCKETS // 2
    nf = jnp.maximum(n, 1).astype(F32)
    large = max_exact + (jnp.log(nf / max_exact) / math.log(REL_MAX_DIST / max_exact)
                         * (REL_BUCKETS - max_exact)).astype(I32)
    large = jnp.minimum(large, REL_BUCKETS - 1)
    return jnp.where(n < max_exact, n, large)


def _bias_tables(rel_bias):
    assert BLOCK + 1 >= REL_MAX_DIST
    s = jnp.arange(BLOCK)[:, None]
    t = jnp.arange(BLOCK)[None, :]
    tiles = []
    for block_gap in (1, 0):
        dist = t - s + block_gap * BLOCK
        onehot = (_rel_bucket(dist)[:, :, None] == jnp.arange(REL_BUCKETS)).astype(F32)
        looked_up = jnp.einsum("stb,bh->sth", onehot, rel_bias, precision=lax.Precision.HIGHEST)
        tile = (looked_up - rel_bias[REL_BUCKETS - 1]) * LOG2E
        tile = jnp.where((dist >= 0)[:, :, None], tile, -jnp.inf)
        tiles.append(jnp.transpose(tile, (2, 0, 1)))
    return jnp.stack(tiles).astype(F32)


def kernel(x, mem, positions, rel_bias, ln_g, ln_b, ffn1_up, ffn1_down, w_in, q_norm, kv_norm, w_uq, w_ukv,
           w_mem_kv, w_branch, w_out, ffn2_up, ffn2_down):
    bsz, seq, d = x.shape
    depth = ffn1_up.shape[0]
    n = bsz * seq
    alpha = (2 * depth) ** 0.25
    k_sel = min(TOPK_MAX, seq // 4)
    assert d == D_MODEL and seq % (2 * BLOCK) == 0

    inv_freq = ROPE_THETA ** (-jnp.arange(0, ROPE_DIM, 2, dtype=F32) / ROPE_DIM)
    ang = positions.astype(F32)[..., None] * inv_freq
    cos, sin = jnp.cos(ang).reshape(n, -1), jnp.sin(ang).reshape(n, -1)
    pad = jnp.zeros((n, LANES - ROPE_DIM), F32)
    cc = jnp.concatenate([cos, cos, pad], axis=1)
    ss = jnp.concatenate([-sin, sin, pad], axis=1)
    bias_tab = _bias_tables(rel_bias)

    xf = x.reshape(n, d)
    memf = mem.reshape(bsz * mem.shape[1], d)
    ln_g3 = ln_g.reshape(depth * 3, 1, d)
    ln_b3 = ln_b.reshape(depth * 3, 1, d)
    qn3 = q_norm.reshape(depth, 1, Q_LORA)
    kvn3 = kv_norm.reshape(depth, 1, KV_LORA)
    w_in_p = _pack_w_in(w_in.astype(BF16))
    w_uq_p = _pack_w_uq(w_uq.astype(BF16))
    w_ukv_p = _pack_w_ukv(w_ukv.astype(BF16))
    w_branch_b = w_branch.astype(BF16)
    w_out_b = w_out.astype(BF16)

    for l in range(depth):
        xf = _ffn(xf, ffn1_up, ffn1_down, ln_g3, ln_b3, l, 0, alpha)
        proj = _proj(xf, w_in_p, l, 1024, 1024, "in_proj")
        proj3 = proj.reshape(bsz, seq, PROJ_COLS)
        o_a = _dsa(proj3, bias_tab, k_sel).reshape(n, A_WIDTH)
        qt, k, vt = _mla_proj(proj, cc, ss, qn3, kvn3, w_uq_p, w_ukv_p, l)
        o_b = _mla_attn(qt, k.reshape(bsz, seq, -1), vt).reshape(n, B_WIDTH)
        mkv = _proj(memf, w_mem_kv, l, 512, 1024, "mem_proj")
        o_c = _mem_attn(proj3, mkv.reshape(bsz, mem.shape[1], 2 * C_WIDTH)).reshape(n, C_WIDTH)
        xf = _mix_out(o_a, o_b, o_c, proj, xf, w_branch_b, w_out_b, ln_g3, ln_b3, l, alpha)
        xf = _ffn(xf, ffn2_up, ffn2_down, ln_g3, ln_b3, l, 2, alpha)
    return xf.reshape(bsz, seq, d)
```

```python
import functools
import math

import jax
import jax.numpy as jnp
from jax import lax
from jax.experimental import pallas as pl
from jax.experimental.pallas import tpu as pltpu

F32 = jnp.float32
BF16 = jnp.bfloat16
I32 = jnp.int32

D_MODEL = 2048
D_FF = 5632
HEAD_DIM = 128
A_HEADS = 6
IDX_HEADS = 16
IDX_DIM = 64
TOPK_MAX = 256
B_HEADS = 6
Q_LORA = 512
KV_LORA = 512
NOPE_DIM = 128
ROPE_DIM = 64
V_DIM = 128
ROPE_THETA = 10000.0
C_HEADS = 4
REL_BUCKETS = 32
REL_MAX_DIST = 128
LN_EPS = 1e-5
RMS_EPS = 1e-6
A_WIDTH = A_HEADS * HEAD_DIM
B_WIDTH = B_HEADS * V_DIM
C_WIDTH = C_HEADS * HEAD_DIM
IN_SIZES = (A_WIDTH, HEAD_DIM, HEAD_DIM, IDX_HEADS * IDX_DIM, IDX_DIM, IDX_HEADS,
            Q_LORA, KV_LORA, ROPE_DIM, C_WIDTH, 3 * D_MODEL)

LANES = 128
SUBLANES = 8
BLOCK = 256
MLA_QK = 2 * LANES
INT_MIN = -2 ** 31
LOG2E = 1.4426950408889634

PROJ_COLS = 10240
BLK_AK, BLK_AV = 6, 7
BLK_IK, BLK_IW, BLK_KRA, BLK_KRB = 16, 17, 18, 19
BLK512_CQ, BLK512_CKV, BLK512_C = 5, 6, 7
BLK2048_GATE0 = 2

VMEM_LIMIT = 56 * 1024 * 1024
FFN_VMEM_LIMIT = 60 * 1024 * 1024


def _params(sem, vmem=VMEM_LIMIT):
    return pltpu.CompilerParams(dimension_semantics=sem, vmem_limit_bytes=vmem)


def _layer_norm_rows(y, g, b):
    mu = jnp.mean(y, axis=-1, keepdims=True)
    d = y - mu
    var = jnp.mean(d * d, axis=-1, keepdims=True)
    return d * lax.rsqrt(var + LN_EPS) * g + b


def _ffn_kernel(x_ref, wg_ref, wu_ref, wd_ref, g_ref, b_ref, o_ref, xb_ref, *, alpha):
    j = pl.program_id(1)

    @pl.when(j == 0)
    def _():
        xb_ref[...] = x_ref[...].astype(BF16)
        o_ref[...] = jnp.zeros_like(o_ref)

    xb = xb_ref[...]
    gate = jnp.dot(xb, wg_ref[0].astype(BF16), preferred_element_type=F32)
    up = jnp.dot(xb, wu_ref[0].astype(BF16), preferred_element_type=F32)
    h = (gate * (1.0 / (1.0 + jnp.exp(-gate))) * up).astype(BF16)
    o_ref[...] += jnp.dot(h, wd_ref[0].astype(BF16), preferred_element_type=F32)

    @pl.when(j == pl.num_programs(1) - 1)
    def _():
        y = alpha * x_ref[...] + 0.5 * o_ref[...]
        o_ref[...] = _layer_norm_rows(y, g_ref[0], b_ref[0])


def _ffn(x, w_up, w_down, ln_g, ln_b, layer, which, alpha, tm=1024, tf=256):
    n = x.shape[0]
    tm = min(tm, n)
    nf = D_FF // tf
    return pl.pallas_call(
        functools.partial(_ffn_kernel, alpha=alpha),
        grid=(n // tm, nf),
        in_specs=[
            pl.BlockSpec((tm, D_MODEL), lambda i, j: (i, 0)),
            pl.BlockSpec((1, D_MODEL, tf), lambda i, j: (layer, 0, j)),
            pl.BlockSpec((1, D_MODEL, tf), lambda i, j: (layer, 0, j + nf)),
            pl.BlockSpec((1, tf, D_MODEL), lambda i, j: (layer, j, 0)),
            pl.BlockSpec((1, 1, D_MODEL), lambda i, j: (3 * layer + which, 0, 0)),
            pl.BlockSpec((1, 1, D_MODEL), lambda i, j: (3 * layer + which, 0, 0)),
        ],
        out_specs=pl.BlockSpec((tm, D_MODEL), lambda i, j: (i, 0)),
        out_shape=jax.ShapeDtypeStruct((n, D_MODEL), F32),
        scratch_shapes=[pltpu.VMEM((tm, D_MODEL), BF16)],
        compiler_params=_params(("parallel", "arbitrary"), FFN_VMEM_LIMIT),
        name="ffn",
    )(x, w_up, w_up, w_down, ln_g, ln_b)


def _proj_kernel(x_ref, w_ref, o_ref, xb_ref):
    @pl.when(pl.program_id(1) == 0)
    def _():
        xb_ref[...] = x_ref[...].astype(BF16)

    o_ref[...] = jnp.dot(xb_ref[...], w_ref[0].astype(BF16), preferred_element_type=F32).astype(o_ref.dtype)


def _proj(x, w, layer, tm, tn, name):
    n, k = x.shape
    tm = min(tm, n)
    cols = w.shape[2]
    return pl.pallas_call(
        _proj_kernel,
        grid=(n // tm, cols // tn),
        in_specs=[pl.BlockSpec((tm, k), lambda i, j: (i, 0)),
                  pl.BlockSpec((1, k, tn), lambda i, j: (layer, 0, j))],
        out_specs=pl.BlockSpec((tm, tn), lambda i, j: (i, j)),
        out_shape=jax.ShapeDtypeStruct((n, cols), BF16),
        scratch_shapes=[pltpu.VMEM((tm, k), BF16)],
        compiler_params=_params(("parallel", "arbitrary")),
        name=name,
    )(x, w)


def _dsa_kernel(aq_ref, iq_ref, iw_ref, ik_ref, ak_ref, av_ref, bias_ref, o_ref,
                skey_ref, ika_ref, ikb_ref, vat_ref, m_ref, l_ref, acc_ref, lg_ref, *, k_sel, seq):
    i = pl.program_id(1)
    blk = BLOCK

    @pl.when(i == 0)
    def _():
        ik = ik_ref[0]
        lane = lax.broadcasted_iota(I32, ik.shape, 1)
        zero = jnp.zeros_like(ik)
        ika_ref[...] = jnp.where(lane < IDX_DIM, ik, zero)
        ikb_ref[...] = jnp.where(lane >= IDX_DIM, ik, zero)
        for c in range(seq // blk):
            vat_ref[c] = av_ref[0, c * blk:(c + 1) * blk, :].astype(F32).T.astype(BF16)

    iqt = iq_ref[0].astype(F32).T.astype(BF16)
    wt = iw_ref[0].astype(F32).T * (IDX_HEADS * IDX_DIM) ** -0.5

    srow = lax.broadcasted_iota(I32, (blk, blk), 0)
    tcol = lax.broadcasted_iota(I32, (blk, blk), 1)

    def score_body(c, carry):
        s0 = pl.multiple_of(c * blk, blk)
        ka = ika_ref[pl.ds(s0, blk), :]
        kb = ikb_ref[pl.ds(s0, blk), :]
        sc = jnp.zeros((blk, blk), F32)
        for j in range(IDX_HEADS // 2):
            rhs = iqt[j * LANES:(j + 1) * LANES, :]
            sc = sc + wt[2 * j:2 * j + 1, :] * jnp.maximum(jnp.dot(ka, rhs, preferred_element_type=F32), 0.0)
            sc = sc + wt[2 * j + 1:2 * j + 2, :] * jnp.maximum(jnp.dot(kb, rhs, preferred_element_type=F32), 0.0)
        bits = pltpu.bitcast(sc, I32)
        key = bits ^ ((bits >> 31) & jnp.int32(0x7FFFFFFF))
        causal = srow <= tcol + jnp.where(c < i, jnp.int32(blk), jnp.int32(0))
        skey_ref[pl.ds(s0, blk), :] = jnp.where(causal, key, jnp.int32(INT_MIN))
        return carry

    lax.fori_loop(0, i + 1, score_body, 0)

    n_acc = 4

    def bit_body(it, u):
        bit = jnp.left_shift(jnp.int32(1), jnp.int32(31) - it)
        trial_u = u | bit
        trial = jnp.broadcast_to(trial_u ^ jnp.int32(INT_MIN), (SUBLANES, blk))

        def cnt_body(c, accs):
            keys = skey_ref[pl.ds(pl.multiple_of(c * blk, blk), blk), :]
            accs = list(accs)
            for r in range(blk // SUBLANES):
                rows = keys[r * SUBLANES:(r + 1) * SUBLANES, :]
                accs[r % n_acc] = jnp.where(rows >= trial, accs[r % n_acc] + 1.0, accs[r % n_acc])
            return tuple(accs)

        zero = jnp.zeros((SUBLANES, blk), F32)
        accs = lax.fori_loop(0, i + 1, cnt_body, (zero,) * n_acc)
        cnt = jnp.sum((accs[0] + accs[1]) + (accs[2] + accs[3]), axis=0, keepdims=True)
        return jnp.where(cnt >= float(k_sel), trial_u, u)

    thr = lax.fori_loop(0, 32, bit_body, jnp.zeros((1, blk), I32)) ^ jnp.int32(INT_MIN)

    qt = (aq_ref[0].astype(F32) * (HEAD_DIM ** -0.5 * LOG2E)).T.astype(BF16)

    m_ref[...] = jnp.full_like(m_ref, -jnp.inf)
    l_ref[...] = jnp.zeros_like(l_ref)
    acc_ref[...] = jnp.zeros_like(acc_ref)

    def logits(c):
        ka = ak_ref[0, pl.ds(pl.multiple_of(c * blk, blk), blk), :]
        return [jnp.dot(ka, qt[h * LANES:(h + 1) * LANES, :], preferred_element_type=F32) for h in range(A_HEADS)]

    def att_step(c, bias_idx, has_next):
        slot = c % 2
        nxt = logits(c + 1) if has_next else None
        sel = skey_ref[pl.ds(pl.multiple_of(c * blk, blk), blk), :] >= thr
        vt = vat_ref[c]
        for h in range(A_HEADS):
            lg = lg_ref[slot, h]
            if bias_idx is not None:
                lg = lg + bias_ref[bias_idx, h]
            lg = jnp.where(sel, lg, -jnp.inf)
            m_old = m_ref[h]
            m_new = jnp.maximum(m_old, jnp.max(lg, axis=0, keepdims=True))
            m_use = jnp.where(m_new == -jnp.inf, 0.0, m_new)
            alpha = jnp.exp2(m_old - m_use)
            p = jnp.exp2(lg - m_use)
            l_ref[h] = l_ref[h] * alpha + jnp.sum(p, axis=0, keepdims=True)
            acc_ref[h] = acc_ref[h] * alpha + jnp.dot(vt, p.astype(BF16), preferred_element_type=F32)
            m_ref[h] = m_new
        if has_next:
            for h in range(A_HEADS):
                lg_ref[1 - slot, h] = nxt[h]

    first = logits(0)
    for h in range(A_HEADS):
        lg_ref[0, h] = first[h]

    def far_body(c, carry):
        att_step(c, None, True)
        return carry

    lax.fori_loop(0, jnp.maximum(i - 1, 0), far_body, 0)

    @pl.when(i >= 1)
    def _():
        att_step(i - 1, 0, True)

    att_step(i, 1, False)

    for h in range(A_HEADS):
        o_ref[0, :, h * LANES:(h + 1) * LANES] = (acc_ref[h] / l_ref[h]).T.astype(BF16)


def _dsa(proj3, bias_tab, k_sel):
    bsz, seq, _ = proj3.shape
    blk = BLOCK
    return pl.pallas_call(
        functools.partial(_dsa_kernel, k_sel=k_sel, seq=seq),
        grid=(bsz, seq // blk),
        in_specs=[
            pl.BlockSpec((1, blk, A_WIDTH), lambda b, i: (b, i, 0)),
            pl.BlockSpec((1, blk, IDX_HEADS * IDX_DIM), lambda b, i: (b, i, 1)),
            pl.BlockSpec((1, blk, LANES), lambda b, i: (b, i, BLK_IW)),
            pl.BlockSpec((1, seq, LANES), lambda b, i: (b, 0, BLK_IK)),
            pl.BlockSpec((1, seq, LANES), lambda b, i: (b, 0, BLK_AK)),
            pl.BlockSpec((1, seq, LANES), lambda b, i: (b, 0, BLK_AV)),
            pl.BlockSpec(bias_tab.shape, lambda b, i: (0, 0, 0, 0)),
        ],
        out_specs=pl.BlockSpec((1, blk, A_WIDTH), lambda b, i: (b, i, 0)),
        out_shape=jax.ShapeDtypeStruct((bsz, seq, A_WIDTH), BF16),
        scratch_shapes=[
            pltpu.VMEM((seq, blk), I32),
            pltpu.VMEM((seq, LANES), BF16),
            pltpu.VMEM((seq, LANES), BF16),
            pltpu.VMEM((seq // blk, HEAD_DIM, blk), BF16),
            pltpu.VMEM((A_HEADS, 1, blk), F32),
            pltpu.VMEM((A_HEADS, 1, blk), F32),
            pltpu.VMEM((A_HEADS, HEAD_DIM, blk), F32),
            pltpu.VMEM((2, A_HEADS, blk, blk), F32),
        ],
        compiler_params=_params(("parallel", "arbitrary")),
        name="dsa",
    )(proj3, proj3, proj3, proj3, proj3, proj3, bias_tab)


def _rms_rows(x, g):
    return x * lax.rsqrt(jnp.mean(x * x, axis=-1, keepdims=True) + RMS_EPS) * g


def _mla_proj_kernel(cq_ref, ckv_ref, kra_ref, krb_ref, cc_ref, ss_ref, qn_ref, kvn_ref, wq_ref, wkv_ref,
                     qt_ref, k_ref, vt_ref):
    tm = cq_ref.shape[0]
    cc = cc_ref[...]
    ss = ss_ref[...]
    qscale = (NOPE_DIM + ROPE_DIM) ** -0.5 * LOG2E
    cq = _rms_rows(cq_ref[...].astype(F32), qn_ref[0]).astype(BF16)
    q3 = jnp.dot(cq, wq_ref[0], preferred_element_type=F32)
    for h in range(B_HEADS):
        base = 3 * LANES * h
        rot = q3[:, base + LANES:base + 2 * LANES] * cc + q3[:, base + 2 * LANES:base + 3 * LANES] * ss
        qt_ref[h, 0:LANES, :] = (q3[:, base:base + LANES] * qscale).T.astype(BF16)
        qt_ref[h, LANES:MLA_QK, :] = (rot * qscale).T.astype(BF16)
    ckv = _rms_rows(ckv_ref[...].astype(F32), kvn_ref[0]).astype(BF16)
    kv = jnp.dot(ckv, wkv_ref[0], preferred_element_type=F32)
    krot = (kra_ref[...].astype(F32) * cc + krb_ref[...].astype(F32) * ss).astype(BF16)
    for h in range(B_HEADS):
        k_ref[:, MLA_QK * h:MLA_QK * h + LANES] = kv[:, h * LANES:(h + 1) * LANES].astype(BF16)
        k_ref[:, MLA_QK * h + LANES:MLA_QK * (h + 1)] = krot
        v_h = kv[:, B_HEADS * NOPE_DIM + h * V_DIM:B_HEADS * NOPE_DIM + (h + 1) * V_DIM]
        for c in range(tm // BLOCK):
            vt_ref[h, c] = v_h[c * BLOCK:(c + 1) * BLOCK, :].T.astype(BF16)


def _mla_proj(proj, cc, ss, qn, kvn, wq, wkv, layer, tm=512):
    n = proj.shape[0]
    per_layer = lambda i: (layer, 0, 0)
    return pl.pallas_call(
        _mla_proj_kernel,
        grid=(n // tm,),
        in_specs=[
            pl.BlockSpec((tm, Q_LORA), lambda i: (i, BLK512_CQ)),
            pl.BlockSpec((tm, KV_LORA), lambda i: (i, BLK512_CKV)),
            pl.BlockSpec((tm, LANES), lambda i: (i, BLK_KRA)),
            pl.BlockSpec((tm, LANES), lambda i: (i, BLK_KRB)),
            pl.BlockSpec((tm, LANES), lambda i: (i, 0)),
            pl.BlockSpec((tm, LANES), lambda i: (i, 0)),
            pl.BlockSpec((1, 1, Q_LORA), per_layer),
            pl.BlockSpec((1, 1, KV_LORA), per_layer),
            pl.BlockSpec((1,) + wq.shape[1:], per_layer),
            pl.BlockSpec((1,) + wkv.shape[1:], per_layer),
        ],
        out_specs=[
            pl.BlockSpec((B_HEADS, MLA_QK, tm), lambda i: (0, 0, i)),
            pl.BlockSpec((tm, B_HEADS * MLA_QK), lambda i: (i, 0)),
            pl.BlockSpec((B_HEADS, tm // BLOCK, V_DIM, BLOCK), lambda i: (0, i, 0, 0)),
        ],
        out_shape=[jax.ShapeDtypeStruct((B_HEADS, MLA_QK, n), BF16),
                   jax.ShapeDtypeStruct((n, B_HEADS * MLA_QK), BF16),
                   jax.ShapeDtypeStruct((B_HEADS, n // BLOCK, V_DIM, BLOCK), BF16)],
        compiler_params=_params(("parallel",)),
        name="mla_proj",
    )(proj, proj, proj, proj, cc, ss, qn, kvn, wq, wkv)


def _mla_attn_kernel(qt_ref, k_ref, vt_ref, o_ref, lg0_ref, lg1_ref, p_ref, al_ref, m_ref, l_ref, acc_ref):
    lg_refs = (lg0_ref, lg1_ref)
    qi = pl.program_id(1)
    blk = BLOCK
    srow = lax.broadcasted_iota(I32, (blk, blk), 0)
    tcol = lax.broadcasted_iota(I32, (blk, blk), 1)
    causal_bias = jnp.where(srow <= tcol, 0.0, -jnp.inf).astype(F32)

    def logits(c):
        s0 = pl.multiple_of(c * blk, blk)
        return [jnp.dot(k_ref[0, pl.ds(s0, blk), h * MLA_QK:(h + 1) * MLA_QK], qt_ref[h],
                        preferred_element_type=F32) for h in range(B_HEADS)]

    def accumulate(c, slot):
        for h in range(B_HEADS):
            pv = jnp.dot(vt_ref[h, c], p_ref[slot, h], preferred_element_type=F32)
            acc_ref[h] = acc_ref[h] * al_ref[slot, h] + pv

    m_ref[...] = jnp.full_like(m_ref, -jnp.inf)
    l_ref[...] = jnp.zeros_like(l_ref)
    acc_ref[...] = jnp.zeros_like(acc_ref)
    p_ref[1] = jnp.zeros_like(p_ref[1])
    al_ref[1] = jnp.ones_like(al_ref[1])
    first = logits(0)
    for h in range(B_HEADS):
        lg0_ref[h] = first[h]
    def stage(c, slot, has_next=True):
        if has_next:
            nxt = logits(jnp.minimum(c + 1, qi))
            for h in range(B_HEADS):
                lg_refs[1 - slot][h] = nxt[h]
        accumulate(jnp.maximum(c - 1, 0), 1 - slot)
        mask = jnp.where(c == qi, causal_bias, 0.0)
        for h in range(B_HEADS):
            lg = lg_refs[slot][h] + mask
            m_old = m_ref[h]
            m_new = jnp.maximum(m_old, jnp.max(lg, axis=0, keepdims=True))
            alpha = jnp.exp2(m_old - m_new)
            p = jnp.exp2(lg - m_new)
            l_ref[h] = l_ref[h] * alpha + jnp.sum(p, axis=0, keepdims=True)
            m_ref[h] = m_new
            p_ref[slot, h] = p.astype(BF16)
            al_ref[slot, h] = alpha

    def pair_body(j, carry):
        stage(2 * j, 0)
        stage(2 * j + 1, 1)
        return carry

    n_chunks = qi + 1
    lax.fori_loop(0, n_chunks // 2, pair_body, 0)

    @pl.when(n_chunks % 2 == 1)
    def _():
        stage(qi, 0, has_next=False)

    accumulate(qi, qi % 2)
    for h in range(B_HEADS):
        o_ref[0, :, h * V_DIM:(h + 1) * V_DIM] = (acc_ref[h] / l_ref[h]).T.astype(BF16)


def _mla_attn(qt, k3, vt):
    bsz, seq, _ = k3.shape
    blk = BLOCK
    nq = seq // blk
    return pl.pallas_call(
        _mla_attn_kernel,
        grid=(bsz, nq),
        in_specs=[
            pl.BlockSpec((B_HEADS, MLA_QK, blk), lambda b, i: (0, 0, b * nq + i)),
            pl.BlockSpec((1, seq, B_HEADS * MLA_QK), lambda b, i: (b, 0, 0)),
            pl.BlockSpec((B_HEADS, nq, V_DIM, blk), lambda b, i: (0, b, 0, 0)),
        ],
        out_specs=pl.BlockSpec((1, blk, B_WIDTH), lambda b, i: (b, i, 0)),
        out_shape=jax.ShapeDtypeStruct((bsz, seq, B_WIDTH), BF16),
        scratch_shapes=[
            pltpu.VMEM((B_HEADS, blk, blk), F32),
            pltpu.VMEM((B_HEADS, blk, blk), F32),
            pltpu.VMEM((2, B_HEADS, blk, blk), BF16),
            pltpu.VMEM((2, B_HEADS, 1, blk), F32),
            pltpu.VMEM((B_HEADS, 1, blk), F32),
            pltpu.VMEM((B_HEADS, 1, blk), F32),
            pltpu.VMEM((B_HEADS, V_DIM, blk), F32),
        ],
        compiler_params=_params(("parallel", "arbitrary")),
        name="mla_attn",
    )(qt, k3, vt)


def _mem_attn_kernel(q_ref, kv_ref, o_ref):
    scale = HEAD_DIM ** -0.5
    for h in range(C_HEADS):
        q = q_ref[0, :, h * LANES:(h + 1) * LANES]
        k = kv_ref[0, :, h * LANES:(h + 1) * LANES]
        v = kv_ref[0, :, C_WIDTH + h * LANES:C_WIDTH + (h + 1) * LANES]
        s = lax.dot_general(q, k, (((1,), (1,)), ((), ())), preferred_element_type=F32) * scale
        p = jnp.exp(s - jnp.max(s, axis=-1, keepdims=True))
        l = jnp.sum(p, axis=-1, keepdims=True)
        o = jnp.dot(p.astype(BF16), v, preferred_element_type=F32) / l
        o_ref[0, :, h * LANES:(h + 1) * LANES] = o.astype(BF16)


def _mem_attn(proj3, mkv3, tq=512):
    bsz, seq, _ = proj3.shape
    mlen = mkv3.shape[1]
    return pl.pallas_call(
        _mem_attn_kernel,
        grid=(bsz, seq // tq),
        in_specs=[pl.BlockSpec((1, tq, C_WIDTH), lambda b, i: (b, i, BLK512_C)),
                  pl.BlockSpec((1, mlen, 2 * C_WIDTH), lambda b, i: (b, 0, 0))],
        out_specs=pl.BlockSpec((1, tq, C_WIDTH), lambda b, i: (b, i, 0)),
        out_shape=jax.ShapeDtypeStruct((bsz, seq, C_WIDTH), BF16),
        compiler_params=_params(("parallel", "arbitrary")),
        name="mem_attn",
    )(proj3, mkv3)


def _mix_out_kernel(oa_ref, ob_ref, oc_ref, ga_ref, gb_ref, gc_ref, x_ref, wb_ref, wo_ref, g_ref, b_ref,
                    o_ref, *, alpha):
    def gated(o_blk, gate_ref, r0, r1):
        y = jnp.dot(o_blk, wb_ref[0, r0:r1, :], preferred_element_type=F32)
        return y * (1.0 / (1.0 + jnp.exp(-gate_ref[...].astype(F32))))

    y = gated(oa_ref[...], ga_ref, 0, A_WIDTH)
    y = y + gated(ob_ref[...], gb_ref, A_WIDTH, A_WIDTH + B_WIDTH)
    y = y + gated(oc_ref[...], gc_ref, A_WIDTH + B_WIDTH, A_WIDTH + B_WIDTH + C_WIDTH)
    mix = jnp.dot(y.astype(BF16), wo_ref[0], preferred_element_type=F32)
    o_ref[...] = _layer_norm_rows(alpha * x_ref[...] + mix, g_ref[0], b_ref[0])


def _mix_out(oa, ob, oc, proj, x, wb, wo, ln_g, ln_b, layer, alpha, tm=256):
    n = x.shape[0]
    per_layer = lambda i: (layer, 0, 0)
    ln_row = lambda i: (3 * layer + 1, 0, 0)
    return pl.pallas_call(
        functools.partial(_mix_out_kernel, alpha=alpha),
        grid=(n // tm,),
        in_specs=[
            pl.BlockSpec((tm, A_WIDTH), lambda i: (i, 0)),
            pl.BlockSpec((tm, B_WIDTH), lambda i: (i, 0)),
            pl.BlockSpec((tm, C_WIDTH), lambda i: (i, 0)),
            pl.BlockSpec((tm, D_MODEL), lambda i: (i, BLK2048_GATE0)),
            pl.BlockSpec((tm, D_MODEL), lambda i: (i, BLK2048_GATE0 + 1)),
            pl.BlockSpec((tm, D_MODEL), lambda i: (i, BLK2048_GATE0 + 2)),
            pl.BlockSpec((tm, D_MODEL), lambda i: (i, 0)),
            pl.BlockSpec((1,) + wb.shape[1:], per_layer),
            pl.BlockSpec((1,) + wo.shape[1:], per_layer),
            pl.BlockSpec((1, 1, D_MODEL), ln_row),
            pl.BlockSpec((1, 1, D_MODEL), ln_row),
        ],
        out_specs=pl.BlockSpec((tm, D_MODEL), lambda i: (i, 0)),
        out_shape=jax.ShapeDtypeStruct((n, D_MODEL), F32),
        compiler_params=_params(("parallel",)),
        name="mix_out",
    )(oa, ob, oc, proj, proj, proj, x, wb, wo, ln_g, ln_b)


_IN_OFFS = tuple(sum(IN_SIZES[:k]) for k in range(len(IN_SIZES) + 1))


def _pack_w_in_kernel(w_ref, o_ref):
    x = w_ref[0]
    rows = x.shape[0]
    lane = lax.broadcasted_iota(I32, (rows, LANES), 1)
    zero = jnp.zeros((rows, LANES), F32)
    o = _IN_OFFS
    half = ROPE_DIM // 2

    def put(col, val):
        o_ref[0, :, col:col + val.shape[1]] = val.astype(BF16)

    put(0, x[:, :o[4]])
    ik_blk = x[:, o[4]:o[4] + LANES]
    put(BLK_IK * LANES, jnp.where(lane < IDX_DIM, ik_blk, pltpu.roll(ik_blk, IDX_DIM, 1)))
    put(BLK_IW * LANES, jnp.where(lane < IDX_HEADS, x[:, o[5]:o[5] + LANES], zero))
    kr_blk = x[:, o[8]:o[8] + LANES]
    put(BLK_KRA * LANES, jnp.where(lane < ROPE_DIM, kr_blk, zero))
    swapped = jnp.where(lane < half, pltpu.roll(kr_blk, LANES - half, 1), pltpu.roll(kr_blk, half, 1))
    put(BLK_KRB * LANES, jnp.where(lane < ROPE_DIM, swapped, zero))
    put(BLK512_CQ * Q_LORA, x[:, o[6]:o[7]])
    put(BLK512_CKV * KV_LORA, x[:, o[7]:o[8]])
    put(BLK512_C * C_WIDTH, x[:, o[9]:o[10]])
    put(BLK2048_GATE0 * D_MODEL, x[:, o[10]:o[11]])


def _pack_w_in(w, rows=128):
    depth, d, cols = w.shape
    return pl.pallas_call(
        _pack_w_in_kernel,
        grid=(depth, d // rows),
        in_specs=[pl.BlockSpec((1, rows, cols), lambda l, i: (l, i, 0))],
        out_specs=pl.BlockSpec((1, rows, PROJ_COLS), lambda l, i: (l, i, 0)),
        out_shape=jax.ShapeDtypeStruct((depth, d, PROJ_COLS), BF16),
        compiler_params=_params(("parallel", "parallel")),
        name="pack_w_in",
    )(w)


def _pack_w_uq(w):
    half = ROPE_DIM // 2
    zeros = jnp.zeros(w.shape[:-1] + (LANES - ROPE_DIM,), w.dtype)
    cols = []
    for h in range(B_HEADS):
        base = h * (NOPE_DIM + ROPE_DIM)
        r = w[..., base + NOPE_DIM:base + NOPE_DIM + ROPE_DIM]
        cols += [w[..., base:base + NOPE_DIM], r, zeros, r[..., half:], r[..., :half], zeros]
    return jnp.concatenate(cols, axis=-1).astype(BF16)


def _pack_w_ukv(w):
    step = NOPE_DIM + V_DIM
    ks = [w[..., h * step:h * step + NOPE_DIM] for h in range(B_HEADS)]
    vs = [w[..., h * step + NOPE_DIM:(h + 1) * step] for h in range(B_HEADS)]
    return jnp.concatenate(ks + vs, axis=-1).astype(BF16)


def _rel_bucket(dist):
    n = jnp.maximum(dist, 0)
    max_exact = REL_BUCKETS // 2
    nf = jnp.maximum(n, 1).astype(F32)
    large = max_exact + (jnp.log(nf / max_exact) / math.log(REL_MAX_DIST / max_exact)
                         * (REL_BUCKETS - max_exact)).astype(I32)
    large = jnp.minimum(large, REL_BUCKETS - 1)
    return jnp.where(n < max_exact, n, large)


def _bias_tables(rel_bias):
    assert BLOCK + 1 >= REL_MAX_DIST
    s = jnp.arange(BLOCK)[:, None]
    t = jnp.arange(BLOCK)[None, :]
    tiles = []
    for block_gap in (1, 0):
        dist = t - s + block_gap * BLOCK
        onehot = (_rel_bucket(dist)[:, :, None] == jnp.arange(REL_BUCKETS)).astype(F32)
        looked_up = jnp.einsum("stb,bh->sth", onehot, rel_bias, precision=lax.Precision.HIGHEST)
        tile = (looked_up - rel_bias[REL_BUCKETS - 1]) * LOG2E
        tile = jnp.where((dist >= 0)[:, :, None], tile, -jnp.inf)
        tiles.append(jnp.transpose(tile, (2, 0, 1)))
    return jnp.stack(tiles).astype(F32)


def kernel(x, mem, positions, rel_bias, ln_g, ln_b, ffn1_up, ffn1_down, w_in, q_norm, kv_norm, w_uq, w_ukv,
           w_mem_kv, w_branch, w_out, ffn2_up, ffn2_down):
    bsz, seq, d = x.shape
    depth = ffn1_up.shape[0]
    n = bsz * seq
    alpha = (2 * depth) ** 0.25
    k_sel = min(TOPK_MAX, seq // 4)
    assert d == D_MODEL and seq % (2 * BLOCK) == 0

    inv_freq = ROPE_THETA ** (-jnp.arange(0, ROPE_DIM, 2, dtype=F32) / ROPE_DIM)
    ang = positions.astype(F32)[..., None] * inv_freq
    cos, sin = jnp.cos(ang).reshape(n, -1), jnp.sin(ang).reshape(n, -1)
    pad = jnp.zeros((n, LANES - ROPE_DIM), F32)
    cc = jnp.concatenate([cos, cos, pad], axis=1)
    ss = jnp.concatenate([-sin, sin, pad], axis=1)
    bias_tab = _bias_tables(rel_bias)

    xf = x.reshape(n, d)
    memf = mem.reshape(bsz * mem.shape[1], d)
    ln_g3 = ln_g.reshape(depth * 3, 1, d)
    ln_b3 = ln_b.reshape(depth * 3, 1, d)
    qn3 = q_norm.reshape(depth, 1, Q_LORA)
    kvn3 = kv_norm.reshape(depth, 1, KV_LORA)
    w_in_p = _pack_w_in(w_in)
    w_uq_p = _pack_w_uq(w_uq.astype(BF16))
    w_ukv_p = _pack_w_ukv(w_ukv.astype(BF16))
    w_branch_b = w_branch.astype(BF16)
    w_out_b = w_out.astype(BF16)

    for l in range(depth):
        xf = _ffn(xf, ffn1_up, ffn1_down, ln_g3, ln_b3, l, 0, alpha)
        proj = _proj(xf, w_in_p, l, 1024, 1024, "in_proj")
        proj3 = proj.reshape(bsz, seq, PROJ_COLS)
        o_a = _dsa(proj3, bias_tab, k_sel).reshape(n, A_WIDTH)
        qt, k, vt = _mla_proj(proj, cc, ss, qn3, kvn3, w_uq_p, w_ukv_p, l)
        o_b = _mla_attn(qt, k.reshape(bsz, seq, -1), vt).reshape(n, B_WIDTH)
        mkv = _proj(memf, w_mem_kv, l, 512, 1024, "mem_proj")
        o_c = _mem_attn(proj3, mkv.reshape(bsz, mem.shape[1], 2 * C_WIDTH)).reshape(n, C_WIDTH)
        xf = _mix_out(o_a, o_b, o_c, proj, xf, w_branch_b, w_out_b, ln_g3, ln_b3, l, alpha)
        xf = _ffn(xf, ffn2_up, ffn2_down, ln_g3, ln_b3, l, 2, alpha)
    return xf.reshape(bsz, seq, d)
```

```python
import functools
import math

import jax
import jax.numpy as jnp
from jax import lax
from jax.experimental import pallas as pl
from jax.experimental.pallas import tpu as pltpu

F32 = jnp.float32
BF16 = jnp.bfloat16
I32 = jnp.int32
I16 = jnp.int16

D_MODEL = 2048
D_FF = 5632
HEAD_DIM = 128
A_HEADS = 6
IDX_HEADS = 16
IDX_DIM = 64
TOPK_MAX = 256
B_HEADS = 6
Q_LORA = 512
KV_LORA = 512
NOPE_DIM = 128
ROPE_DIM = 64
V_DIM = 128
ROPE_THETA = 10000.0
C_HEADS = 4
REL_BUCKETS = 32
REL_MAX_DIST = 128
LN_EPS = 1e-5
RMS_EPS = 1e-6
A_WIDTH = A_HEADS * HEAD_DIM
B_WIDTH = B_HEADS * V_DIM
C_WIDTH = C_HEADS * HEAD_DIM
IN_SIZES = (A_WIDTH, HEAD_DIM, HEAD_DIM, IDX_HEADS * IDX_DIM, IDX_DIM, IDX_HEADS,
            Q_LORA, KV_LORA, ROPE_DIM, C_WIDTH, 3 * D_MODEL)

LANES = 128
SUBLANES = 8
BLOCK = 256
MLA_QK = 2 * LANES
INT_MIN = -2 ** 31
HALF_RANGE = 2 ** 15
LOG2E = 1.4426950408889634

PROJ_COLS = 10240
BLK_AK, BLK_AV = 6, 7
BLK_IK, BLK_IW, BLK_KRA, BLK_KRB = 16, 17, 18, 19
BLK512_CQ, BLK512_CKV, BLK512_C = 5, 6, 7
BLK2048_GATE0 = 2

VMEM_LIMIT = 56 * 1024 * 1024
FFN_VMEM_LIMIT = 60 * 1024 * 1024


def _params(sem, vmem=VMEM_LIMIT):
    return pltpu.CompilerParams(dimension_semantics=sem, vmem_limit_bytes=vmem)


def _layer_norm_rows(y, g, b):
    mu = jnp.mean(y, axis=-1, keepdims=True)
    d = y - mu
    var = jnp.mean(d * d, axis=-1, keepdims=True)
    return d * lax.rsqrt(var + LN_EPS) * g + b


def _ffn_kernel(x_ref, wg_ref, wu_ref, wd_ref, g_ref, b_ref, o_ref, xb_ref, *, alpha):
    j = pl.program_id(1)

    @pl.when(j == 0)
    def _():
        xb_ref[...] = x_ref[...].astype(BF16)
        o_ref[...] = jnp.zeros_like(o_ref)

    xb = xb_ref[...]
    gate = jnp.dot(xb, wg_ref[0].astype(BF16), preferred_element_type=F32)
    up = jnp.dot(xb, wu_ref[0].astype(BF16), preferred_element_type=F32)
    h = (gate * (1.0 / (1.0 + jnp.exp(-gate))) * up).astype(BF16)
    o_ref[...] += jnp.dot(h, wd_ref[0].astype(BF16), preferred_element_type=F32)

    @pl.when(j == pl.num_programs(1) - 1)
    def _():
        y = alpha * x_ref[...] + 0.5 * o_ref[...]
        o_ref[...] = _layer_norm_rows(y, g_ref[0], b_ref[0])


def _ffn(x, w_up, w_down, ln_g, ln_b, layer, which, alpha, tm=1024, tf=256):
    n = x.shape[0]
    tm = min(tm, n)
    nf = D_FF // tf
    return pl.pallas_call(
        functools.partial(_ffn_kernel, alpha=alpha),
        grid=(n // tm, nf),
        in_specs=[
            pl.BlockSpec((tm, D_MODEL), lambda i, j: (i, 0)),
            pl.BlockSpec((1, D_MODEL, tf), lambda i, j: (layer, 0, j)),
            pl.BlockSpec((1, D_MODEL, tf), lambda i, j: (layer, 0, j + nf)),
            pl.BlockSpec((1, tf, D_MODEL), lambda i, j: (layer, j, 0)),
            pl.BlockSpec((1, 1, D_MODEL), lambda i, j: (3 * layer + which, 0, 0)),
            pl.BlockSpec((1, 1, D_MODEL), lambda i, j: (3 * layer + which, 0, 0)),
        ],
        out_specs=pl.BlockSpec((tm, D_MODEL), lambda i, j: (i, 0)),
        out_shape=jax.ShapeDtypeStruct((n, D_MODEL), F32),
        scratch_shapes=[pltpu.VMEM((tm, D_MODEL), BF16)],
        compiler_params=_params(("parallel", "arbitrary"), FFN_VMEM_LIMIT),
        name="ffn",
    )(x, w_up, w_up, w_down, ln_g, ln_b)


def _proj_kernel(x_ref, w_ref, o_ref, xb_ref):
    @pl.when(pl.program_id(1) == 0)
    def _():
        xb_ref[...] = x_ref[...].astype(BF16)

    o_ref[...] = jnp.dot(xb_ref[...], w_ref[0].astype(BF16), preferred_element_type=F32).astype(o_ref.dtype)


def _proj(x, w, layer, tm, tn, name):
    n, k = x.shape
    tm = min(tm, n)
    cols = w.shape[2]
    return pl.pallas_call(
        _proj_kernel,
        grid=(n // tm, cols // tn),
        in_specs=[pl.BlockSpec((tm, k), lambda i, j: (i, 0)),
                  pl.BlockSpec((1, k, tn), lambda i, j: (layer, 0, j))],
        out_specs=pl.BlockSpec((tm, tn), lambda i, j: (i, j)),
        out_shape=jax.ShapeDtypeStruct((n, cols), BF16),
        scratch_shapes=[pltpu.VMEM((tm, k), BF16)],
        compiler_params=_params(("parallel", "arbitrary")),
        name=name,
    )(x, w)


def _dsa_kernel(aq_ref, iq_ref, iw_ref, ik_ref, ak_ref, av_ref, bias_ref, o_ref,
                skey_ref, hi_ref, lo_ref, ika_ref, ikb_ref, vat_ref, m_ref, l_ref, acc_ref,
                lg0_ref, lg1_ref, p_ref, al_ref, *, k_sel, seq):
    i = pl.program_id(1)
    blk = BLOCK
    lg_refs = (lg0_ref, lg1_ref)

    @pl.when(i == 0)
    def _():
        ik = ik_ref[0]
        lane = lax.broadcasted_iota(I32, ik.shape, 1)
        zero = jnp.zeros_like(ik)
        ika_ref[...] = jnp.where(lane < IDX_DIM, ik, zero)
        ikb_ref[...] = jnp.where(lane >= IDX_DIM, ik, zero)
        for c in range(seq // blk):
            vat_ref[c] = av_ref[0, c * blk:(c + 1) * blk, :].astype(F32).T.astype(BF16)

    iqt = iq_ref[0].astype(F32).T.astype(BF16)
    wt = iw_ref[0].astype(F32).T * (IDX_HEADS * IDX_DIM) ** -0.5

    srow = lax.broadcasted_iota(I32, (blk, blk), 0)
    tcol = lax.broadcasted_iota(I32, (blk, blk), 1)

    def score_body(c, carry):
        s0 = pl.multiple_of(c * blk, blk)
        ka = ika_ref[pl.ds(s0, blk), :]
        kb = ikb_ref[pl.ds(s0, blk), :]
        sc = jnp.zeros((blk, blk), F32)
        for j in range(IDX_HEADS // 2):
            rhs = iqt[j * LANES:(j + 1) * LANES, :]
            sc = sc + wt[2 * j:2 * j + 1, :] * jnp.maximum(jnp.dot(ka, rhs, preferred_element_type=F32), 0.0)
            sc = sc + wt[2 * j + 1:2 * j + 2, :] * jnp.maximum(jnp.dot(kb, rhs, preferred_element_type=F32), 0.0)
        bits = pltpu.bitcast(sc, I32)
        key = bits ^ ((bits >> 31) & jnp.int32(0x7FFFFFFF))
        causal = srow <= tcol + jnp.where(c < i, jnp.int32(blk), jnp.int32(0))
        key = jnp.where(causal, key, jnp.int32(INT_MIN))
        skey_ref[pl.ds(s0, blk), :] = key
        hi_ref[pl.ds(s0, blk), :] = (key >> 16).astype(I16)
        lo_ref[pl.ds(s0, blk), :] = ((key & jnp.int32(0xFFFF)) - HALF_RANGE).astype(I16)
        return carry

    lax.fori_loop(0, i + 1, score_body, 0)

    n_acc = 4
    rows16 = 2 * SUBLANES

    def count_at_least(ref, trial):
        trial16 = jnp.broadcast_to(trial.astype(I16), (rows16, blk))

        def cnt_body(c, accs):
            vals = ref[pl.ds(pl.multiple_of(c * blk, blk), blk), :]
            accs = list(accs)
            for r in range(blk // rows16):
                rows = vals[r * rows16:(r + 1) * rows16, :]
                accs[r % n_acc] = jnp.where(rows >= trial16, accs[r % n_acc] + jnp.int16(1), accs[r % n_acc])
            return tuple(accs)

        zero = jnp.zeros((rows16, blk), I16)
        accs = lax.fori_loop(0, i + 1, cnt_body, (zero,) * n_acc)
        total = (accs[0] + accs[1]) + (accs[2] + accs[3])
        return jnp.sum(total.astype(I32), axis=0, keepdims=True)

    def largest_with_count(ref, need):
        def bit_body(it, u):
            trial_u = u | jnp.left_shift(jnp.int32(1), jnp.int32(15) - it)
            return jnp.where(count_at_least(ref, trial_u - HALF_RANGE) >= need, trial_u, u)

        return lax.fori_loop(0, 16, bit_body, jnp.zeros((1, blk), I32))

    hi_thr = largest_with_count(hi_ref, k_sel) - HALF_RANGE
    above = jnp.where(hi_thr == HALF_RANGE - 1, 0, count_at_least(hi_ref, jnp.minimum(hi_thr + 1, HALF_RANGE - 1)))
    hi_thr16 = jnp.broadcast_to(hi_thr.astype(I16), (blk, blk))

    def keep_candidates(c, carry):
        rows = pl.ds(pl.multiple_of(c * blk, blk), blk)
        lo_ref[rows, :] = jnp.where(hi_ref[rows, :] == hi_thr16, lo_ref[rows, :], jnp.int16(-HALF_RANGE))
        return carry

    lax.fori_loop(0, i + 1, keep_candidates, 0)
    lo_thr = largest_with_count(lo_ref, k_sel - above)
    thr = hi_thr * (2 * HALF_RANGE) + lo_thr

    qt = (aq_ref[0].astype(F32) * (HEAD_DIM ** -0.5 * LOG2E)).T.astype(BF16)

    m_ref[...] = jnp.full_like(m_ref, -jnp.inf)
    l_ref[...] = jnp.zeros_like(l_ref)
    acc_ref[...] = jnp.zeros_like(acc_ref)

    def logits(c):
        ka = ak_ref[0, pl.ds(pl.multiple_of(c * blk, blk), blk), :]
        return [jnp.dot(ka, qt[h * LANES:(h + 1) * LANES, :], preferred_element_type=F32) for h in range(A_HEADS)]

    def accumulate(c, slot):
        vt = vat_ref[c]
        for h in range(A_HEADS):
            pv = jnp.dot(vt, p_ref[slot, h], preferred_element_type=F32)
            acc_ref[h] = acc_ref[h] * al_ref[slot, h] + pv

    p_ref[1] = jnp.zeros_like(p_ref[1])
    al_ref[1] = jnp.ones_like(al_ref[1])
    first = logits(0)
    for h in range(A_HEADS):
        lg0_ref[h] = first[h]

    def stage(c, slot, has_next=True):
        if has_next:
            nxt = logits(jnp.minimum(c + 1, i))
            for h in range(A_HEADS):
                lg_refs[1 - slot][h] = nxt[h]
        accumulate(jnp.maximum(c - 1, 0), 1 - slot)
        kind = jnp.clip(c - i + 2, 0, 2)
        sel = skey_ref[pl.ds(pl.multiple_of(c * blk, blk), blk), :] >= thr
        for h in range(A_HEADS):
            lg = jnp.where(sel, lg_refs[slot][h] + bias_ref[kind, h], -jnp.inf)
            m_old = m_ref[h]
            m_new = jnp.maximum(m_old, jnp.max(lg, axis=0, keepdims=True))
            m_use = jnp.where(m_new == -jnp.inf, 0.0, m_new)
            alpha = jnp.exp2(m_old - m_use)
            p = jnp.exp2(lg - m_use)
            l_ref[h] = l_ref[h] * alpha + jnp.sum(p, axis=0, keepdims=True)
            m_ref[h] = m_new
            p_ref[slot, h] = p.astype(BF16)
            al_ref[slot, h] = alpha

    def pair_body(j, carry):
        stage(2 * j, 0)
        stage(2 * j + 1, 1)
        return carry

    n_chunks = i + 1
    lax.fori_loop(0, n_chunks // 2, pair_body, 0)

    @pl.when(n_chunks % 2 == 1)
    def _():
        stage(i, 0, has_next=False)

    accumulate(i, i % 2)

    for h in range(A_HEADS):
        o_ref[0, :, h * LANES:(h + 1) * LANES] = (acc_ref[h] / l_ref[h]).T.astype(BF16)


def _dsa(proj3, bias_tab, k_sel):
    bsz, seq, _ = proj3.shape
    blk = BLOCK
    return pl.pallas_call(
        functools.partial(_dsa_kernel, k_sel=k_sel, seq=seq),
        grid=(bsz, seq // blk),
        in_specs=[
            pl.BlockSpec((1, blk, A_WIDTH), lambda b, i: (b, i, 0)),
            pl.BlockSpec((1, blk, IDX_HEADS * IDX_DIM), lambda b, i: (b, i, 1)),
            pl.BlockSpec((1, blk, LANES), lambda b, i: (b, i, BLK_IW)),
            pl.BlockSpec((1, seq, LANES), lambda b, i: (b, 0, BLK_IK)),
            pl.BlockSpec((1, seq, LANES), lambda b, i: (b, 0, BLK_AK)),
            pl.BlockSpec((1, seq, LANES), lambda b, i: (b, 0, BLK_AV)),
            pl.BlockSpec(bias_tab.shape, lambda b, i: (0, 0, 0, 0)),
        ],
        out_specs=pl.BlockSpec((1, blk, A_WIDTH), lambda b, i: (b, i, 0)),
        out_shape=jax.ShapeDtypeStruct((bsz, seq, A_WIDTH), BF16),
        scratch_shapes=[
            pltpu.VMEM((seq, blk), I32),
            pltpu.VMEM((seq, blk), I16),
            pltpu.VMEM((seq, blk), I16),
            pltpu.VMEM((seq, LANES), BF16),
            pltpu.VMEM((seq, LANES), BF16),
            pltpu.VMEM((seq // blk, HEAD_DIM, blk), BF16),
            pltpu.VMEM((A_HEADS, 1, blk), F32),
            pltpu.VMEM((A_HEADS, 1, blk), F32),
            pltpu.VMEM((A_HEADS, HEAD_DIM, blk), F32),
            pltpu.VMEM((A_HEADS, blk, blk), F32),
            pltpu.VMEM((A_HEADS, blk, blk), F32),
            pltpu.VMEM((2, A_HEADS, blk, blk), BF16),
            pltpu.VMEM((2, A_HEADS, 1, blk), F32),
        ],
        compiler_params=_params(("parallel", "arbitrary")),
        name="dsa",
    )(proj3, proj3, proj3, proj3, proj3, proj3, bias_tab)


def _rms_rows(x, g):
    return x * lax.rsqrt(jnp.mean(x * x, axis=-1, keepdims=True) + RMS_EPS) * g


def _mla_proj_kernel(cq_ref, ckv_ref, kra_ref, krb_ref, cc_ref, ss_ref, qn_ref, kvn_ref, wq_ref, wkv_ref,
                     qt_ref, k_ref, vt_ref):
    tm = cq_ref.shape[0]
    cc = cc_ref[...]
    ss = ss_ref[...]
    qscale = (NOPE_DIM + ROPE_DIM) ** -0.5 * LOG2E
    cq = _rms_rows(cq_ref[...].astype(F32), qn_ref[0]).astype(BF16)
    q3 = jnp.dot(cq, wq_ref[0], preferred_element_type=F32)
    for h in range(B_HEADS):
        base = 3 * LANES * h
        rot = q3[:, base + LANES:base + 2 * LANES] * cc + q3[:, base + 2 * LANES:base + 3 * LANES] * ss
        qt_ref[h, 0:LANES, :] = (q3[:, base:base + LANES] * qscale).T.astype(BF16)
        qt_ref[h, LANES:MLA_QK, :] = (rot * qscale).T.astype(BF16)
    ckv = _rms_rows(ckv_ref[...].astype(F32), kvn_ref[0]).astype(BF16)
    kv = jnp.dot(ckv, wkv_ref[0], preferred_element_type=F32)
    krot = (kra_ref[...].astype(F32) * cc + krb_ref[...].astype(F32) * ss).astype(BF16)
    for h in range(B_HEADS):
        k_ref[:, MLA_QK * h:MLA_QK * h + LANES] = kv[:, h * LANES:(h + 1) * LANES].astype(BF16)
        k_ref[:, MLA_QK * h + LANES:MLA_QK * (h + 1)] = krot
        v_h = kv[:, B_HEADS * NOPE_DIM + h * V_DIM:B_HEADS * NOPE_DIM + (h + 1) * V_DIM]
        for c in range(tm // BLOCK):
            vt_ref[h, c] = v_h[c * BLOCK:(c + 1) * BLOCK, :].T.astype(BF16)


def _mla_proj(proj, cc, ss, qn, kvn, wq, wkv, layer, tm=512):
    n = proj.shape[0]
    per_layer = lambda i: (layer, 0, 0)
    return pl.pallas_call(
        _mla_proj_kernel,
        grid=(n // tm,),
        in_specs=[
            pl.BlockSpec((tm, Q_LORA), lambda i: (i, BLK512_CQ)),
            pl.BlockSpec((tm, KV_LORA), lambda i: (i, BLK512_CKV)),
            pl.BlockSpec((tm, LANES), lambda i: (i, BLK_KRA)),
            pl.BlockSpec((tm, LANES), lambda i: (i, BLK_KRB)),
            pl.BlockSpec((tm, LANES), lambda i: (i, 0)),
            pl.BlockSpec((tm, LANES), lambda i: (i, 0)),
            pl.BlockSpec((1, 1, Q_LORA), per_layer),
            pl.BlockSpec((1, 1, KV_LORA), per_layer),
            pl.BlockSpec((1,) + wq.shape[1:], per_layer),
            pl.BlockSpec((1,) + wkv.shape[1:], per_layer),
        ],
        out_specs=[
            pl.BlockSpec((B_HEADS, MLA_QK, tm), lambda i: (0, 0, i)),
            pl.BlockSpec((tm, B_HEADS * MLA_QK), lambda i: (i, 0)),
            pl.BlockSpec((B_HEADS, tm // BLOCK, V_DIM, BLOCK), lambda i: (0, i, 0, 0)),
        ],
        out_shape=[jax.ShapeDtypeStruct((B_HEADS, MLA_QK, n), BF16),
                   jax.ShapeDtypeStruct((n, B_HEADS * MLA_QK), BF16),
                   jax.ShapeDtypeStruct((B_HEADS, n // BLOCK, V_DIM, BLOCK), BF16)],
        compiler_params=_params(("parallel",)),
        name="mla_proj",
    )(proj, proj, proj, proj, cc, ss, qn, kvn, wq, wkv)


def _mla_attn_kernel(qt_ref, k_ref, vt_ref, o_ref, lg0_ref, lg1_ref, p_ref, al_ref, m_ref, l_ref, acc_ref):
    lg_refs = (lg0_ref, lg1_ref)
    qi = pl.program_id(1)
    blk = BLOCK
    srow = lax.broadcasted_iota(I32, (blk, blk), 0)
    tcol = lax.broadcasted_iota(I32, (blk, blk), 1)
    causal_bias = jnp.where(srow <= tcol, 0.0, -jnp.inf).astype(F32)

    def logits(c):
        s0 = pl.multiple_of(c * blk, blk)
        return [jnp.dot(k_ref[0, pl.ds(s0, blk), h * MLA_QK:(h + 1) * MLA_QK], qt_ref[h],
                        preferred_element_type=F32) for h in range(B_HEADS)]

    def accumulate(c, slot):
        for h in range(B_HEADS):
            pv = jnp.dot(vt_ref[h, c], p_ref[slot, h], preferred_element_type=F32)
            acc_ref[h] = acc_ref[h] * al_ref[slot, h] + pv

    m_ref[...] = jnp.full_like(m_ref, -jnp.inf)
    l_ref[...] = jnp.zeros_like(l_ref)
    acc_ref[...] = jnp.zeros_like(acc_ref)
    p_ref[1] = jnp.zeros_like(p_ref[1])
    al_ref[1] = jnp.ones_like(al_ref[1])
    first = logits(0)
    for h in range(B_HEADS):
        lg0_ref[h] = first[h]
    def stage(c, slot, has_next=True):
        if has_next:
            nxt = logits(jnp.minimum(c + 1, qi))
            for h in range(B_HEADS):
                lg_refs[1 - slot][h] = nxt[h]
        accumulate(jnp.maximum(c - 1, 0), 1 - slot)
        mask = jnp.where(c == qi, causal_bias, 0.0)
        for h in range(B_HEADS):
            lg = lg_refs[slot][h] + mask
            m_old = m_ref[h]
            m_new = jnp.maximum(m_old, jnp.max(lg, axis=0, keepdims=True))
            alpha = jnp.exp2(m_old - m_new)
            p = jnp.exp2(lg - m_new)
            l_ref[h] = l_ref[h] * alpha + jnp.sum(p, axis=0, keepdims=True)
            m_ref[h] = m_new
            p_ref[slot, h] = p.astype(BF16)
            al_ref[slot, h] = alpha

    def pair_body(j, carry):
        stage(2 * j, 0)
        stage(2 * j + 1, 1)
        return carry

    n_chunks = qi + 1
    lax.fori_loop(0, n_chunks // 2, pair_body, 0)

    @pl.when(n_chunks % 2 == 1)
    def _():
        stage(qi, 0, has_next=False)

    accumulate(qi, qi % 2)
    for h in range(B_HEADS):
        o_ref[0, :, h * V_DIM:(h + 1) * V_DIM] = (acc_ref[h] / l_ref[h]).T.astype(BF16)


def _mla_attn(qt, k3, vt):
    bsz, seq, _ = k3.shape
    blk = BLOCK
    nq = seq // blk
    return pl.pallas_call(
        _mla_attn_kernel,
        grid=(bsz, nq),
        in_specs=[
            pl.BlockSpec((B_HEADS, MLA_QK, blk), lambda b, i: (0, 0, b * nq + i)),
            pl.BlockSpec((1, seq, B_HEADS * MLA_QK), lambda b, i: (b, 0, 0)),
            pl.BlockSpec((B_HEADS, nq, V_DIM, blk), lambda b, i: (0, b, 0, 0)),
        ],
        out_specs=pl.BlockSpec((1, blk, B_WIDTH), lambda b, i: (b, i, 0)),
        out_shape=jax.ShapeDtypeStruct((bsz, seq, B_WIDTH), BF16),
        scratch_shapes=[
            pltpu.VMEM((B_HEADS, blk, blk), F32),
            pltpu.VMEM((B_HEADS, blk, blk), F32),
            pltpu.VMEM((2, B_HEADS, blk, blk), BF16),
            pltpu.VMEM((2, B_HEADS, 1, blk), F32),
            pltpu.VMEM((B_HEADS, 1, blk), F32),
            pltpu.VMEM((B_HEADS, 1, blk), F32),
            pltpu.VMEM((B_HEADS, V_DIM, blk), F32),
        ],
        compiler_params=_params(("parallel", "arbitrary")),
        name="mla_attn",
    )(qt, k3, vt)


def _mem_attn_kernel(q_ref, kv_ref, o_ref):
    scale = HEAD_DIM ** -0.5
    for h in range(C_HEADS):
        q = q_ref[0, :, h * LANES:(h + 1) * LANES]
        k = kv_ref[0, :, h * LANES:(h + 1) * LANES]
        v = kv_ref[0, :, C_WIDTH + h * LANES:C_WIDTH + (h + 1) * LANES]
        s = lax.dot_general(q, k, (((1,), (1,)), ((), ())), preferred_element_type=F32) * scale
        p = jnp.exp(s - jnp.max(s, axis=-1, keepdims=True))
        l = jnp.sum(p, axis=-1, keepdims=True)
        o = jnp.dot(p.astype(BF16), v, preferred_element_type=F32) / l
        o_ref[0, :, h * LANES:(h + 1) * LANES] = o.astype(BF16)


def _mem_attn(proj3, mkv3, tq=512):
    bsz, seq, _ = proj3.shape
    mlen = mkv3.shape[1]
    return pl.pallas_call(
        _mem_attn_kernel,
        grid=(bsz, seq // tq),
        in_specs=[pl.BlockSpec((1, tq, C_WIDTH), lambda b, i: (b, i, BLK512_C)),
                  pl.BlockSpec((1, mlen, 2 * C_WIDTH), lambda b, i: (b, 0, 0))],
        out_specs=pl.BlockSpec((1, tq, C_WIDTH), lambda b, i: (b, i, 0)),
        out_shape=jax.ShapeDtypeStruct((bsz, seq, C_WIDTH), BF16),
        compiler_params=_params(("parallel", "arbitrary")),
        name="mem_attn",
    )(proj3, mkv3)


def _mix_out_kernel(oa_ref, ob_ref, oc_ref, ga_ref, gb_ref, gc_ref, x_ref, wb_ref, wo_ref, g_ref, b_ref,
                    o_ref, *, alpha):
    def gated(o_blk, gate_ref, r0, r1):
        y = jnp.dot(o_blk, wb_ref[0, r0:r1, :], preferred_element_type=F32)
        return y * (1.0 / (1.0 + jnp.exp(-gate_ref[...].astype(F32))))

    y = gated(oa_ref[...], ga_ref, 0, A_WIDTH)
    y = y + gated(ob_ref[...], gb_ref, A_WIDTH, A_WIDTH + B_WIDTH)
    y = y + gated(oc_ref[...], gc_ref, A_WIDTH + B_WIDTH, A_WIDTH + B_WIDTH + C_WIDTH)
    mix = jnp.dot(y.astype(BF16), wo_ref[0], preferred_element_type=F32)
    o_ref[...] = _layer_norm_rows(alpha * x_ref[...] + mix, g_ref[0], b_ref[0])


def _mix_out(oa, ob, oc, proj, x, wb, wo, ln_g, ln_b, layer, alpha, tm=256):
    n = x.shape[0]
    per_layer = lambda i: (layer, 0, 0)
    ln_row = lambda i: (3 * layer + 1, 0, 0)
    return pl.pallas_call(
        functools.partial(_mix_out_kernel, alpha=alpha),
        grid=(n // tm,),
        in_specs=[
            pl.BlockSpec((tm, A_WIDTH), lambda i: (i, 0)),
            pl.BlockSpec((tm, B_WIDTH), lambda i: (i, 0)),
            pl.BlockSpec((tm, C_WIDTH), lambda i: (i, 0)),
            pl.BlockSpec((tm, D_MODEL), lambda i: (i, BLK2048_GATE0)),
            pl.BlockSpec((tm, D_MODEL), lambda i: (i, BLK2048_GATE0 + 1)),
            pl.BlockSpec((tm, D_MODEL), lambda i: (i, BLK2048_GATE0 + 2)),
            pl.BlockSpec((tm, D_MODEL), lambda i: (i, 0)),
            pl.BlockSpec((1,) + wb.shape[1:], per_layer),
            pl.BlockSpec((1,) + wo.shape[1:], per_layer),
            pl.BlockSpec((1, 1, D_MODEL), ln_row),
            pl.BlockSpec((1, 1, D_MODEL), ln_row),
        ],
        out_specs=pl.BlockSpec((tm, D_MODEL), lambda i: (i, 0)),
        out_shape=jax.ShapeDtypeStruct((n, D_MODEL), F32),
        compiler_params=_params(("parallel",)),
        name="mix_out",
    )(oa, ob, oc, proj, proj, proj, x, wb, wo, ln_g, ln_b)


_IN_OFFS = tuple(sum(IN_SIZES[:k]) for k in range(len(IN_SIZES) + 1))


def _pack_w_in_kernel(w_ref, o_ref):
    x = w_ref[0]
    rows = x.shape[0]
    lane = lax.broadcasted_iota(I32, (rows, LANES), 1)
    zero = jnp.zeros((rows, LANES), F32)
    o = _IN_OFFS
    half = ROPE_DIM // 2

    def put(col, val):
        o_ref[0, :, col:col + val.shape[1]] = val.astype(BF16)

    put(0, x[:, :o[4]])
    ik_blk = x[:, o[4]:o[4] + LANES]
    put(BLK_IK * LANES, jnp.where(lane < IDX_DIM, ik_blk, pltpu.roll(ik_blk, IDX_DIM, 1)))
    put(BLK_IW * LANES, jnp.where(lane < IDX_HEADS, x[:, o[5]:o[5] + LANES], zero))
    kr_blk = x[:, o[8]:o[8] + LANES]
    put(BLK_KRA * LANES, jnp.where(lane < ROPE_DIM, kr_blk, zero))
    swapped = jnp.where(lane < half, pltpu.roll(kr_blk, LANES - half, 1), pltpu.roll(kr_blk, half, 1))
    put(BLK_KRB * LANES, jnp.where(lane < ROPE_DIM, swapped, zero))
    put(BLK512_CQ * Q_LORA, x[:, o[6]:o[7]])
    put(BLK512_CKV * KV_LORA, x[:, o[7]:o[8]])
    put(BLK512_C * C_WIDTH, x[:, o[9]:o[10]])
    put(BLK2048_GATE0 * D_MODEL, x[:, o[10]:o[11]])


def _pack_w_in(w, rows=128):
    depth, d, cols = w.shape
    return pl.pallas_call(
        _pack_w_in_kernel,
        grid=(depth, d // rows),
        in_specs=[pl.BlockSpec((1, rows, cols), lambda l, i: (l, i, 0))],
        out_specs=pl.BlockSpec((1, rows, PROJ_COLS), lambda l, i: (l, i, 0)),
        out_shape=jax.ShapeDtypeStruct((depth, d, PROJ_COLS), BF16),
        compiler_params=_params(("parallel", "parallel")),
        name="pack_w_in",
    )(w)


def _pack_w_uq(w):
    half = ROPE_DIM // 2
    zeros = jnp.zeros(w.shape[:-1] + (LANES - ROPE_DIM,), w.dtype)
    cols = []
    for h in range(B_HEADS):
        base = h * (NOPE_DIM + ROPE_DIM)
        r = w[..., base + NOPE_DIM:base + NOPE_DIM + ROPE_DIM]
        cols += [w[..., base:base + NOPE_DIM], r, zeros, r[..., half:], r[..., :half], zeros]
    return jnp.concatenate(cols, axis=-1).astype(BF16)


def _pack_w_ukv(w):
    step = NOPE_DIM + V_DIM
    ks = [w[..., h * step:h * step + NOPE_DIM] for h in range(B_HEADS)]
    vs = [w[..., h * step + NOPE_DIM:(h + 1) * step] for h in range(B_HEADS)]
    return jnp.concatenate(ks + vs, axis=-1).astype(BF16)


def _rel_bucket(dist):
    n = jnp.maximum(dist, 0)
    max_exact = REL_BUCKETS // 2
    nf = jnp.maximum(n, 1).astype(F32)
    large = max_exact + (jnp.log(nf / max_exact) / math.log(REL_MAX_DIST / max_exact)
                         * (REL_BUCKETS - max_exact)).astype(I32)
    large = jnp.minimum(large, REL_BUCKETS - 1)
    return jnp.where(n < max_exact, n, large)


def _bias_tables(rel_bias):
    assert BLOCK + 1 >= REL_MAX_DIST
    s = jnp.arange(BLOCK)[:, None]
    t = jnp.arange(BLOCK)[None, :]
    tiles = [jnp.zeros((A_HEADS, BLOCK, BLOCK), F32)]
    for block_gap in (1, 0):
        dist = t - s + block_gap * BLOCK
        onehot = (_rel_bucket(dist)[:, :, None] == jnp.arange(REL_BUCKETS)).astype(F32)
        looked_up = jnp.einsum("stb,bh->sth", onehot, rel_bias, precision=lax.Precision.HIGHEST)
        tile = (looked_up - rel_bias[REL_BUCKETS - 1]) * LOG2E
        tile = jnp.where((dist >= 0)[:, :, None], tile, -jnp.inf)
        tiles.append(jnp.transpose(tile, (2, 0, 1)))
    return jnp.stack(tiles).astype(F32)


def kernel(x, mem, positions, rel_bias, ln_g, ln_b, ffn1_up, ffn1_down, w_in, q_norm, kv_norm, w_uq, w_ukv,
           w_mem_kv, w_branch, w_out, ffn2_up, ffn2_down):
    bsz, seq, d = x.shape
    depth = ffn1_up.shape[0]
    n = bsz * seq
    alpha = (2 * depth) ** 0.25
    k_sel = min(TOPK_MAX, seq // 4)
    assert d == D_MODEL and seq % (2 * BLOCK) == 0

    inv_freq = ROPE_THETA ** (-jnp.arange(0, ROPE_DIM, 2, dtype=F32) / ROPE_DIM)
    ang = positions.astype(F32)[..., None] * inv_freq
    cos, sin = jnp.cos(ang).reshape(n, -1), jnp.sin(ang).reshape(n, -1)
    pad = jnp.zeros((n, LANES - ROPE_DIM), F32)
    cc = jnp.concatenate([cos, cos, pad], axis=1)
    ss = jnp.concatenate([-sin, sin, pad], axis=1)
    bias_tab = _bias_tables(rel_bias)

    xf = x.reshape(n, d)
    memf = mem.reshape(bsz * mem.shape[1], d)
    ln_g3 = ln_g.reshape(depth * 3, 1, d)
    ln_b3 = ln_b.reshape(depth * 3, 1, d)
    qn3 = q_norm.reshape(depth, 1, Q_LORA)
    kvn3 = kv_norm.reshape(depth, 1, KV_LORA)
    w_in_p = _pack_w_in(w_in)
    w_uq_p = _pack_w_uq(w_uq.astype(BF16))
    w_ukv_p = _pack_w_ukv(w_ukv.astype(BF16))
    w_branch_b = w_branch.astype(BF16)
    w_out_b = w_out.astype(BF16)

    for l in range(depth):
        xf = _ffn(xf, ffn1_up, ffn1_down, ln_g3, ln_b3, l, 0, alpha)
        proj = _proj(xf, w_in_p, l, 1024, 1024, "in_proj")
        proj3 = proj.reshape(bsz, seq, PROJ_COLS)
        o_a = _dsa(proj3, bias_tab, k_sel).reshape(n, A_WIDTH)
        qt, k, vt = _mla_proj(proj, cc, ss, qn3, kvn3, w_uq_p, w_ukv_p, l)
        o_b = _mla_attn(qt, k.reshape(bsz, seq, -1), vt).reshape(n, B_WIDTH)
        mkv = _proj(memf, w_mem_kv, l, 512, 1024, "mem_proj")
        o_c = _mem_attn(proj3, mkv.reshape(bsz, mem.shape[1], 2 * C_WIDTH)).reshape(n, C_WIDTH)
        xf = _mix_out(o_a, o_b, o_c, proj, xf, w_branch_b, w_out_b, ln_g3, ln_b3, l, alpha)
        xf = _ffn(xf, ffn2_up, ffn2_down, ln_g3, ln_b3, l, 2, alpha)
    return xf.reshape(bsz, seq, d)
```

```python
import functools
import math

import jax
import jax.numpy as jnp
from jax import lax
from jax.experimental import pallas as pl
from jax.experimental.pallas import tpu as pltpu

F32 = jnp.float32
BF16 = jnp.bfloat16
I32 = jnp.int32
I16 = jnp.int16

D_MODEL = 2048
D_FF = 5632
HEAD_DIM = 128
A_HEADS = 6
IDX_HEADS = 16
IDX_DIM = 64
TOPK_MAX = 256
B_HEADS = 6
Q_LORA = 512
KV_LORA = 512
NOPE_DIM = 128
ROPE_DIM = 64
V_DIM = 128
ROPE_THETA = 10000.0
C_HEADS = 4
REL_BUCKETS = 32
REL_MAX_DIST = 128
LN_EPS = 1e-5
RMS_EPS = 1e-6
A_WIDTH = A_HEADS * HEAD_DIM
B_WIDTH = B_HEADS * V_DIM
C_WIDTH = C_HEADS * HEAD_DIM
IN_SIZES = (A_WIDTH, HEAD_DIM, HEAD_DIM, IDX_HEADS * IDX_DIM, IDX_DIM, IDX_HEADS,
            Q_LORA, KV_LORA, ROPE_DIM, C_WIDTH, 3 * D_MODEL)

LANES = 128
SUBLANES = 8
BLOCK = 256
MLA_QK = 2 * LANES
INT_MIN = -2 ** 31
HALF_RANGE = 2 ** 15
LOG2E = 1.4426950408889634

PROJ_COLS = 10240
BLK_AK, BLK_AV = 6, 7
BLK_IK, BLK_IW, BLK_KRA, BLK_KRB = 16, 17, 18, 19
BLK512_CQ, BLK512_CKV, BLK512_C = 5, 6, 7
BLK2048_GATE0 = 2

VMEM_LIMIT = 56 * 1024 * 1024
FFN_VMEM_LIMIT = 60 * 1024 * 1024


def _params(sem, vmem=VMEM_LIMIT):
    return pltpu.CompilerParams(dimension_semantics=sem, vmem_limit_bytes=vmem)


def _layer_norm_rows(y, g, b):
    mu = jnp.mean(y, axis=-1, keepdims=True)
    d = y - mu
    var = jnp.mean(d * d, axis=-1, keepdims=True)
    return d * lax.rsqrt(var + LN_EPS) * g + b


def _ffn_kernel(x_ref, wg_ref, wu_ref, wd_ref, g_ref, b_ref, o_ref, xb_ref, *, alpha):
    j = pl.program_id(1)

    @pl.when(j == 0)
    def _():
        x = x_ref[...]
        xb_ref[...] = x.astype(BF16)
        o_ref[...] = (2.0 * alpha) * x

    xb = xb_ref[...]
    gate = jnp.dot(xb, wg_ref[0].astype(BF16), preferred_element_type=F32)
    up = jnp.dot(xb, wu_ref[0].astype(BF16), preferred_element_type=F32)
    h = (gate * (1.0 / (1.0 + jnp.exp(-gate))) * up).astype(BF16)
    o_ref[...] += jnp.dot(h, wd_ref[0].astype(BF16), preferred_element_type=F32)

    @pl.when(j == pl.num_programs(1) - 1)
    def _():
        o_ref[...] = _layer_norm_rows(0.5 * o_ref[...], g_ref[0], b_ref[0])


def _ffn(x, w_up, w_down, ln_g, ln_b, layer, which, alpha, tm=1024, tf=256):
    n = x.shape[0]
    tm = min(tm, n)
    nf = D_FF // tf
    return pl.pallas_call(
        functools.partial(_ffn_kernel, alpha=alpha),
        grid=(n // tm, nf),
        in_specs=[
            pl.BlockSpec((tm, D_MODEL), lambda i, j: (i, 0)),
            pl.BlockSpec((1, D_MODEL, tf), lambda i, j: (layer, 0, j)),
            pl.BlockSpec((1, D_MODEL, tf), lambda i, j: (layer, 0, j + nf)),
            pl.BlockSpec((1, tf, D_MODEL), lambda i, j: (layer, j, 0)),
            pl.BlockSpec((1, 1, D_MODEL), lambda i, j: (3 * layer + which, 0, 0)),
            pl.BlockSpec((1, 1, D_MODEL), lambda i, j: (3 * layer + which, 0, 0)),
        ],
        out_specs=pl.BlockSpec((tm, D_MODEL), lambda i, j: (i, 0)),
        out_shape=jax.ShapeDtypeStruct((n, D_MODEL), F32),
        scratch_shapes=[pltpu.VMEM((tm, D_MODEL), BF16)],
        compiler_params=_params(("parallel", "arbitrary"), FFN_VMEM_LIMIT),
        name="ffn",
    )(x, w_up, w_up, w_down, ln_g, ln_b)


def _proj_kernel(x_ref, w_ref, o_ref, xb_ref, *, w_is_transposed):
    @pl.when(pl.program_id(1) == 0)
    def _():
        xb_ref[...] = x_ref[...].astype(BF16)

    contract_w = 1 if w_is_transposed else 0
    o_ref[...] = lax.dot_general(xb_ref[...], w_ref[0].astype(BF16), (((1,), (contract_w,)), ((), ())),
                                 preferred_element_type=F32).astype(o_ref.dtype)


def _proj(x, w, layer, tm, tn, name, w_is_transposed=False):
    n, k = x.shape
    tm = min(tm, n)
    cols = w.shape[1] if w_is_transposed else w.shape[2]
    w_spec = (pl.BlockSpec((1, tn, k), lambda i, j: (layer, j, 0)) if w_is_transposed
              else pl.BlockSpec((1, k, tn), lambda i, j: (layer, 0, j)))
    return pl.pallas_call(
        functools.partial(_proj_kernel, w_is_transposed=w_is_transposed),
        grid=(n // tm, cols // tn),
        in_specs=[pl.BlockSpec((tm, k), lambda i, j: (i, 0)), w_spec],
        out_specs=pl.BlockSpec((tm, tn), lambda i, j: (i, j)),
        out_shape=jax.ShapeDtypeStruct((n, cols), BF16),
        scratch_shapes=[pltpu.VMEM((tm, k), BF16)],
        compiler_params=_params(("parallel", "arbitrary")),
        name=name,
    )(x, w)


def _dsa_kernel(aq_ref, iq_ref, iw_ref, ik_ref, ak_ref, av_ref, bias_ref, o_ref,
                skey_ref, hi_ref, lo_ref, ika_ref, ikb_ref, vat_ref, m_ref, l_ref, acc_ref,
                lg0_ref, lg1_ref, p_ref, al_ref, *, k_sel, seq):
    i = pl.program_id(1)
    blk = BLOCK
    lg_refs = (lg0_ref, lg1_ref)

    @pl.when(i == 0)
    def _():
        ik = ik_ref[0]
        lane = lax.broadcasted_iota(I32, ik.shape, 1)
        zero = jnp.zeros_like(ik)
        ika_ref[...] = jnp.where(lane < IDX_DIM, ik, zero)
        ikb_ref[...] = jnp.where(lane >= IDX_DIM, ik, zero)
        for c in range(seq // blk):
            vat_ref[c] = av_ref[0, c * blk:(c + 1) * blk, :].astype(F32).T.astype(BF16)

    iqt = iq_ref[0].astype(F32).T.astype(BF16)
    wt = iw_ref[0].astype(F32).T * (IDX_HEADS * IDX_DIM) ** -0.5

    srow = lax.broadcasted_iota(I32, (blk, blk), 0)
    tcol = lax.broadcasted_iota(I32, (blk, blk), 1)

    def score_body(c, carry):
        s0 = pl.multiple_of(c * blk, blk)
        ka = ika_ref[pl.ds(s0, blk), :]
        kb = ikb_ref[pl.ds(s0, blk), :]
        sc = jnp.zeros((blk, blk), F32)
        for j in range(IDX_HEADS // 2):
            rhs = iqt[j * LANES:(j + 1) * LANES, :]
            sc = sc + wt[2 * j:2 * j + 1, :] * jnp.maximum(jnp.dot(ka, rhs, preferred_element_type=F32), 0.0)
            sc = sc + wt[2 * j + 1:2 * j + 2, :] * jnp.maximum(jnp.dot(kb, rhs, preferred_element_type=F32), 0.0)
        bits = pltpu.bitcast(sc, I32)
        key = bits ^ ((bits >> 31) & jnp.int32(0x7FFFFFFF))
        causal = srow <= tcol + jnp.where(c < i, jnp.int32(blk), jnp.int32(0))
        key = jnp.where(causal, key, jnp.int32(INT_MIN))
        skey_ref[pl.ds(s0, blk), :] = key
        hi_ref[pl.ds(s0, blk), :] = (key >> 16).astype(I16)
        lo_ref[pl.ds(s0, blk), :] = ((key & jnp.int32(0xFFFF)) - HALF_RANGE).astype(I16)
        return carry

    lax.fori_loop(0, i + 1, score_body, 0)

    n_acc = 4
    rows16 = 2 * SUBLANES

    def count_at_least(ref, trial):
        trial16 = jnp.broadcast_to(trial.astype(I16), (rows16, blk))

        def cnt_body(c, accs):
            vals = ref[pl.ds(pl.multiple_of(c * blk, blk), blk), :]
            accs = list(accs)
            for r in range(blk // rows16):
                rows = vals[r * rows16:(r + 1) * rows16, :]
                accs[r % n_acc] = jnp.where(rows >= trial16, accs[r % n_acc] + jnp.int16(1), accs[r % n_acc])
            return tuple(accs)

        zero = jnp.zeros((rows16, blk), I16)
        accs = lax.fori_loop(0, i + 1, cnt_body, (zero,) * n_acc)
        total = (accs[0] + accs[1]) + (accs[2] + accs[3])
        return jnp.sum(total.astype(I32), axis=0, keepdims=True)

    def largest_with_count(ref, need):
        def bit_body(it, u):
            trial_u = u | jnp.left_shift(jnp.int32(1), jnp.int32(15) - it)
            return jnp.where(count_at_least(ref, trial_u - HALF_RANGE) >= need, trial_u, u)

        return lax.fori_loop(0, 16, bit_body, jnp.zeros((1, blk), I32))

    hi_thr = largest_with_count(hi_ref, k_sel) - HALF_RANGE
    above = jnp.where(hi_thr == HALF_RANGE - 1, 0, count_at_least(hi_ref, jnp.minimum(hi_thr + 1, HALF_RANGE - 1)))
    hi_thr16 = jnp.broadcast_to(hi_thr.astype(I16), (blk, blk))

    def keep_candidates(c, carry):
        rows = pl.ds(pl.multiple_of(c * blk, blk), blk)
        lo_ref[rows, :] = jnp.where(hi_ref[rows, :] == hi_thr16, lo_ref[rows, :], jnp.int16(-HALF_RANGE))
        return carry

    lax.fori_loop(0, i + 1, keep_candidates, 0)
    lo_thr = largest_with_count(lo_ref, k_sel - above)
    thr = hi_thr * (2 * HALF_RANGE) + lo_thr

    qt = (aq_ref[0].astype(F32) * (HEAD_DIM ** -0.5 * LOG2E)).T.astype(BF16)

    m_ref[...] = jnp.full_like(m_ref, -jnp.inf)
    l_ref[...] = jnp.zeros_like(l_ref)
    acc_ref[...] = jnp.zeros_like(acc_ref)

    def logits(c):
        ka = ak_ref[0, pl.ds(pl.multiple_of(c * blk, blk), blk), :]
        return [jnp.dot(ka, qt[h * LANES:(h + 1) * LANES, :], preferred_element_type=F32) for h in range(A_HEADS)]

    def accumulate(c, slot):
        vt = vat_ref[c]
        for h in range(A_HEADS):
            pv = jnp.dot(vt, p_ref[slot, h], preferred_element_type=F32)
            acc_ref[h] = acc_ref[h] * al_ref[slot, h] + pv

    p_ref[1] = jnp.zeros_like(p_ref[1])
    al_ref[1] = jnp.ones_like(al_ref[1])
    first = logits(0)
    for h in range(A_HEADS):
        lg0_ref[h] = first[h]

    def stage(c, slot, has_next=True):
        if has_next:
            nxt = logits(jnp.minimum(c + 1, i))
            for h in range(A_HEADS):
                lg_refs[1 - slot][h] = nxt[h]
        accumulate(jnp.maximum(c - 1, 0), 1 - slot)
        kind = jnp.clip(c - i + 2, 0, 2)
        sel = skey_ref[pl.ds(pl.multiple_of(c * blk, blk), blk), :] >= thr
        for h in range(A_HEADS):
            lg = jnp.where(sel, lg_refs[slot][h] + bias_ref[kind, h], -jnp.inf)
            m_old = m_ref[h]
            m_new = jnp.maximum(m_old, jnp.max(lg, axis=0, keepdims=True))
            m_use = jnp.where(m_new == -jnp.inf, 0.0, m_new)
            alpha = jnp.exp2(m_old - m_use)
            p = jnp.exp2(lg - m_use)
            l_ref[h] = l_ref[h] * alpha + jnp.sum(p, axis=0, keepdims=True)
            m_ref[h] = m_new
            p_ref[slot, h] = p.astype(BF16)
            al_ref[slot, h] = alpha

    def pair_body(j, carry):
        stage(2 * j, 0)
        stage(2 * j + 1, 1)
        return carry

    n_chunks = i + 1
    lax.fori_loop(0, n_chunks // 2, pair_body, 0)

    @pl.when(n_chunks % 2 == 1)
    def _():
        stage(i, 0, has_next=False)

    accumulate(i, i % 2)

    for h in range(A_HEADS):
        o_ref[0, :, h * LANES:(h + 1) * LANES] = (acc_ref[h] / l_ref[h]).T.astype(BF16)


def _dsa(proj3, bias_tab, k_sel):
    bsz, seq, _ = proj3.shape
    blk = BLOCK
    return pl.pallas_call(
        functools.partial(_dsa_kernel, k_sel=k_sel, seq=seq),
        grid=(bsz, seq // blk),
        in_specs=[
            pl.BlockSpec((1, blk, A_WIDTH), lambda b, i: (b, i, 0)),
            pl.BlockSpec((1, blk, IDX_HEADS * IDX_DIM), lambda b, i: (b, i, 1)),
            pl.BlockSpec((1, blk, LANES), lambda b, i: (b, i, BLK_IW)),
            pl.BlockSpec((1, seq, LANES), lambda b, i: (b, 0, BLK_IK)),
            pl.BlockSpec((1, seq, LANES), lambda b, i: (b, 0, BLK_AK)),
            pl.BlockSpec((1, seq, LANES), lambda b, i: (b, 0, BLK_AV)),
            pl.BlockSpec(bias_tab.shape, lambda b, i: (0, 0, 0, 0)),
        ],
        out_specs=pl.BlockSpec((1, blk, A_WIDTH), lambda b, i: (b, i, 0)),
        out_shape=jax.ShapeDtypeStruct((bsz, seq, A_WIDTH), BF16),
        scratch_shapes=[
            pltpu.VMEM((seq, blk), I32),
            pltpu.VMEM((seq, blk), I16),
            pltpu.VMEM((seq, blk), I16),
            pltpu.VMEM((seq, LANES), BF16),
            pltpu.VMEM((seq, LANES), BF16),
            pltpu.VMEM((seq // blk, HEAD_DIM, blk), BF16),
            pltpu.VMEM((A_HEADS, 1, blk), F32),
            pltpu.VMEM((A_HEADS, 1, blk), F32),
            pltpu.VMEM((A_HEADS, HEAD_DIM, blk), F32),
            pltpu.VMEM((A_HEADS, blk, blk), F32),
            pltpu.VMEM((A_HEADS, blk, blk), F32),
            pltpu.VMEM((2, A_HEADS, blk, blk), BF16),
            pltpu.VMEM((2, A_HEADS, 1, blk), F32),
        ],
        compiler_params=_params(("parallel", "arbitrary")),
        name="dsa",
    )(proj3, proj3, proj3, proj3, proj3, proj3, bias_tab)


def _rms_rows(x, g):
    return x * lax.rsqrt(jnp.mean(x * x, axis=-1, keepdims=True) + RMS_EPS) * g


def _mla_proj_kernel(cq_ref, ckv_ref, kra_ref, krb_ref, cc_ref, ss_ref, qn_ref, kvn_ref, wq_ref, wkv_ref,
                     qt_ref, k_ref, vt_ref):
    tm = cq_ref.shape[0]
    cc = cc_ref[...]
    ss = ss_ref[...]
    qscale = (NOPE_DIM + ROPE_DIM) ** -0.5 * LOG2E
    cq = _rms_rows(cq_ref[...].astype(F32), qn_ref[0]).astype(BF16)
    q3 = jnp.dot(cq, wq_ref[0], preferred_element_type=F32)
    for h in range(B_HEADS):
        base = 3 * LANES * h
        rot = q3[:, base + LANES:base + 2 * LANES] * cc + q3[:, base + 2 * LANES:base + 3 * LANES] * ss
        qt_ref[h, 0:LANES, :] = (q3[:, base:base + LANES] * qscale).T.astype(BF16)
        qt_ref[h, LANES:MLA_QK, :] = (rot * qscale).T.astype(BF16)
    ckv = _rms_rows(ckv_ref[...].astype(F32), kvn_ref[0]).astype(BF16)
    kv = jnp.dot(ckv, wkv_ref[0], preferred_element_type=F32)
    krot = (kra_ref[...].astype(F32) * cc + krb_ref[...].astype(F32) * ss).astype(BF16)
    for h in range(B_HEADS):
        k_ref[:, MLA_QK * h:MLA_QK * h + LANES] = kv[:, h * LANES:(h + 1) * LANES].astype(BF16)
        k_ref[:, MLA_QK * h + LANES:MLA_QK * (h + 1)] = krot
        v_h = kv[:, B_HEADS * NOPE_DIM + h * V_DIM:B_HEADS * NOPE_DIM + (h + 1) * V_DIM]
        for c in range(tm // BLOCK):
            vt_ref[h, c] = v_h[c * BLOCK:(c + 1) * BLOCK, :].T.astype(BF16)


def _mla_proj(proj, cc, ss, qn, kvn, wq, wkv, layer, tm=512):
    n = proj.shape[0]
    per_layer = lambda i: (layer, 0, 0)
    return pl.pallas_call(
        _mla_proj_kernel,
        grid=(n // tm,),
        in_specs=[
            pl.BlockSpec((tm, Q_LORA), lambda i: (i, BLK512_CQ)),
            pl.BlockSpec((tm, KV_LORA), lambda i: (i, BLK512_CKV)),
            pl.BlockSpec((tm, LANES), lambda i: (i, BLK_KRA)),
            pl.BlockSpec((tm, LANES), lambda i: (i, BLK_KRB)),
            pl.BlockSpec((tm, LANES), lambda i: (i, 0)),
            pl.BlockSpec((tm, LANES), lambda i: (i, 0)),
            pl.BlockSpec((1, 1, Q_LORA), per_layer),
            pl.BlockSpec((1, 1, KV_LORA), per_layer),
            pl.BlockSpec((1,) + wq.shape[1:], per_layer),
            pl.BlockSpec((1,) + wkv.shape[1:], per_layer),
        ],
        out_specs=[
            pl.BlockSpec((B_HEADS, MLA_QK, tm), lambda i: (0, 0, i)),
            pl.BlockSpec((tm, B_HEADS * MLA_QK), lambda i: (i, 0)),
            pl.BlockSpec((B_HEADS, tm // BLOCK, V_DIM, BLOCK), lambda i: (0, i, 0, 0)),
        ],
        out_shape=[jax.ShapeDtypeStruct((B_HEADS, MLA_QK, n), BF16),
                   jax.ShapeDtypeStruct((n, B_HEADS * MLA_QK), BF16),
                   jax.ShapeDtypeStruct((B_HEADS, n // BLOCK, V_DIM, BLOCK), BF16)],
        compiler_params=_params(("parallel",)),
        name="mla_proj",
    )(proj, proj, proj, proj, cc, ss, qn, kvn, wq, wkv)


def _mla_attn_kernel(qt_ref, k_ref, vt_ref, o_ref, lg0_ref, lg1_ref, p_ref, al_ref, m_ref, l_ref, acc_ref):
    lg_refs = (lg0_ref, lg1_ref)
    qi = pl.program_id(1)
    blk = BLOCK
    srow = lax.broadcasted_iota(I32, (blk, blk), 0)
    tcol = lax.broadcasted_iota(I32, (blk, blk), 1)
    causal_bias = jnp.where(srow <= tcol, 0.0, -jnp.inf).astype(F32)

    def logits(c):
        s0 = pl.multiple_of(c * blk, blk)
        return [jnp.dot(k_ref[0, pl.ds(s0, blk), h * MLA_QK:(h + 1) * MLA_QK], qt_ref[h],
                        preferred_element_type=F32) for h in range(B_HEADS)]

    def accumulate(c, slot):
        for h in range(B_HEADS):
            pv = jnp.dot(vt_ref[h, c], p_ref[slot, h], preferred_element_type=F32)
            acc_ref[h] = acc_ref[h] * al_ref[slot, h] + pv

    m_ref[...] = jnp.full_like(m_ref, -jnp.inf)
    l_ref[...] = jnp.zeros_like(l_ref)
    acc_ref[...] = jnp.zeros_like(acc_ref)
    p_ref[1] = jnp.zeros_like(p_ref[1])
    al_ref[1] = jnp.ones_like(al_ref[1])
    first = logits(0)
    for h in range(B_HEADS):
        lg0_ref[h] = first[h]
    def stage(c, slot, has_next=True):
        if has_next:
            nxt = logits(jnp.minimum(c + 1, qi))
            for h in range(B_HEADS):
                lg_refs[1 - slot][h] = nxt[h]
        accumulate(jnp.maximum(c - 1, 0), 1 - slot)
        mask = jnp.where(c == qi, causal_bias, 0.0)
        for h in range(B_HEADS):
            lg = lg_refs[slot][h] + mask
            m_old = m_ref[h]
            m_new = jnp.maximum(m_old, jnp.max(lg, axis=0, keepdims=True))
            alpha = jnp.exp2(m_old - m_new)
            p = jnp.exp2(lg - m_new)
            l_ref[h] = l_ref[h] * alpha + jnp.sum(p, axis=0, keepdims=True)
            m_ref[h] = m_new
            p_ref[slot, h] = p.astype(BF16)
            al_ref[slot, h] = alpha

    def pair_body(j, carry):
        stage(2 * j, 0)
        stage(2 * j + 1, 1)
        return carry

    n_chunks = qi + 1
    lax.fori_loop(0, n_chunks // 2, pair_body, 0)

    @pl.when(n_chunks % 2 == 1)
    def _():
        stage(qi, 0, has_next=False)

    accumulate(qi, qi % 2)
    for h in range(B_HEADS):
        o_ref[0, :, h * V_DIM:(h + 1) * V_DIM] = (acc_ref[h] / l_ref[h]).T.astype(BF16)


def _mla_attn(qt, k3, vt):
    bsz, seq, _ = k3.shape
    blk = BLOCK
    nq = seq // blk
    return pl.pallas_call(
        _mla_attn_kernel,
        grid=(bsz, nq),
        in_specs=[
            pl.BlockSpec((B_HEADS, MLA_QK, blk), lambda b, i: (0, 0, b * nq + i)),
            pl.BlockSpec((1, seq, B_HEADS * MLA_QK), lambda b, i: (b, 0, 0)),
            pl.BlockSpec((B_HEADS, nq, V_DIM, blk), lambda b, i: (0, b, 0, 0)),
        ],
        out_specs=pl.BlockSpec((1, blk, B_WIDTH), lambda b, i: (b, i, 0)),
        out_shape=jax.ShapeDtypeStruct((bsz, seq, B_WIDTH), BF16),
        scratch_shapes=[
            pltpu.VMEM((B_HEADS, blk, blk), F32),
            pltpu.VMEM((B_HEADS, blk, blk), F32),
            pltpu.VMEM((2, B_HEADS, blk, blk), BF16),
            pltpu.VMEM((2, B_HEADS, 1, blk), F32),
            pltpu.VMEM((B_HEADS, 1, blk), F32),
            pltpu.VMEM((B_HEADS, 1, blk), F32),
            pltpu.VMEM((B_HEADS, V_DIM, blk), F32),
        ],
        compiler_params=_params(("parallel", "arbitrary")),
        name="mla_attn",
    )(qt, k3, vt)


def _mem_attn_kernel(q_ref, kv_ref, o_ref):
    scale = HEAD_DIM ** -0.5
    for h in range(C_HEADS):
        q = q_ref[0, :, h * LANES:(h + 1) * LANES]
        k = kv_ref[0, :, h * LANES:(h + 1) * LANES]
        v = kv_ref[0, :, C_WIDTH + h * LANES:C_WIDTH + (h + 1) * LANES]
        s = lax.dot_general(q, k, (((1,), (1,)), ((), ())), preferred_element_type=F32) * scale
        p = jnp.exp(s - jnp.max(s, axis=-1, keepdims=True))
        l = jnp.sum(p, axis=-1, keepdims=True)
        o = jnp.dot(p.astype(BF16), v, preferred_element_type=F32) / l
        o_ref[0, :, h * LANES:(h + 1) * LANES] = o.astype(BF16)


def _mem_attn(proj3, mkv3, tq=512):
    bsz, seq, _ = proj3.shape
    mlen = mkv3.shape[1]
    return pl.pallas_call(
        _mem_attn_kernel,
        grid=(bsz, seq // tq),
        in_specs=[pl.BlockSpec((1, tq, C_WIDTH), lambda b, i: (b, i, BLK512_C)),
                  pl.BlockSpec((1, mlen, 2 * C_WIDTH), lambda b, i: (b, 0, 0))],
        out_specs=pl.BlockSpec((1, tq, C_WIDTH), lambda b, i: (b, i, 0)),
        out_shape=jax.ShapeDtypeStruct((bsz, seq, C_WIDTH), BF16),
        compiler_params=_params(("parallel", "arbitrary")),
        name="mem_attn",
    )(proj3, mkv3)


def _mix_out_kernel(oa_ref, ob_ref, oc_ref, ga_ref, gb_ref, gc_ref, x_ref, wb_ref, wo_ref, g_ref, b_ref,
                    o_ref, *, alpha):
    def gated(o_blk, gate_ref, r0, r1):
        y = jnp.dot(o_blk, wb_ref[0, r0:r1, :], preferred_element_type=F32)
        return y * (1.0 / (1.0 + jnp.exp(-gate_ref[...].astype(F32))))

    y = gated(oa_ref[...], ga_ref, 0, A_WIDTH)
    y = y + gated(ob_ref[...], gb_ref, A_WIDTH, A_WIDTH + B_WIDTH)
    y = y + gated(oc_ref[...], gc_ref, A_WIDTH + B_WIDTH, A_WIDTH + B_WIDTH + C_WIDTH)
    mix = jnp.dot(y.astype(BF16), wo_ref[0], preferred_element_type=F32)
    o_ref[...] = _layer_norm_rows(alpha * x_ref[...] + mix, g_ref[0], b_ref[0])


def _mix_out(oa, ob, oc, proj, x, wb, wo, ln_g, ln_b, layer, alpha, tm=256):
    n = x.shape[0]
    per_layer = lambda i: (layer, 0, 0)
    ln_row = lambda i: (3 * layer + 1, 0, 0)
    return pl.pallas_call(
        functools.partial(_mix_out_kernel, alpha=alpha),
        grid=(n // tm,),
        in_specs=[
            pl.BlockSpec((tm, A_WIDTH), lambda i: (i, 0)),
            pl.BlockSpec((tm, B_WIDTH), lambda i: (i, 0)),
            pl.BlockSpec((tm, C_WIDTH), lambda i: (i, 0)),
            pl.BlockSpec((tm, D_MODEL), lambda i: (i, BLK2048_GATE0)),
            pl.BlockSpec((tm, D_MODEL), lambda i: (i, BLK2048_GATE0 + 1)),
            pl.BlockSpec((tm, D_MODEL), lambda i: (i, BLK2048_GATE0 + 2)),
            pl.BlockSpec((tm, D_MODEL), lambda i: (i, 0)),
            pl.BlockSpec((1,) + wb.shape[1:], per_layer),
            pl.BlockSpec((1,) + wo.shape[1:], per_layer),
            pl.BlockSpec((1, 1, D_MODEL), ln_row),
            pl.BlockSpec((1, 1, D_MODEL), ln_row),
        ],
        out_specs=pl.BlockSpec((tm, D_MODEL), lambda i: (i, 0)),
        out_shape=jax.ShapeDtypeStruct((n, D_MODEL), F32),
        compiler_params=_params(("parallel",)),
        name="mix_out",
    )(oa, ob, oc, proj, proj, proj, x, wb, wo, ln_g, ln_b)


_IN_OFFS = tuple(sum(IN_SIZES[:k]) for k in range(len(IN_SIZES) + 1))


def _pack_w_in_t(w):
    wt = jnp.swapaxes(w, 1, 2)
    o = _IN_OFFS
    seg = lambda k: wt[:, o[k]:o[k + 1]]
    zeros = lambda r: jnp.zeros((w.shape[0], r, w.shape[1]), w.dtype)
    ik, iw, kr = seg(4), seg(5), seg(8)
    half = ROPE_DIM // 2
    packed = jnp.concatenate([
        wt[:, :o[4]],
        ik, ik,
        iw, zeros(LANES - IDX_HEADS),
        kr, zeros(LANES - ROPE_DIM),
        kr[:, half:], kr[:, :half], zeros(LANES - ROPE_DIM),
        seg(6), seg(7), seg(9), seg(10),
    ], axis=1)
    assert packed.shape[1] == PROJ_COLS
    return packed.astype(BF16)


def _pack_w_uq(w):
    half = ROPE_DIM // 2
    zeros = jnp.zeros(w.shape[:-1] + (LANES - ROPE_DIM,), w.dtype)
    cols = []
    for h in range(B_HEADS):
        base = h * (NOPE_DIM + ROPE_DIM)
        r = w[..., base + NOPE_DIM:base + NOPE_DIM + ROPE_DIM]
        cols += [w[..., base:base + NOPE_DIM], r, zeros, r[..., half:], r[..., :half], zeros]
    return jnp.concatenate(cols, axis=-1).astype(BF16)


def _pack_w_ukv(w):
    step = NOPE_DIM + V_DIM
    ks = [w[..., h * step:h * step + NOPE_DIM] for h in range(B_HEADS)]
    vs = [w[..., h * step + NOPE_DIM:(h + 1) * step] for h in range(B_HEADS)]
    return jnp.concatenate(ks + vs, axis=-1).astype(BF16)


def _rel_bucket(dist):
    n = jnp.maximum(dist, 0)
    max_exact = REL_BUCKETS // 2
    nf = jnp.maximum(n, 1).astype(F32)
    large = max_exact + (jnp.log(nf / max_exact) / math.log(REL_MAX_DIST / max_exact)
                         * (REL_BUCKETS - max_exact)).astype(I32)
    large = jnp.minimum(large, REL_BUCKETS - 1)
    return jnp.where(n < max_exact, n, large)


def _bias_tables(rel_bias):
    assert BLOCK + 1 >= REL_MAX_DIST
    s = jnp.arange(BLOCK)[:, None]
    t = jnp.arange(BLOCK)[None, :]
    tiles = [jnp.zeros((A_HEADS, BLOCK, BLOCK), F32)]
    for block_gap in (1, 0):
        dist = t - s + block_gap * BLOCK
        onehot = (_rel_bucket(dist)[:, :, None] == jnp.arange(REL_BUCKETS)).astype(F32)
        looked_up = jnp.einsum("stb,bh->sth", onehot, rel_bias, precision=lax.Precision.HIGHEST)
        tile = (looked_up - rel_bias[REL_BUCKETS - 1]) * LOG2E
        tile = jnp.where((dist >= 0)[:, :, None], tile, -jnp.inf)
        tiles.append(jnp.transpose(tile, (2, 0, 1)))
    return jnp.stack(tiles).astype(F32)


def kernel(x, mem, positions, rel_bias, ln_g, ln_b, ffn1_up, ffn1_down, w_in, q_norm, kv_norm, w_uq, w_ukv,
           w_mem_kv, w_branch, w_out, ffn2_up, ffn2_down):
    bsz, seq, d = x.shape
    depth = ffn1_up.shape[0]
    n = bsz * seq
    alpha = (2 * depth) ** 0.25
    k_sel = min(TOPK_MAX, seq // 4)
    assert d == D_MODEL and seq % (2 * BLOCK) == 0

    inv_freq = ROPE_THETA ** (-jnp.arange(0, ROPE_DIM, 2, dtype=F32) / ROPE_DIM)
    ang = positions.astype(F32)[..., None] * inv_freq
    cos, sin = jnp.cos(ang).reshape(n, -1), jnp.sin(ang).reshape(n, -1)
    pad = jnp.zeros((n, LANES - ROPE_DIM), F32)
    cc = jnp.concatenate([cos, cos, pad], axis=1)
    ss = jnp.concatenate([-sin, sin, pad], axis=1)
    bias_tab = _bias_tables(rel_bias)

    xf = x.reshape(n, d)
    memf = mem.reshape(bsz * mem.shape[1], d)
    ln_g3 = ln_g.reshape(depth * 3, 1, d)
    ln_b3 = ln_b.reshape(depth * 3, 1, d)
    qn3 = q_norm.reshape(depth, 1, Q_LORA)
    kvn3 = kv_norm.reshape(depth, 1, KV_LORA)
    w_in_p = _pack_w_in_t(w_in)
    w_uq_p = _pack_w_uq(w_uq.astype(BF16))
    w_ukv_p = _pack_w_ukv(w_ukv.astype(BF16))
    w_branch_b = w_branch.astype(BF16)
    w_out_b = w_out.astype(BF16)

    for l in range(depth):
        xf = _ffn(xf, ffn1_up, ffn1_down, ln_g3, ln_b3, l, 0, alpha)
        proj = _proj(xf, w_in_p, l, 1024, 1024, "in_proj", w_is_transposed=True)
        proj3 = proj.reshape(bsz, seq, PROJ_COLS)
        o_a = _dsa(proj3, bias_tab, k_sel).reshape(n, A_WIDTH)
        qt, k, vt = _mla_proj(proj, cc, ss, qn3, kvn3, w_uq_p, w_ukv_p, l)
        o_b = _mla_attn(qt, k.reshape(bsz, seq, -1), vt).reshape(n, B_WIDTH)
        mkv = _proj(memf, w_mem_kv, l, 512, 1024, "mem_proj")
        o_c = _mem_attn(proj3, mkv.reshape(bsz, mem.shape[1], 2 * C_WIDTH)).reshape(n, C_WIDTH)
        xf = _mix_out(o_a, o_b, o_c, proj, xf, w_branch_b, w_out_b, ln_g3, ln_b3, l, alpha)
        xf = _ffn(xf, ffn2_up, ffn2_down, ln_g3, ln_b3, l, 2, alpha)
    return xf.reshape(bsz, seq, d)
```

```python
import functools
import math

import jax
import jax.numpy as jnp
from jax import lax
from jax.experimental import pallas as pl
from jax.experimental.pallas import tpu as pltpu

F32 = jnp.float32
BF16 = jnp.bfloat16
I32 = jnp.int32
I16 = jnp.int16

D_MODEL = 2048
D_FF = 5632
HEAD_DIM = 128
A_HEADS = 6
IDX_HEADS = 16
IDX_DIM = 64
TOPK_MAX = 256
B_HEADS = 6
Q_LORA = 512
KV_LORA = 512
NOPE_DIM = 128
ROPE_DIM = 64
V_DIM = 128
ROPE_THETA = 10000.0
C_HEADS = 4
REL_BUCKETS = 32
REL_MAX_DIST = 128
LN_EPS = 1e-5
RMS_EPS = 1e-6
A_WIDTH = A_HEADS * HEAD_DIM
B_WIDTH = B_HEADS * V_DIM
C_WIDTH = C_HEADS * HEAD_DIM
IN_SIZES = (A_WIDTH, HEAD_DIM, HEAD_DIM, IDX_HEADS * IDX_DIM, IDX_DIM, IDX_HEADS,
            Q_LORA, KV_LORA, ROPE_DIM, C_WIDTH, 3 * D_MODEL)

LANES = 128
SUBLANES = 8
BLOCK = 256
MLA_QK = 2 * LANES
INT_MIN = -2 ** 31
HALF_RANGE = 2 ** 15
LOG2E = 1.4426950408889634

PROJ_COLS = 10240
BLK_AK, BLK_AV = 6, 7
BLK_IK, BLK_IW, BLK_KRA, BLK_KRB = 16, 17, 18, 19
BLK512_CQ, BLK512_CKV, BLK512_C = 5, 6, 7
BLK2048_GATE0 = 2

VMEM_LIMIT = 56 * 1024 * 1024
FFN_VMEM_LIMIT = 60 * 1024 * 1024


def _params(sem, vmem=VMEM_LIMIT):
    return pltpu.CompilerParams(dimension_semantics=sem, vmem_limit_bytes=vmem)


def _layer_norm_rows(y, g, b):
    mu = jnp.mean(y, axis=-1, keepdims=True)
    d = y - mu
    var = jnp.mean(d * d, axis=-1, keepdims=True)
    return d * lax.rsqrt(var + LN_EPS) * g + b


def _ffn_kernel(x_ref, wg_ref, wu_ref, wd_ref, g_ref, b_ref, o_ref, xb_ref, *, alpha):
    j = pl.program_id(1)

    @pl.when(j == 0)
    def _():
        x = x_ref[...]
        xb_ref[...] = x.astype(BF16)
        o_ref[...] = (2.0 * alpha) * x

    xb = xb_ref[...]
    gate = jnp.dot(xb, wg_ref[0].astype(BF16), preferred_element_type=F32)
    up = jnp.dot(xb, wu_ref[0].astype(BF16), preferred_element_type=F32)
    h = (gate * (1.0 / (1.0 + jnp.exp(-gate))) * up).astype(BF16)
    o_ref[...] += jnp.dot(h, wd_ref[0].astype(BF16), preferred_element_type=F32)

    @pl.when(j == pl.num_programs(1) - 1)
    def _():
        o_ref[...] = _layer_norm_rows(0.5 * o_ref[...], g_ref[0], b_ref[0])


def _ffn(x, w_up, w_down, ln_g, ln_b, layer, which, alpha, tm=1024, tf=256):
    n = x.shape[0]
    tm = min(tm, n)
    nf = D_FF // tf
    return pl.pallas_call(
        functools.partial(_ffn_kernel, alpha=alpha),
        grid=(n // tm, nf),
        in_specs=[
            pl.BlockSpec((tm, D_MODEL), lambda i, j: (i, 0)),
            pl.BlockSpec((1, D_MODEL, tf), lambda i, j: (layer, 0, j)),
            pl.BlockSpec((1, D_MODEL, tf), lambda i, j: (layer, 0, j + nf)),
            pl.BlockSpec((1, tf, D_MODEL), lambda i, j: (layer, j, 0)),
            pl.BlockSpec((1, 1, D_MODEL), lambda i, j: (3 * layer + which, 0, 0)),
            pl.BlockSpec((1, 1, D_MODEL), lambda i, j: (3 * layer + which, 0, 0)),
        ],
        out_specs=pl.BlockSpec((tm, D_MODEL), lambda i, j: (i, 0)),
        out_shape=jax.ShapeDtypeStruct((n, D_MODEL), F32),
        scratch_shapes=[pltpu.VMEM((tm, D_MODEL), BF16)],
        compiler_params=_params(("parallel", "arbitrary"), FFN_VMEM_LIMIT),
        name="ffn",
    )(x, w_up, w_up, w_down, ln_g, ln_b)


def _proj_kernel(x_ref, w_ref, o_ref, xb_ref, *, w_is_transposed):
    @pl.when(pl.program_id(1) == 0)
    def _():
        xb_ref[...] = x_ref[...].astype(BF16)

    contract_w = 1 if w_is_transposed else 0
    o_ref[...] = lax.dot_general(xb_ref[...], w_ref[0].astype(BF16), (((1,), (contract_w,)), ((), ())),
                                 preferred_element_type=F32).astype(o_ref.dtype)


def _proj(x, w, layer, tm, tn, name, w_is_transposed=False):
    n, k = x.shape
    tm = min(tm, n)
    cols = w.shape[1] if w_is_transposed else w.shape[2]
    w_spec = (pl.BlockSpec((1, tn, k), lambda i, j: (layer, j, 0)) if w_is_transposed
              else pl.BlockSpec((1, k, tn), lambda i, j: (layer, 0, j)))
    return pl.pallas_call(
        functools.partial(_proj_kernel, w_is_transposed=w_is_transposed),
        grid=(n // tm, cols // tn),
        in_specs=[pl.BlockSpec((tm, k), lambda i, j: (i, 0)), w_spec],
        out_specs=pl.BlockSpec((tm, tn), lambda i, j: (i, j)),
        out_shape=jax.ShapeDtypeStruct((n, cols), BF16),
        scratch_shapes=[pltpu.VMEM((tm, k), BF16)],
        compiler_params=_params(("parallel", "arbitrary")),
        name=name,
    )(x, w)


def _dsa_kernel(aq_ref, iq_ref, iw_ref, ik_ref, ak_ref, av_ref, bias_ref, o_ref,
                skey_ref, hi_ref, lo_ref, ika_ref, ikb_ref, vat_ref, m_ref, l_ref, acc_ref,
                lg0_ref, lg1_ref, p_ref, al_ref, *, k_sel, seq):
    i = pl.program_id(1)
    blk = BLOCK
    lg_refs = (lg0_ref, lg1_ref)

    @pl.when(i == 0)
    def _():
        ik = ik_ref[0]
        lane = lax.broadcasted_iota(I32, ik.shape, 1)
        zero = jnp.zeros_like(ik)
        ika_ref[...] = jnp.where(lane < IDX_DIM, ik, zero)
        ikb_ref[...] = jnp.where(lane >= IDX_DIM, ik, zero)
        for c in range(seq // blk):
            vat_ref[c] = av_ref[0, c * blk:(c + 1) * blk, :].astype(F32).T.astype(BF16)

    iqt = iq_ref[0].astype(F32).T.astype(BF16)
    wt = iw_ref[0].astype(F32).T * (IDX_HEADS * IDX_DIM) ** -0.5

    srow = lax.broadcasted_iota(I32, (blk, blk), 0)
    tcol = lax.broadcasted_iota(I32, (blk, blk), 1)

    def score_body(c, carry):
        s0 = pl.multiple_of(c * blk, blk)
        ka = ika_ref[pl.ds(s0, blk), :]
        kb = ikb_ref[pl.ds(s0, blk), :]
        sc = jnp.zeros((blk, blk), F32)
        for j in range(IDX_HEADS // 2):
            rhs = iqt[j * LANES:(j + 1) * LANES, :]
            sc = sc + wt[2 * j:2 * j + 1, :] * jnp.maximum(jnp.dot(ka, rhs, preferred_element_type=F32), 0.0)
            sc = sc + wt[2 * j + 1:2 * j + 2, :] * jnp.maximum(jnp.dot(kb, rhs, preferred_element_type=F32), 0.0)
        bits = pltpu.bitcast(sc, I32)
        key = bits ^ ((bits >> 31) & jnp.int32(0x7FFFFFFF))
        causal = srow <= tcol + jnp.where(c < i, jnp.int32(blk), jnp.int32(0))
        key = jnp.where(causal, key, jnp.int32(INT_MIN))
        skey_ref[pl.ds(s0, blk), :] = key
        hi_ref[pl.ds(s0, blk), :] = (key >> 16).astype(I16)
        lo_ref[pl.ds(s0, blk), :] = ((key & jnp.int32(0xFFFF)) - HALF_RANGE).astype(I16)
        return carry

    lax.fori_loop(0, i + 1, score_body, 0)

    n_acc = 4
    rows16 = 2 * SUBLANES

    def count_at_least(ref, trial):
        trial16 = jnp.broadcast_to(trial.astype(I16), (rows16, blk))

        def cnt_body(c, accs):
            vals = ref[pl.ds(pl.multiple_of(c * blk, blk), blk), :]
            accs = list(accs)
            for r in range(blk // rows16):
                rows = vals[r * rows16:(r + 1) * rows16, :]
                accs[r % n_acc] = jnp.where(rows >= trial16, accs[r % n_acc] + jnp.int16(1), accs[r % n_acc])
            return tuple(accs)

        zero = jnp.zeros((rows16, blk), I16)
        accs = lax.fori_loop(0, i + 1, cnt_body, (zero,) * n_acc)
        total = (accs[0] + accs[1]) + (accs[2] + accs[3])
        return jnp.sum(total.astype(I32), axis=0, keepdims=True)

    def largest_with_count(ref, need):
        def bit_body(it, u):
            trial_u = u | jnp.left_shift(jnp.int32(1), jnp.int32(15) - it)
            return jnp.where(count_at_least(ref, trial_u - HALF_RANGE) >= need, trial_u, u)

        return lax.fori_loop(0, 16, bit_body, jnp.zeros((1, blk), I32))

    hi_thr = largest_with_count(hi_ref, k_sel) - HALF_RANGE
    above = jnp.where(hi_thr == HALF_RANGE - 1, 0, count_at_least(hi_ref, jnp.minimum(hi_thr + 1, HALF_RANGE - 1)))
    hi_thr16 = jnp.broadcast_to(hi_thr.astype(I16), (blk, blk))

    def keep_candidates(c, carry):
        rows = pl.ds(pl.multiple_of(c * blk, blk), blk)
        lo_ref[rows, :] = jnp.where(hi_ref[rows, :] == hi_thr16, lo_ref[rows, :], jnp.int16(-HALF_RANGE))
        return carry

    lax.fori_loop(0, i + 1, keep_candidates, 0)
    lo_thr = largest_with_count(lo_ref, k_sel - above)
    thr = hi_thr * (2 * HALF_RANGE) + lo_thr

    def count_keys_at_least(trial):
        trial8 = jnp.broadcast_to(trial, (SUBLANES, blk))

        def cnt_body(c, acc):
            keys = skey_ref[pl.ds(pl.multiple_of(c * blk, blk), blk), :]
            for r in range(blk // SUBLANES):
                acc = acc + jnp.where(keys[r * SUBLANES:(r + 1) * SUBLANES, :] >= trial8, 1, 0)
            return acc

        acc = lax.fori_loop(0, i + 1, cnt_body, jnp.zeros((SUBLANES, blk), I32))
        return jnp.sum(acc, axis=0, keepdims=True)

    live = thr != jnp.int32(INT_MIN)
    surplus = jnp.where(live, count_keys_at_least(thr) - k_sel, 0)

    @pl.when(jnp.max(surplus.astype(F32)) > 0.0)
    def _():
        above_thr = jnp.where(thr == jnp.int32(-INT_MIN - 1), 0,
                              count_keys_at_least(jnp.minimum(thr, jnp.int32(-INT_MIN - 2)) + 1))
        allowed = (k_sel - above_thr).astype(F32)
        prefix_ones = jnp.where(tcol <= srow, 1.0, 0.0).astype(BF16)

        def demote(c, seen):
            rows = pl.ds(pl.multiple_of(c * blk, blk), blk)
            keys = skey_ref[rows, :]
            tied = jnp.logical_and(keys == thr, live)
            rank = seen + jnp.dot(prefix_ones, jnp.where(tied, 1.0, 0.0).astype(BF16), preferred_element_type=F32)
            skey_ref[rows, :] = jnp.where(jnp.logical_and(tied, rank > allowed), jnp.int32(INT_MIN), keys)
            return rank[blk - 1:blk, :]

        lax.fori_loop(0, i + 1, demote, jnp.zeros((1, blk), F32))

    qt = (aq_ref[0].astype(F32) * (HEAD_DIM ** -0.5 * LOG2E)).T.astype(BF16)

    m_ref[...] = jnp.full_like(m_ref, -jnp.inf)
    l_ref[...] = jnp.zeros_like(l_ref)
    acc_ref[...] = jnp.zeros_like(acc_ref)

    def logits(c):
        ka = ak_ref[0, pl.ds(pl.multiple_of(c * blk, blk), blk), :]
        return [jnp.dot(ka, qt[h * LANES:(h + 1) * LANES, :], preferred_element_type=F32) for h in range(A_HEADS)]

    def accumulate(c, slot):
        vt = vat_ref[c]
        for h in range(A_HEADS):
            pv = jnp.dot(vt, p_ref[slot, h], preferred_element_type=F32)
            acc_ref[h] = acc_ref[h] * al_ref[slot, h] + pv

    p_ref[1] = jnp.zeros_like(p_ref[1])
    al_ref[1] = jnp.ones_like(al_ref[1])
    first = logits(0)
    for h in range(A_HEADS):
        lg0_ref[h] = first[h]

    def stage(c, slot, has_next=True):
        if has_next:
            nxt = logits(jnp.minimum(c + 1, i))
            for h in range(A_HEADS):
                lg_refs[1 - slot][h] = nxt[h]
        accumulate(jnp.maximum(c - 1, 0), 1 - slot)
        kind = jnp.clip(c - i + 2, 0, 2)
        sel = skey_ref[pl.ds(pl.multiple_of(c * blk, blk), blk), :] >= thr
        for h in range(A_HEADS):
            lg = jnp.where(sel, lg_refs[slot][h] + bias_ref[kind, h], -jnp.inf)
            m_old = m_ref[h]
            m_new = jnp.maximum(m_old, jnp.max(lg, axis=0, keepdims=True))
            m_use = jnp.where(m_new == -jnp.inf, 0.0, m_new)
            alpha = jnp.exp2(m_old - m_use)
            p = jnp.exp2(lg - m_use)
            l_ref[h] = l_ref[h] * alpha + jnp.sum(p, axis=0, keepdims=True)
            m_ref[h] = m_new
            p_ref[slot, h] = p.astype(BF16)
            al_ref[slot, h] = alpha

    def pair_body(j, carry):
        stage(2 * j, 0)
        stage(2 * j + 1, 1)
        return carry

    n_chunks = i + 1
    lax.fori_loop(0, n_chunks // 2, pair_body, 0)

    @pl.when(n_chunks % 2 == 1)
    def _():
        stage(i, 0, has_next=False)

    accumulate(i, i % 2)

    for h in range(A_HEADS):
        o_ref[0, :, h * LANES:(h + 1) * LANES] = (acc_ref[h] / l_ref[h]).T.astype(BF16)


def _dsa(proj3, bias_tab, k_sel):
    bsz, seq, _ = proj3.shape
    blk = BLOCK
    return pl.pallas_call(
        functools.partial(_dsa_kernel, k_sel=k_sel, seq=seq),
        grid=(bsz, seq // blk),
        in_specs=[
            pl.BlockSpec((1, blk, A_WIDTH), lambda b, i: (b, i, 0)),
            pl.BlockSpec((1, blk, IDX_HEADS * IDX_DIM), lambda b, i: (b, i, 1)),
            pl.BlockSpec((1, blk, LANES), lambda b, i: (b, i, BLK_IW)),
            pl.BlockSpec((1, seq, LANES), lambda b, i: (b, 0, BLK_IK)),
            pl.BlockSpec((1, seq, LANES), lambda b, i: (b, 0, BLK_AK)),
            pl.BlockSpec((1, seq, LANES), lambda b, i: (b, 0, BLK_AV)),
            pl.BlockSpec(bias_tab.shape, lambda b, i: (0, 0, 0, 0)),
        ],
        out_specs=pl.BlockSpec((1, blk, A_WIDTH), lambda b, i: (b, i, 0)),
        out_shape=jax.ShapeDtypeStruct((bsz, seq, A_WIDTH), BF16),
        scratch_shapes=[
            pltpu.VMEM((seq, blk), I32),
            pltpu.VMEM((seq, blk), I16),
            pltpu.VMEM((seq, blk), I16),
            pltpu.VMEM((seq, LANES), BF16),
            pltpu.VMEM((seq, LANES), BF16),
            pltpu.VMEM((seq // blk, HEAD_DIM, blk), BF16),
            pltpu.VMEM((A_HEADS, 1, blk), F32),
            pltpu.VMEM((A_HEADS, 1, blk), F32),
            pltpu.VMEM((A_HEADS, HEAD_DIM, blk), F32),
            pltpu.VMEM((A_HEADS, blk, blk), F32),
            pltpu.VMEM((A_HEADS, blk, blk), F32),
            pltpu.VMEM((2, A_HEADS, blk, blk), BF16),
            pltpu.VMEM((2, A_HEADS, 1, blk), F32),
        ],
        compiler_params=_params(("parallel", "arbitrary")),
        name="dsa",
    )(proj3, proj3, proj3, proj3, proj3, proj3, bias_tab)


def _rms_rows(x, g):
    return x * lax.rsqrt(jnp.mean(x * x, axis=-1, keepdims=True) + RMS_EPS) * g


def _mla_proj_kernel(cq_ref, ckv_ref, kra_ref, krb_ref, cc_ref, ss_ref, qn_ref, kvn_ref, wq_ref, wkv_ref,
                     qt_ref, k_ref, vt_ref):
    tm = cq_ref.shape[0]
    cc = cc_ref[...]
    ss = ss_ref[...]
    qscale = (NOPE_DIM + ROPE_DIM) ** -0.5 * LOG2E
    cq = _rms_rows(cq_ref[...].astype(F32), qn_ref[0]).astype(BF16)
    q3 = jnp.dot(cq, wq_ref[0], preferred_element_type=F32)
    for h in range(B_HEADS):
        base = 3 * LANES * h
        rot = q3[:, base + LANES:base + 2 * LANES] * cc + q3[:, base + 2 * LANES:base + 3 * LANES] * ss
        qt_ref[h, 0:LANES, :] = (q3[:, base:base + LANES] * qscale).T.astype(BF16)
        qt_ref[h, LANES:MLA_QK, :] = (rot * qscale).T.astype(BF16)
    ckv = _rms_rows(ckv_ref[...].astype(F32), kvn_ref[0]).astype(BF16)
    kv = jnp.dot(ckv, wkv_ref[0], preferred_element_type=F32)
    krot = (kra_ref[...].astype(F32) * cc + krb_ref[...].astype(F32) * ss).astype(BF16)
    for h in range(B_HEADS):
        k_ref[:, MLA_QK * h:MLA_QK * h + LANES] = kv[:, h * LANES:(h + 1) * LANES].astype(BF16)
        k_ref[:, MLA_QK * h + LANES:MLA_QK * (h + 1)] = krot
        v_h = kv[:, B_HEADS * NOPE_DIM + h * V_DIM:B_HEADS * NOPE_DIM + (h + 1) * V_DIM]
        for c in range(tm // BLOCK):
            vt_ref[h, c] = v_h[c * BLOCK:(c + 1) * BLOCK, :].T.astype(BF16)


def _mla_proj(proj, cc, ss, qn, kvn, wq, wkv, layer, tm=512):
    n = proj.shape[0]
    per_layer = lambda i: (layer, 0, 0)
    return pl.pallas_call(
        _mla_proj_kernel,
        grid=(n // tm,),
        in_specs=[
            pl.BlockSpec((tm, Q_LORA), lambda i: (i, BLK512_CQ)),
            pl.BlockSpec((tm, KV_LORA), lambda i: (i, BLK512_CKV)),
            pl.BlockSpec((tm, LANES), lambda i: (i, BLK_KRA)),
            pl.BlockSpec((tm, LANES), lambda i: (i, BLK_KRB)),
            pl.BlockSpec((tm, LANES), lambda i: (i, 0)),
            pl.BlockSpec((tm, LANES), lambda i: (i, 0)),
            pl.BlockSpec((1, 1, Q_LORA), per_layer),
            pl.BlockSpec((1, 1, KV_LORA), per_layer),
            pl.BlockSpec((1,) + wq.shape[1:], per_layer),
            pl.BlockSpec((1,) + wkv.shape[1:], per_layer),
        ],
        out_specs=[
            pl.BlockSpec((B_HEADS, MLA_QK, tm), lambda i: (0, 0, i)),
            pl.BlockSpec((tm, B_HEADS * MLA_QK), lambda i: (i, 0)),
            pl.BlockSpec((B_HEADS, tm // BLOCK, V_DIM, BLOCK), lambda i: (0, i, 0, 0)),
        ],
        out_shape=[jax.ShapeDtypeStruct((B_HEADS, MLA_QK, n), BF16),
                   jax.ShapeDtypeStruct((n, B_HEADS * MLA_QK), BF16),
                   jax.ShapeDtypeStruct((B_HEADS, n // BLOCK, V_DIM, BLOCK), BF16)],
        compiler_params=_params(("parallel",)),
        name="mla_proj",
    )(proj, proj, proj, proj, cc, ss, qn, kvn, wq, wkv)


def _mla_attn_kernel(qt_ref, k_ref, vt_ref, o_ref, lg0_ref, lg1_ref, p_ref, al_ref, m_ref, l_ref, acc_ref):
    lg_refs = (lg0_ref, lg1_ref)
    qi = pl.program_id(1)
    blk = BLOCK
    srow = lax.broadcasted_iota(I32, (blk, blk), 0)
    tcol = lax.broadcasted_iota(I32, (blk, blk), 1)
    causal_bias = jnp.where(srow <= tcol, 0.0, -jnp.inf).astype(F32)

    def logits(c):
        s0 = pl.multiple_of(c * blk, blk)
        return [jnp.dot(k_ref[0, pl.ds(s0, blk), h * MLA_QK:(h + 1) * MLA_QK], qt_ref[h],
                        preferred_element_type=F32) for h in range(B_HEADS)]

    def accumulate(c, slot):
        for h in range(B_HEADS):
            pv = jnp.dot(vt_ref[h, c], p_ref[slot, h], preferred_element_type=F32)
            acc_ref[h] = acc_ref[h] * al_ref[slot, h] + pv

    m_ref[...] = jnp.full_like(m_ref, -jnp.inf)
    l_ref[...] = jnp.zeros_like(l_ref)
    acc_ref[...] = jnp.zeros_like(acc_ref)
    p_ref[1] = jnp.zeros_like(p_ref[1])
    al_ref[1] = jnp.ones_like(al_ref[1])
    first = logits(0)
    for h in range(B_HEADS):
        lg0_ref[h] = first[h]
    def stage(c, slot, has_next=True):
        if has_next:
            nxt = logits(jnp.minimum(c + 1, qi))
            for h in range(B_HEADS):
                lg_refs[1 - slot][h] = nxt[h]
        accumulate(jnp.maximum(c - 1, 0), 1 - slot)
        mask = jnp.where(c == qi, causal_bias, 0.0)
        for h in range(B_HEADS):
            lg = lg_refs[slot][h] + mask
            m_old = m_ref[h]
            m_new = jnp.maximum(m_old, jnp.max(lg, axis=0, keepdims=True))
            alpha = jnp.exp2(m_old - m_new)
            p = jnp.exp2(lg - m_new)
            l_ref[h] = l_ref[h] * alpha + jnp.sum(p, axis=0, keepdims=True)
            m_ref[h] = m_new
            p_ref[slot, h] = p.astype(BF16)
            al_ref[slot, h] = alpha

    def pair_body(j, carry):
        stage(2 * j, 0)
        stage(2 * j + 1, 1)
        return carry

    n_chunks = qi + 1
    lax.fori_loop(0, n_chunks // 2, pair_body, 0)

    @pl.when(n_chunks % 2 == 1)
    def _():
        stage(qi, 0, has_next=False)

    accumulate(qi, qi % 2)
    for h in range(B_HEADS):
        o_ref[0, :, h * V_DIM:(h + 1) * V_DIM] = (acc_ref[h] / l_ref[h]).T.astype(BF16)


def _mla_attn(qt, k3, vt):
    bsz, seq, _ = k3.shape
    blk = BLOCK
    nq = seq // blk
    return pl.pallas_call(
        _mla_attn_kernel,
        grid=(bsz, nq),
        in_specs=[
            pl.BlockSpec((B_HEADS, MLA_QK, blk), lambda b, i: (0, 0, b * nq + i)),
            pl.BlockSpec((1, seq, B_HEADS * MLA_QK), lambda b, i: (b, 0, 0)),
            pl.BlockSpec((B_HEADS, nq, V_DIM, blk), lambda b, i: (0, b, 0, 0)),
        ],
        out_specs=pl.BlockSpec((1, blk, B_WIDTH), lambda b, i: (b, i, 0)),
        out_shape=jax.ShapeDtypeStruct((bsz, seq, B_WIDTH), BF16),
        scratch_shapes=[
            pltpu.VMEM((B_HEADS, blk, blk), F32),
            pltpu.VMEM((B_HEADS, blk, blk), F32),
            pltpu.VMEM((2, B_HEADS, blk, blk), BF16),
            pltpu.VMEM((2, B_HEADS, 1, blk), F32),
            pltpu.VMEM((B_HEADS, 1, blk), F32),
            pltpu.VMEM((B_HEADS, 1, blk), F32),
            pltpu.VMEM((B_HEADS, V_DIM, blk), F32),
        ],
        compiler_params=_params(("parallel", "arbitrary")),
        name="mla_attn",
    )(qt, k3, vt)


def _mem_attn_kernel(q_ref, kv_ref, o_ref):
    scale = HEAD_DIM ** -0.5
    for h in range(C_HEADS):
        q = q_ref[0, :, h * LANES:(h + 1) * LANES]
        k = kv_ref[0, :, h * LANES:(h + 1) * LANES]
        v = kv_ref[0, :, C_WIDTH + h * LANES:C_WIDTH + (h + 1) * LANES]
        s = lax.dot_general(q, k, (((1,), (1,)), ((), ())), preferred_element_type=F32) * scale
        p = jnp.exp(s - jnp.max(s, axis=-1, keepdims=True))
        l = jnp.sum(p, axis=-1, keepdims=True)
        o = jnp.dot(p.astype(BF16), v, preferred_element_type=F32) / l
        o_ref[0, :, h * LANES:(h + 1) * LANES] = o.astype(BF16)


def _mem_attn(proj3, mkv3, tq=512):
    bsz, seq, _ = proj3.shape
    mlen = mkv3.shape[1]
    return pl.pallas_call(
        _mem_attn_kernel,
        grid=(bsz, seq // tq),
        in_specs=[pl.BlockSpec((1, tq, C_WIDTH), lambda b, i: (b, i, BLK512_C)),
                  pl.BlockSpec((1, mlen, 2 * C_WIDTH), lambda b, i: (b, 0, 0))],
        out_specs=pl.BlockSpec((1, tq, C_WIDTH), lambda b, i: (b, i, 0)),
        out_shape=jax.ShapeDtypeStruct((bsz, seq, C_WIDTH), BF16),
        compiler_params=_params(("parallel", "arbitrary")),
        name="mem_attn",
    )(proj3, mkv3)


def _mix_out_kernel(oa_ref, ob_ref, oc_ref, ga_ref, gb_ref, gc_ref, x_ref, wb_ref, wo_ref, g_ref, b_ref,
                    o_ref, *, alpha):
    def gated(o_blk, gate_ref, r0, r1):
        y = jnp.dot(o_blk, wb_ref[0, r0:r1, :], preferred_element_type=F32)
        return y * (1.0 / (1.0 + jnp.exp(-gate_ref[...].astype(F32))))

    y = gated(oa_ref[...], ga_ref, 0, A_WIDTH)
    y = y + gated(ob_ref[...], gb_ref, A_WIDTH, A_WIDTH + B_WIDTH)
    y = y + gated(oc_ref[...], gc_ref, A_WIDTH + B_WIDTH, A_WIDTH + B_WIDTH + C_WIDTH)
    mix = jnp.dot(y.astype(BF16), wo_ref[0], preferred_element_type=F32)
    o_ref[...] = _layer_norm_rows(alpha * x_ref[...] + mix, g_ref[0], b_ref[0])


def _mix_out(oa, ob, oc, proj, x, wb, wo, ln_g, ln_b, layer, alpha, tm=256):
    n = x.shape[0]
    per_layer = lambda i: (layer, 0, 0)
    ln_row = lambda i: (3 * layer + 1, 0, 0)
    return pl.pallas_call(
        functools.partial(_mix_out_kernel, alpha=alpha),
        grid=(n // tm,),
        in_specs=[
            pl.BlockSpec((tm, A_WIDTH), lambda i: (i, 0)),
            pl.BlockSpec((tm, B_WIDTH), lambda i: (i, 0)),
            pl.BlockSpec((tm, C_WIDTH), lambda i: (i, 0)),
            pl.BlockSpec((tm, D_MODEL), lambda i: (i, BLK2048_GATE0)),
            pl.BlockSpec((tm, D_MODEL), lambda i: (i, BLK2048_GATE0 + 1)),
            pl.BlockSpec((tm, D_MODEL), lambda i: (i, BLK2048_GATE0 + 2)),
            pl.BlockSpec((tm, D_MODEL), lambda i: (i, 0)),
            pl.BlockSpec((1,) + wb.shape[1:], per_layer),
            pl.BlockSpec((1,) + wo.shape[1:], per_layer),
            pl.BlockSpec((1, 1, D_MODEL), ln_row),
            pl.BlockSpec((1, 1, D_MODEL), ln_row),
        ],
        out_specs=pl.BlockSpec((tm, D_MODEL), lambda i: (i, 0)),
        out_shape=jax.ShapeDtypeStruct((n, D_MODEL), F32),
        compiler_params=_params(("parallel",)),
        name="mix_out",
    )(oa, ob, oc, proj, proj, proj, x, wb, wo, ln_g, ln_b)


_IN_OFFS = tuple(sum(IN_SIZES[:k]) for k in range(len(IN_SIZES) + 1))


def _pack_w_in_t(w):
    wt = jnp.swapaxes(w, 1, 2)
    o = _IN_OFFS
    seg = lambda k: wt[:, o[k]:o[k + 1]]
    zeros = lambda r: jnp.zeros((w.shape[0], r, w.shape[1]), w.dtype)
    ik, iw, kr = seg(4), seg(5), seg(8)
    half = ROPE_DIM // 2
    packed = jnp.concatenate([
        wt[:, :o[4]],
        ik, ik,
        iw, zeros(LANES - IDX_HEADS),
        kr, zeros(LANES - ROPE_DIM),
        kr[:, half:], kr[:, :half], zeros(LANES - ROPE_DIM),
        seg(6), seg(7), seg(9), seg(10),
    ], axis=1)
    assert packed.shape[1] == PROJ_COLS
    return packed.astype(BF16)


def _pack_w_uq(w):
    half = ROPE_DIM // 2
    zeros = jnp.zeros(w.shape[:-1] + (LANES - ROPE_DIM,), w.dtype)
    cols = []
    for h in range(B_HEADS):
        base = h * (NOPE_DIM + ROPE_DIM)
        r = w[..., base + NOPE_DIM:base + NOPE_DIM + ROPE_DIM]
        cols += [w[..., base:base + NOPE_DIM], r, zeros, r[..., half:], r[..., :half], zeros]
    return jnp.concatenate(cols, axis=-1).astype(BF16)


def _pack_w_ukv(w):
    step = NOPE_DIM + V_DIM
    ks = [w[..., h * step:h * step + NOPE_DIM] for h in range(B_HEADS)]
    vs = [w[..., h * step + NOPE_DIM:(h + 1) * step] for h in range(B_HEADS)]
    return jnp.concatenate(ks + vs, axis=-1).astype(BF16)


def _rel_bucket(dist):
    n = jnp.maximum(dist, 0)
    max_exact = REL_BUCKETS // 2
    nf = jnp.maximum(n, 1).astype(F32)
    large = max_exact + (jnp.log(nf / max_exact) / math.log(REL_MAX_DIST / max_exact)
                         * (REL_BUCKETS - max_exact)).astype(I32)
    large = jnp.minimum(large, REL_BUCKETS - 1)
    return jnp.where(n < max_exact, n, large)


def _bias_tables(rel_bias):
    assert BLOCK + 1 >= REL_MAX_DIST
    s = jnp.arange(BLOCK)[:, None]
    t = jnp.arange(BLOCK)[None, :]
    tiles = [jnp.zeros((A_HEADS, BLOCK, BLOCK), F32)]
    for block_gap in (1, 0):
        dist = t - s + block_gap * BLOCK
        onehot = (_rel_bucket(dist)[:, :, None] == jnp.arange(REL_BUCKETS)).astype(F32)
        looked_up = jnp.einsum("stb,bh->sth", onehot, rel_bias, precision=lax.Precision.HIGHEST)
        tile = (looked_up - rel_bias[REL_BUCKETS - 1]) * LOG2E
        tile = jnp.where((dist >= 0)[:, :, None], tile, -jnp.inf)
        tiles.append(jnp.transpose(tile, (2, 0, 1)))
    return jnp.stack(tiles).astype(F32)


def kernel(x, mem, positions, rel_bias, ln_g, ln_b, ffn1_up, ffn1_down, w_in, q_norm, kv_norm, w_uq, w_ukv,
           w_mem_kv, w_branch, w_out, ffn2_up, ffn2_down):
    bsz, seq, d = x.shape
    depth = ffn1_up.shape[0]
    n = bsz * seq
    alpha = (2 * depth) ** 0.25
    k_sel = min(TOPK_MAX, seq // 4)
    assert d == D_MODEL and seq % (2 * BLOCK) == 0

    inv_freq = ROPE_THETA ** (-jnp.arange(0, ROPE_DIM, 2, dtype=F32) / ROPE_DIM)
    ang = positions.astype(F32)[..., None] * inv_freq
    cos, sin = jnp.cos(ang).reshape(n, -1), jnp.sin(ang).reshape(n, -1)
    pad = jnp.zeros((n, LANES - ROPE_DIM), F32)
    cc = jnp.concatenate([cos, cos, pad], axis=1)
    ss = jnp.concatenate([-sin, sin, pad], axis=1)
    bias_tab = _bias_tables(rel_bias)

    xf = x.reshape(n, d)
    memf = mem.reshape(bsz * mem.shape[1], d)
    ln_g3 = ln_g.reshape(depth * 3, 1, d)
    ln_b3 = ln_b.reshape(depth * 3, 1, d)
    qn3 = q_norm.reshape(depth, 1, Q_LORA)
    kvn3 = kv_norm.reshape(depth, 1, KV_LORA)
    w_in_p = _pack_w_in_t(w_in)
    w_uq_p = _pack_w_uq(w_uq.astype(BF16))
    w_ukv_p = _pack_w_ukv(w_ukv.astype(BF16))
    w_branch_b = w_branch.astype(BF16)
    w_out_b = w_out.astype(BF16)

    for l in range(depth):
        xf = _ffn(xf, ffn1_up, ffn1_down, ln_g3, ln_b3, l, 0, alpha)
        proj = _proj(xf, w_in_p, l, 1024, 1024, "in_proj", w_is_transposed=True)
        proj3 = proj.reshape(bsz, seq, PROJ_COLS)
        o_a = _dsa(proj3, bias_tab, k_sel).reshape(n, A_WIDTH)
        qt, k, vt = _mla_proj(proj, cc, ss, qn3, kvn3, w_uq_p, w_ukv_p, l)
        o_b = _mla_attn(qt, k.reshape(bsz, seq, -1), vt).reshape(n, B_WIDTH)
        mkv = _proj(memf, w_mem_kv, l, 512, 1024, "mem_proj")
        o_c = _mem_attn(proj3, mkv.reshape(bsz, mem.shape[1], 2 * C_WIDTH)).reshape(n, C_WIDTH)
        xf = _mix_out(o_a, o_b, o_c, proj, xf, w_branch_b, w_out_b, ln_g3, ln_b3, l, alpha)
        xf = _ffn(xf, ffn2_up, ffn2_down, ln_g3, ln_b3, l, 2, alpha)
    return xf.reshape(bsz, seq, d)
```

```python
import functools
import math

import jax
import jax.numpy as jnp
from jax import lax
from jax.experimental import pallas as pl
from jax.experimental.pallas import tpu as pltpu

F32 = jnp.float32
BF16 = jnp.bfloat16
I32 = jnp.int32
I16 = jnp.int16

D_MODEL = 2048
D_FF = 5632
HEAD_DIM = 128
A_HEADS = 6
IDX_HEADS = 16
IDX_DIM = 64
TOPK_MAX = 256
B_HEADS = 6
Q_LORA = 512
KV_LORA = 512
NOPE_DIM = 128
ROPE_DIM = 64
V_DIM = 128
ROPE_THETA = 10000.0
C_HEADS = 4
REL_BUCKETS = 32
REL_MAX_DIST = 128
LN_EPS = 1e-5
RMS_EPS = 1e-6
A_WIDTH = A_HEADS * HEAD_DIM
B_WIDTH = B_HEADS * V_DIM
C_WIDTH = C_HEADS * HEAD_DIM
IN_SIZES = (A_WIDTH, HEAD_DIM, HEAD_DIM, IDX_HEADS * IDX_DIM, IDX_DIM, IDX_HEADS,
            Q_LORA, KV_LORA, ROPE_DIM, C_WIDTH, 3 * D_MODEL)

LANES = 128
SUBLANES = 8
BLOCK = 256
MLA_QK = 2 * LANES
ONES_ROWS = 16
VT_ROWS = HEAD_DIM + ONES_ROWS
INT_MIN = -2 ** 31
HALF_RANGE = 2 ** 15
LOG2E = 1.4426950408889634

PROJ_COLS = 10240
BLK_AK, BLK_AV = 6, 7
BLK_IK, BLK_IW, BLK_KRA, BLK_KRB = 16, 17, 18, 19
BLK512_CQ, BLK512_CKV, BLK512_C = 5, 6, 7
BLK2048_GATE0 = 2

VMEM_LIMIT = 56 * 1024 * 1024
FFN_VMEM_LIMIT = 60 * 1024 * 1024


def _params(sem, vmem=VMEM_LIMIT):
    return pltpu.CompilerParams(dimension_semantics=sem, vmem_limit_bytes=vmem)


def _layer_norm_rows(y, g, b, scale=1.0):
    mu = jnp.mean(y, axis=-1, keepdims=True)
    d = y - mu
    var = jnp.mean(d * d, axis=-1, keepdims=True)
    return d * (scale * lax.rsqrt(scale * scale * var + LN_EPS)) * g + b


def _ffn_kernel(x_ref, wg_ref, wu_ref, wd_ref, g_ref, b_ref, o_ref, xb_ref, *, alpha):
    j = pl.program_id(1)

    @pl.when(j == 0)
    def _():
        x = x_ref[...]
        xb_ref[...] = x.astype(BF16)
        o_ref[...] = (2.0 * alpha) * x

    xb = xb_ref[...]
    gate = jnp.dot(xb, wg_ref[0].astype(BF16), preferred_element_type=F32)
    up = jnp.dot(xb, wu_ref[0].astype(BF16), preferred_element_type=F32)
    h = (gate * (1.0 / (1.0 + jnp.exp(-gate))) * up).astype(BF16)
    o_ref[...] += jnp.dot(h, wd_ref[0].astype(BF16), preferred_element_type=F32)

    @pl.when(j == pl.num_programs(1) - 1)
    def _():
        o_ref[...] = _layer_norm_rows(o_ref[...], g_ref[0], b_ref[0], scale=0.5)


def _ffn(x, w_up, w_down, ln_g, ln_b, layer, which, alpha, tm=1024, tf=256):
    n = x.shape[0]
    tm = min(tm, n)
    nf = D_FF // tf
    return pl.pallas_call(
        functools.partial(_ffn_kernel, alpha=alpha),
        grid=(n // tm, nf),
        in_specs=[
            pl.BlockSpec((tm, D_MODEL), lambda i, j: (i, 0)),
            pl.BlockSpec((1, D_MODEL, tf), lambda i, j: (layer, 0, j)),
            pl.BlockSpec((1, D_MODEL, tf), lambda i, j: (layer, 0, j + nf)),
            pl.BlockSpec((1, tf, D_MODEL), lambda i, j: (layer, j, 0)),
            pl.BlockSpec((1, 1, D_MODEL), lambda i, j: (3 * layer + which, 0, 0)),
            pl.BlockSpec((1, 1, D_MODEL), lambda i, j: (3 * layer + which, 0, 0)),
        ],
        out_specs=pl.BlockSpec((tm, D_MODEL), lambda i, j: (i, 0)),
        out_shape=jax.ShapeDtypeStruct((n, D_MODEL), F32),
        scratch_shapes=[pltpu.VMEM((tm, D_MODEL), BF16)],
        compiler_params=_params(("parallel", "arbitrary"), FFN_VMEM_LIMIT),
        name="ffn",
    )(x, w_up, w_up, w_down, ln_g, ln_b)


def _proj_kernel(x_ref, w_ref, o_ref, xb_ref, *, w_is_transposed):
    @pl.when(pl.program_id(1) == 0)
    def _():
        xb_ref[...] = x_ref[...].astype(BF16)

    contract_w = 1 if w_is_transposed else 0
    o_ref[...] = lax.dot_general(xb_ref[...], w_ref[0].astype(BF16), (((1,), (contract_w,)), ((), ())),
                                 preferred_element_type=F32).astype(o_ref.dtype)


def _proj(x, w, layer, tm, tn, name, w_is_transposed=False):
    n, k = x.shape
    tm = min(tm, n)
    cols = w.shape[1] if w_is_transposed else w.shape[2]
    w_spec = (pl.BlockSpec((1, tn, k), lambda i, j: (layer, j, 0)) if w_is_transposed
              else pl.BlockSpec((1, k, tn), lambda i, j: (layer, 0, j)))
    return pl.pallas_call(
        functools.partial(_proj_kernel, w_is_transposed=w_is_transposed),
        grid=(n // tm, cols // tn),
        in_specs=[pl.BlockSpec((tm, k), lambda i, j: (i, 0)), w_spec],
        out_specs=pl.BlockSpec((tm, tn), lambda i, j: (i, j)),
        out_shape=jax.ShapeDtypeStruct((n, cols), BF16),
        scratch_shapes=[pltpu.VMEM((tm, k), BF16)],
        compiler_params=_params(("parallel", "arbitrary")),
        name=name,
    )(x, w)


def _dsa_kernel(aq_ref, iq_ref, iw_ref, ik_ref, ak_ref, av_ref, bias_ref, o_ref,
                skey_ref, hi_ref, lo_ref, ika_ref, ikb_ref, vat_ref, m_ref, acc_ref,
                lg0_ref, lg1_ref, p_ref, al_ref, *, k_sel, seq):
    i = pl.program_id(1)
    blk = BLOCK
    lg_refs = (lg0_ref, lg1_ref)

    @pl.when(i == 0)
    def _():
        ik = ik_ref[0]
        lane = lax.broadcasted_iota(I32, ik.shape, 1)
        zero = jnp.zeros_like(ik)
        ika_ref[...] = jnp.where(lane < IDX_DIM, ik, zero)
        ikb_ref[...] = jnp.where(lane >= IDX_DIM, ik, zero)
        for c in range(seq // blk):
            vat_ref[c, 0:HEAD_DIM, :] = av_ref[0, c * blk:(c + 1) * blk, :].astype(F32).T.astype(BF16)
            vat_ref[c, HEAD_DIM:VT_ROWS, :] = jnp.ones((ONES_ROWS, blk), BF16)

    iqt = iq_ref[0].astype(F32).T.astype(BF16)
    wt = iw_ref[0].astype(F32).T * (IDX_HEADS * IDX_DIM) ** -0.5

    srow = lax.broadcasted_iota(I32, (blk, blk), 0)
    tcol = lax.broadcasted_iota(I32, (blk, blk), 1)

    def score_body(c, carry):
        s0 = pl.multiple_of(c * blk, blk)
        ka = ika_ref[pl.ds(s0, blk), :]
        kb = ikb_ref[pl.ds(s0, blk), :]
        sc = jnp.zeros((blk, blk), F32)
        for j in range(IDX_HEADS // 2):
            rhs = iqt[j * LANES:(j + 1) * LANES, :]
            sc = sc + wt[2 * j:2 * j + 1, :] * jnp.maximum(jnp.dot(ka, rhs, preferred_element_type=F32), 0.0)
            sc = sc + wt[2 * j + 1:2 * j + 2, :] * jnp.maximum(jnp.dot(kb, rhs, preferred_element_type=F32), 0.0)
        bits = pltpu.bitcast(sc, I32)
        key = bits ^ ((bits >> 31) & jnp.int32(0x7FFFFFFF))
        causal = srow <= tcol + jnp.where(c < i, jnp.int32(blk), jnp.int32(0))
        key = jnp.where(causal, key, jnp.int32(INT_MIN))
        skey_ref[pl.ds(s0, blk), :] = key
        hi_ref[pl.ds(s0, blk), :] = (key >> 16).astype(I16)
        lo_ref[pl.ds(s0, blk), :] = ((key & jnp.int32(0xFFFF)) - HALF_RANGE).astype(I16)
        return carry

    lax.fori_loop(0, i + 1, score_body, 0)

    n_acc = 4
    rows16 = 2 * SUBLANES

    def count_at_least(ref, trial):
        trial16 = jnp.broadcast_to(trial.astype(I16), (rows16, blk))

        def cnt_body(c, accs):
            vals = ref[pl.ds(pl.multiple_of(c * blk, blk), blk), :]
            accs = list(accs)
            for r in range(blk // rows16):
                rows = vals[r * rows16:(r + 1) * rows16, :]
                accs[r % n_acc] = jnp.where(rows >= trial16, accs[r % n_acc] + jnp.int16(1), accs[r % n_acc])
            return tuple(accs)

        zero = jnp.zeros((rows16, blk), I16)
        accs = lax.fori_loop(0, i + 1, cnt_body, (zero,) * n_acc)
        total = (accs[0] + accs[1]) + (accs[2] + accs[3])
        return jnp.sum(total.astype(I32), axis=0, keepdims=True)

    def largest_with_count(ref, need):
        def bit_body(it, u):
            trial_u = u | jnp.left_shift(jnp.int32(1), jnp.int32(15) - it)
            return jnp.where(count_at_least(ref, trial_u - HALF_RANGE) >= need, trial_u, u)

        return lax.fori_loop(0, 16, bit_body, jnp.zeros((1, blk), I32))

    hi_thr = largest_with_count(hi_ref, k_sel) - HALF_RANGE
    above = jnp.where(hi_thr == HALF_RANGE - 1, 0, count_at_least(hi_ref, jnp.minimum(hi_thr + 1, HALF_RANGE - 1)))
    hi_thr16 = jnp.broadcast_to(hi_thr.astype(I16), (blk, blk))

    def keep_candidates(c, carry):
        rows = pl.ds(pl.multiple_of(c * blk, blk), blk)
        lo_ref[rows, :] = jnp.where(hi_ref[rows, :] == hi_thr16, lo_ref[rows, :], jnp.int16(-HALF_RANGE))
        return carry

    lax.fori_loop(0, i + 1, keep_candidates, 0)
    lo_thr = largest_with_count(lo_ref, k_sel - above)
    thr = hi_thr * (2 * HALF_RANGE) + lo_thr

    def count_keys_at_least(trial):
        trial8 = jnp.broadcast_to(trial, (SUBLANES, blk))

        def cnt_body(c, acc):
            keys = skey_ref[pl.ds(pl.multiple_of(c * blk, blk), blk), :]
            for r in range(blk // SUBLANES):
                acc = acc + jnp.where(keys[r * SUBLANES:(r + 1) * SUBLANES, :] >= trial8, 1, 0)
            return acc

        acc = lax.fori_loop(0, i + 1, cnt_body, jnp.zeros((SUBLANES, blk), I32))
        return jnp.sum(acc, axis=0, keepdims=True)

    live = thr != jnp.int32(INT_MIN)
    surplus = jnp.where(live, count_keys_at_least(thr) - k_sel, 0)

    @pl.when(jnp.max(surplus.astype(F32)) > 0.0)
    def _():
        above_thr = jnp.where(thr == jnp.int32(-INT_MIN - 1), 0,
                              count_keys_at_least(jnp.minimum(thr, jnp.int32(-INT_MIN - 2)) + 1))
        allowed = (k_sel - above_thr).astype(F32)
        prefix_ones = jnp.where(tcol <= srow, 1.0, 0.0).astype(BF16)

        def demote(c, seen):
            rows = pl.ds(pl.multiple_of(c * blk, blk), blk)
            keys = skey_ref[rows, :]
            tied = jnp.logical_and(keys == thr, live)
            rank = seen + jnp.dot(prefix_ones, jnp.where(tied, 1.0, 0.0).astype(BF16), preferred_element_type=F32)
            skey_ref[rows, :] = jnp.where(jnp.logical_and(tied, rank > allowed), jnp.int32(INT_MIN), keys)
            return rank[blk - 1:blk, :]

        lax.fori_loop(0, i + 1, demote, jnp.zeros((1, blk), F32))

    qt = (aq_ref[0].astype(F32) * (HEAD_DIM ** -0.5 * LOG2E)).T.astype(BF16)

    m_ref[...] = jnp.full_like(m_ref, -jnp.inf)
    acc_ref[...] = jnp.zeros_like(acc_ref)

    def logits(c):
        ka = ak_ref[0, pl.ds(pl.multiple_of(c * blk, blk), blk), :]
        return [jnp.dot(ka, qt[h * LANES:(h + 1) * LANES, :], preferred_element_type=F32) for h in range(A_HEADS)]

    def accumulate(c, slot):
        vt = vat_ref[c]
        for h in range(A_HEADS):
            pv = jnp.dot(vt, p_ref[slot, h], preferred_element_type=F32)
            acc_ref[h] = acc_ref[h] * al_ref[slot, h] + pv

    p_ref[1] = jnp.zeros_like(p_ref[1])
    al_ref[1] = jnp.ones_like(al_ref[1])
    first = logits(0)
    for h in range(A_HEADS):
        lg0_ref[h] = first[h]

    def stage(c, slot, has_next=True):
        if has_next:
            nxt = logits(jnp.minimum(c + 1, i))
            for h in range(A_HEADS):
                lg_refs[1 - slot][h] = nxt[h]
        accumulate(jnp.maximum(c - 1, 0), 1 - slot)
        kind = jnp.clip(c - i + 2, 0, 2)
        sel = skey_ref[pl.ds(pl.multiple_of(c * blk, blk), blk), :] >= thr
        for h in range(A_HEADS):
            lg = jnp.where(sel, lg_refs[slot][h] + bias_ref[kind, h], -jnp.inf)
            m_old = m_ref[h]
            m_new = jnp.maximum(m_old, jnp.max(lg, axis=0, keepdims=True))
            m_use = jnp.where(m_new == -jnp.inf, 0.0, m_new)
            alpha = jnp.exp2(m_old - m_use)
            m_ref[h] = m_new
            p_ref[slot, h] = jnp.exp2(lg - m_use).astype(BF16)
            al_ref[slot, h] = alpha

    def pair_body(j, carry):
        stage(2 * j, 0)
        stage(2 * j + 1, 1)
        return carry

    n_chunks = i + 1
    lax.fori_loop(0, n_chunks // 2, pair_body, 0)

    @pl.when(n_chunks % 2 == 1)
    def _():
        stage(i, 0, has_next=False)

    accumulate(i, i % 2)

    for h in range(A_HEADS):
        acc = acc_ref[h]
        o_ref[0, :, h * LANES:(h + 1) * LANES] = (acc[:HEAD_DIM] / acc[HEAD_DIM:HEAD_DIM + 1]).T.astype(BF16)


def _dsa(proj3, bias_tab, k_sel):
    bsz, seq, _ = proj3.shape
    blk = BLOCK
    return pl.pallas_call(
        functools.partial(_dsa_kernel, k_sel=k_sel, seq=seq),
        grid=(bsz, seq // blk),
        in_specs=[
            pl.BlockSpec((1, blk, A_WIDTH), lambda b, i: (b, i, 0)),
            pl.BlockSpec((1, blk, IDX_HEADS * IDX_DIM), lambda b, i: (b, i, 1)),
            pl.BlockSpec((1, blk, LANES), lambda b, i: (b, i, BLK_IW)),
            pl.BlockSpec((1, seq, LANES), lambda b, i: (b, 0, BLK_IK)),
            pl.BlockSpec((1, seq, LANES), lambda b, i: (b, 0, BLK_AK)),
            pl.BlockSpec((1, seq, LANES), lambda b, i: (b, 0, BLK_AV)),
            pl.BlockSpec(bias_tab.shape, lambda b, i: (0, 0, 0, 0)),
        ],
        out_specs=pl.BlockSpec((1, blk, A_WIDTH), lambda b, i: (b, i, 0)),
        out_shape=jax.ShapeDtypeStruct((bsz, seq, A_WIDTH), BF16),
        scratch_shapes=[
            pltpu.VMEM((seq, blk), I32),
            pltpu.VMEM((seq, blk), I16),
            pltpu.VMEM((seq, blk), I16),
            pltpu.VMEM((seq, LANES), BF16),
            pltpu.VMEM((seq, LANES), BF16),
            pltpu.VMEM((seq // blk, VT_ROWS, blk), BF16),
            pltpu.VMEM((A_HEADS, 1, blk), F32),
            pltpu.VMEM((A_HEADS, VT_ROWS, blk), F32),
            pltpu.VMEM((A_HEADS, blk, blk), F32),
            pltpu.VMEM((A_HEADS, blk, blk), F32),
            pltpu.VMEM((2, A_HEADS, blk, blk), BF16),
            pltpu.VMEM((2, A_HEADS, 1, blk), F32),
        ],
        compiler_params=_params(("parallel", "arbitrary")),
        name="dsa",
    )(proj3, proj3, proj3, proj3, proj3, proj3, bias_tab)


def _rms_rows(x, g):
    return x * lax.rsqrt(jnp.mean(x * x, axis=-1, keepdims=True) + RMS_EPS) * g


def _mla_proj_kernel(cq_ref, ckv_ref, kra_ref, krb_ref, cc_ref, ss_ref, qn_ref, kvn_ref, wq_ref, wkv_ref,
                     qt_ref, k_ref, vt_ref):
    tm = cq_ref.shape[0]
    cc = cc_ref[...]
    ss = ss_ref[...]
    qscale = (NOPE_DIM + ROPE_DIM) ** -0.5 * LOG2E
    cq = _rms_rows(cq_ref[...].astype(F32), qn_ref[0]).astype(BF16)
    q3 = jnp.dot(cq, wq_ref[0], preferred_element_type=F32)
    for h in range(B_HEADS):
        base = 3 * LANES * h
        rot = q3[:, base + LANES:base + 2 * LANES] * cc + q3[:, base + 2 * LANES:base + 3 * LANES] * ss
        qt_ref[h, 0:LANES, :] = (q3[:, base:base + LANES] * qscale).T.astype(BF16)
        qt_ref[h, LANES:MLA_QK, :] = (rot * qscale).T.astype(BF16)
    ckv = _rms_rows(ckv_ref[...].astype(F32), kvn_ref[0]).astype(BF16)
    kv = jnp.dot(ckv, wkv_ref[0], preferred_element_type=F32)
    krot = (kra_ref[...].astype(F32) * cc + krb_ref[...].astype(F32) * ss).astype(BF16)
    for h in range(B_HEADS):
        k_ref[:, MLA_QK * h:MLA_QK * h + LANES] = kv[:, h * LANES:(h + 1) * LANES].astype(BF16)
        k_ref[:, MLA_QK * h + LANES:MLA_QK * (h + 1)] = krot
        v_h = kv[:, B_HEADS * NOPE_DIM + h * V_DIM:B_HEADS * NOPE_DIM + (h + 1) * V_DIM]
        for c in range(tm // BLOCK):
            vt_ref[h, c, 0:V_DIM, :] = v_h[c * BLOCK:(c + 1) * BLOCK, :].T.astype(BF16)
            vt_ref[h, c, V_DIM:VT_ROWS, :] = jnp.ones((ONES_ROWS, BLOCK), BF16)


def _mla_proj(proj, cc, ss, qn, kvn, wq, wkv, layer, tm=512):
    n = proj.shape[0]
    per_layer = lambda i: (layer, 0, 0)
    return pl.pallas_call(
        _mla_proj_kernel,
        grid=(n // tm,),
        in_specs=[
            pl.BlockSpec((tm, Q_LORA), lambda i: (i, BLK512_CQ)),
            pl.BlockSpec((tm, KV_LORA), lambda i: (i, BLK512_CKV)),
            pl.BlockSpec((tm, LANES), lambda i: (i, BLK_KRA)),
            pl.BlockSpec((tm, LANES), lambda i: (i, BLK_KRB)),
            pl.BlockSpec((tm, LANES), lambda i: (i, 0)),
            pl.BlockSpec((tm, LANES), lambda i: (i, 0)),
            pl.BlockSpec((1, 1, Q_LORA), per_layer),
            pl.BlockSpec((1, 1, KV_LORA), per_layer),
            pl.BlockSpec((1,) + wq.shape[1:], per_layer),
            pl.BlockSpec((1,) + wkv.shape[1:], per_layer),
        ],
        out_specs=[
            pl.BlockSpec((B_HEADS, MLA_QK, tm), lambda i: (0, 0, i)),
            pl.BlockSpec((tm, B_HEADS * MLA_QK), lambda i: (i, 0)),
            pl.BlockSpec((B_HEADS, tm // BLOCK, VT_ROWS, BLOCK), lambda i: (0, i, 0, 0)),
        ],
        out_shape=[jax.ShapeDtypeStruct((B_HEADS, MLA_QK, n), BF16),
                   jax.ShapeDtypeStruct((n, B_HEADS * MLA_QK), BF16),
                   jax.ShapeDtypeStruct((B_HEADS, n // BLOCK, VT_ROWS, BLOCK), BF16)],
        compiler_params=_params(("parallel",)),
        name="mla_proj",
    )(proj, proj, proj, proj, cc, ss, qn, kvn, wq, wkv)


def _mla_attn_kernel(qt_ref, k_ref, vt_ref, o_ref, lg0_ref, lg1_ref, p_ref, al_ref, m_ref, acc_ref):
    lg_refs = (lg0_ref, lg1_ref)
    qi = pl.program_id(1)
    blk = BLOCK
    srow = lax.broadcasted_iota(I32, (blk, blk), 0)
    tcol = lax.broadcasted_iota(I32, (blk, blk), 1)
    causal_bias = jnp.where(srow <= tcol, 0.0, -jnp.inf).astype(F32)

    def logits(c):
        s0 = pl.multiple_of(c * blk, blk)
        return [jnp.dot(k_ref[0, pl.ds(s0, blk), h * MLA_QK:(h + 1) * MLA_QK], qt_ref[h],
                        preferred_element_type=F32) for h in range(B_HEADS)]

    def accumulate(c, slot):
        for h in range(B_HEADS):
            pv = jnp.dot(vt_ref[h, c], p_ref[slot, h], preferred_element_type=F32)
            acc_ref[h] = acc_ref[h] * al_ref[slot, h] + pv

    m_ref[...] = jnp.full_like(m_ref, -jnp.inf)
    acc_ref[...] = jnp.zeros_like(acc_ref)
    p_ref[1] = jnp.zeros_like(p_ref[1])
    al_ref[1] = jnp.ones_like(al_ref[1])
    first = logits(0)
    for h in range(B_HEADS):
        lg0_ref[h] = first[h]
    def stage(c, slot, has_next=True):
        if has_next:
            nxt = logits(jnp.minimum(c + 1, qi))
            for h in range(B_HEADS):
                lg_refs[1 - slot][h] = nxt[h]
        accumulate(jnp.maximum(c - 1, 0), 1 - slot)
        mask = jnp.where(c == qi, causal_bias, 0.0)
        for h in range(B_HEADS):
            lg = lg_refs[slot][h] + mask
            m_old = m_ref[h]
            m_new = jnp.maximum(m_old, jnp.max(lg, axis=0, keepdims=True))
            m_ref[h] = m_new
            p_ref[slot, h] = jnp.exp2(lg - m_new).astype(BF16)
            al_ref[slot, h] = jnp.exp2(m_old - m_new)

    def pair_body(j, carry):
        stage(2 * j, 0)
        stage(2 * j + 1, 1)
        return carry

    n_chunks = qi + 1
    lax.fori_loop(0, n_chunks // 2, pair_body, 0)

    @pl.when(n_chunks % 2 == 1)
    def _():
        stage(qi, 0, has_next=False)

    accumulate(qi, qi % 2)
    for h in range(B_HEADS):
        acc = acc_ref[h]
        o_ref[0, :, h * V_DIM:(h + 1) * V_DIM] = (acc[:V_DIM] / acc[V_DIM:V_DIM + 1]).T.astype(BF16)


def _mla_attn(qt, k3, vt):
    bsz, seq, _ = k3.shape
    blk = BLOCK
    nq = seq // blk
    return pl.pallas_call(
        _mla_attn_kernel,
        grid=(bsz, nq),
        in_specs=[
            pl.BlockSpec((B_HEADS, MLA_QK, blk), lambda b, i: (0, 0, b * nq + i)),
            pl.BlockSpec((1, seq, B_HEADS * MLA_QK), lambda b, i: (b, 0, 0)),
            pl.BlockSpec((B_HEADS, nq, VT_ROWS, blk), lambda b, i: (0, b, 0, 0)),
        ],
        out_specs=pl.BlockSpec((1, blk, B_WIDTH), lambda b, i: (b, i, 0)),
        out_shape=jax.ShapeDtypeStruct((bsz, seq, B_WIDTH), BF16),
        scratch_shapes=[
            pltpu.VMEM((B_HEADS, blk, blk), F32),
            pltpu.VMEM((B_HEADS, blk, blk), F32),
            pltpu.VMEM((2, B_HEADS, blk, blk), BF16),
            pltpu.VMEM((2, B_HEADS, 1, blk), F32),
            pltpu.VMEM((B_HEADS, 1, blk), F32),
            pltpu.VMEM((B_HEADS, VT_ROWS, blk), F32),
        ],
        compiler_params=_params(("parallel", "arbitrary")),
        name="mla_attn",
    )(qt, k3, vt)


def _mem_attn_kernel(q_ref, kv_ref, o_ref):
    scale = HEAD_DIM ** -0.5
    for h in range(C_HEADS):
        q = q_ref[0, :, h * LANES:(h + 1) * LANES]
        k = kv_ref[0, :, h * LANES:(h + 1) * LANES]
        v = kv_ref[0, :, C_WIDTH + h * LANES:C_WIDTH + (h + 1) * LANES]
        s = lax.dot_general(q, k, (((1,), (1,)), ((), ())), preferred_element_type=F32) * scale
        p = jnp.exp(s - jnp.max(s, axis=-1, keepdims=True))
        l = jnp.sum(p, axis=-1, keepdims=True)
        o = jnp.dot(p.astype(BF16), v, preferred_element_type=F32) / l
        o_ref[0, :, h * LANES:(h + 1) * LANES] = o.astype(BF16)


def _mem_attn(proj3, mkv3, tq=512):
    bsz, seq, _ = proj3.shape
    mlen = mkv3.shape[1]
    return pl.pallas_call(
        _mem_attn_kernel,
        grid=(bsz, seq // tq),
        in_specs=[pl.BlockSpec((1, tq, C_WIDTH), lambda b, i: (b, i, BLK512_C)),
                  pl.BlockSpec((1, mlen, 2 * C_WIDTH), lambda b, i: (b, 0, 0))],
        out_specs=pl.BlockSpec((1, tq, C_WIDTH), lambda b, i: (b, i, 0)),
        out_shape=jax.ShapeDtypeStruct((bsz, seq, C_WIDTH), BF16),
        compiler_params=_params(("parallel", "arbitrary")),
        name="mem_attn",
    )(proj3, mkv3)


def _mix_out_kernel(oa_ref, ob_ref, oc_ref, ga_ref, gb_ref, gc_ref, x_ref, wb_ref, wo_ref, g_ref, b_ref,
                    o_ref, *, alpha):
    def gated(o_blk, gate_ref, r0, r1):
        y = jnp.dot(o_blk, wb_ref[0, r0:r1, :], preferred_element_type=F32)
        return y * (1.0 / (1.0 + jnp.exp(-gate_ref[...].astype(F32))))

    y = gated(oa_ref[...], ga_ref, 0, A_WIDTH)
    y = y + gated(ob_ref[...], gb_ref, A_WIDTH, A_WIDTH + B_WIDTH)
    y = y + gated(oc_ref[...], gc_ref, A_WIDTH + B_WIDTH, A_WIDTH + B_WIDTH + C_WIDTH)
    mix = jnp.dot(y.astype(BF16), wo_ref[0], preferred_element_type=F32)
    o_ref[...] = _layer_norm_rows(alpha * x_ref[...] + mix, g_ref[0], b_ref[0])


def _mix_out(oa, ob, oc, proj, x, wb, wo, ln_g, ln_b, layer, alpha, tm=256):
    n = x.shape[0]
    per_layer = lambda i: (layer, 0, 0)
    ln_row = lambda i: (3 * layer + 1, 0, 0)
    return pl.pallas_call(
        functools.partial(_mix_out_kernel, alpha=alpha),
        grid=(n // tm,),
        in_specs=[
            pl.BlockSpec((tm, A_WIDTH), lambda i: (i, 0)),
            pl.BlockSpec((tm, B_WIDTH), lambda i: (i, 0)),
            pl.BlockSpec((tm, C_WIDTH), lambda i: (i, 0)),
            pl.BlockSpec((tm, D_MODEL), lambda i: (i, BLK2048_GATE0)),
            pl.BlockSpec((tm, D_MODEL), lambda i: (i, BLK2048_GATE0 + 1)),
            pl.BlockSpec((tm, D_MODEL), lambda i: (i, BLK2048_GATE0 + 2)),
            pl.BlockSpec((tm, D_MODEL), lambda i: (i, 0)),
            pl.BlockSpec((1,) + wb.shape[1:], per_layer),
            pl.BlockSpec((1,) + wo.shape[1:], per_layer),
            pl.BlockSpec((1, 1, D_MODEL), ln_row),
            pl.BlockSpec((1, 1, D_MODEL), ln_row),
        ],
        out_specs=pl.BlockSpec((tm, D_MODEL), lambda i: (i, 0)),
        out_shape=jax.ShapeDtypeStruct((n, D_MODEL), F32),
        compiler_params=_params(("parallel",)),
        name="mix_out",
    )(oa, ob, oc, proj, proj, proj, x, wb, wo, ln_g, ln_b)


_IN_OFFS = tuple(sum(IN_SIZES[:k]) for k in range(len(IN_SIZES) + 1))


def _pack_w_in_t(w):
    wt = jnp.swapaxes(w, 1, 2)
    o = _IN_OFFS
    seg = lambda k: wt[:, o[k]:o[k + 1]]
    zeros = lambda r: jnp.zeros((w.shape[0], r, w.shape[1]), w.dtype)
    ik, iw, kr = seg(4), seg(5), seg(8)
    half = ROPE_DIM // 2
    packed = jnp.concatenate([
        wt[:, :o[4]],
        ik, ik,
        iw, zeros(LANES - IDX_HEADS),
        kr, zeros(LANES - ROPE_DIM),
        kr[:, half:], kr[:, :half], zeros(LANES - ROPE_DIM),
        seg(6), seg(7), seg(9), seg(10),
    ], axis=1)
    assert packed.shape[1] == PROJ_COLS
    return packed.astype(BF16)


def _pack_w_uq(w):
    half = ROPE_DIM // 2
    zeros = jnp.zeros(w.shape[:-1] + (LANES - ROPE_DIM,), w.dtype)
    cols = []
    for h in range(B_HEADS):
        base = h * (NOPE_DIM + ROPE_DIM)
        r = w[..., base + NOPE_DIM:base + NOPE_DIM + ROPE_DIM]
        cols += [w[..., base:base + NOPE_DIM], r, zeros, r[..., half:], r[..., :half], zeros]
    return jnp.concatenate(cols, axis=-1).astype(BF16)


def _pack_w_ukv(w):
    step = NOPE_DIM + V_DIM
    ks = [w[..., h * step:h * step + NOPE_DIM] for h in range(B_HEADS)]
    vs = [w[..., h * step + NOPE_DIM:(h + 1) * step] for h in range(B_HEADS)]
    return jnp.concatenate(ks + vs, axis=-1).astype(BF16)


def _rel_bucket(dist):
    n = jnp.maximum(dist, 0)
    max_exact = REL_BUCKETS // 2
    nf = jnp.maximum(n, 1).astype(F32)
    large = max_exact + (jnp.log(nf / max_exact) / math.log(REL_MAX_DIST / max_exact)
                         * (REL_BUCKETS - max_exact)).astype(I32)
    large = jnp.minimum(large, REL_BUCKETS - 1)
    return jnp.where(n < max_exact, n, large)


def _bias_tables(rel_bias):
    assert BLOCK + 1 >= REL_MAX_DIST
    s = jnp.arange(BLOCK)[:, None]
    t = jnp.arange(BLOCK)[None, :]
    tiles = [jnp.zeros((A_HEADS, BLOCK, BLOCK), F32)]
    for block_gap in (1, 0):
        dist = t - s + block_gap * BLOCK
        onehot = (_rel_bucket(dist)[:, :, None] == jnp.arange(REL_BUCKETS)).astype(F32)
        looked_up = jnp.einsum("stb,bh->sth", onehot, rel_bias, precision=lax.Precision.HIGHEST)
        tile = (looked_up - rel_bias[REL_BUCKETS - 1]) * LOG2E
        tile = jnp.where((dist >= 0)[:, :, None], tile, -jnp.inf)
        tiles.append(jnp.transpose(tile, (2, 0, 1)))
    return jnp.stack(tiles).astype(F32)


def kernel(x, mem, positions, rel_bias, ln_g, ln_b, ffn1_up, ffn1_down, w_in, q_norm, kv_norm, w_uq, w_ukv,
           w_mem_kv, w_branch, w_out, ffn2_up, ffn2_down):
    bsz, seq, d = x.shape
    depth = ffn1_up.shape[0]
    n = bsz * seq
    alpha = (2 * depth) ** 0.25
    k_sel = min(TOPK_MAX, seq // 4)
    assert d == D_MODEL and seq % (2 * BLOCK) == 0

    inv_freq = ROPE_THETA ** (-jnp.arange(0, ROPE_DIM, 2, dtype=F32) / ROPE_DIM)
    ang = positions.astype(F32)[..., None] * inv_freq
    cos, sin = jnp.cos(ang).reshape(n, -1), jnp.sin(ang).reshape(n, -1)
    pad = jnp.zeros((n, LANES - ROPE_DIM), F32)
    cc = jnp.concatenate([cos, cos, pad], axis=1)
    ss = jnp.concatenate([-sin, sin, pad], axis=1)
    bias_tab = _bias_tables(rel_bias)

    xf = x.reshape(n, d)
    memf = mem.reshape(bsz * mem.shape[1], d)
    ln_g3 = ln_g.reshape(depth * 3, 1, d)
    ln_b3 = ln_b.reshape(depth * 3, 1, d)
    qn3 = q_norm.reshape(depth, 1, Q_LORA)
    kvn3 = kv_norm.reshape(depth, 1, KV_LORA)
    w_in_p = _pack_w_in_t(w_in)
    w_uq_p = _pack_w_uq(w_uq.astype(BF16))
    w_ukv_p = _pack_w_ukv(w_ukv.astype(BF16))
    w_branch_b = w_branch.astype(BF16)
    w_out_b = w_out.astype(BF16)

    for l in range(depth):
        xf = _ffn(xf, ffn1_up, ffn1_down, ln_g3, ln_b3, l, 0, alpha)
        proj = _proj(xf, w_in_p, l, 1024, 1024, "in_proj", w_is_transposed=True)
        proj3 = proj.reshape(bsz, seq, PROJ_COLS)
        o_a = _dsa(proj3, bias_tab, k_sel).reshape(n, A_WIDTH)
        qt, k, vt = _mla_proj(proj, cc, ss, qn3, kvn3, w_uq_p, w_ukv_p, l)
        o_b = _mla_attn(qt, k.reshape(bsz, seq, -1), vt).reshape(n, B_WIDTH)
        mkv = _proj(memf, w_mem_kv, l, 512, 1024, "mem_proj")
        o_c = _mem_attn(proj3, mkv.reshape(bsz, mem.shape[1], 2 * C_WIDTH)).reshape(n, C_WIDTH)
        xf = _mix_out(o_a, o_b, o_c, proj, xf, w_branch_b, w_out_b, ln_g3, ln_b3, l, alpha)
        xf = _ffn(xf, ffn2_up, ffn2_down, ln_g3, ln_b3, l, 2, alpha)
    return xf.reshape(bsz, seq, d)
```

```python
import functools
import math

import jax
import jax.numpy as jnp
from jax import lax
from jax.experimental import pallas as pl
from jax.experimental.pallas import tpu as pltpu

F32 = jnp.float32
BF16 = jnp.bfloat16
I32 = jnp.int32
I16 = jnp.int16

D_MODEL = 2048
D_FF = 5632
HEAD_DIM = 128
A_HEADS = 6
IDX_HEADS = 16
IDX_DIM = 64
TOPK_MAX = 256
B_HEADS = 6
Q_LORA = 512
KV_LORA = 512
NOPE_DIM = 128
ROPE_DIM = 64
V_DIM = 128
ROPE_THETA = 10000.0
C_HEADS = 4
REL_BUCKETS = 32
REL_MAX_DIST = 128
LN_EPS = 1e-5
RMS_EPS = 1e-6
A_WIDTH = A_HEADS * HEAD_DIM
B_WIDTH = B_HEADS * V_DIM
C_WIDTH = C_HEADS * HEAD_DIM
IN_SIZES = (A_WIDTH, HEAD_DIM, HEAD_DIM, IDX_HEADS * IDX_DIM, IDX_DIM, IDX_HEADS,
            Q_LORA, KV_LORA, ROPE_DIM, C_WIDTH, 3 * D_MODEL)

LANES = 128
SUBLANES = 8
BLOCK = 256
MLA_QK = 2 * LANES
ONES_ROWS = 16
VT_ROWS = HEAD_DIM + ONES_ROWS
INT_MIN = -2 ** 31
HALF_RANGE = 2 ** 15
LOG2E = 1.4426950408889634

PROJ_COLS = 10240
BLK_AK, BLK_AV = 6, 7
BLK_IK, BLK_IW, BLK_KRA, BLK_KRB = 16, 17, 18, 19
BLK512_CQ, BLK512_CKV, BLK512_C = 5, 6, 7
BLK2048_GATE0 = 2

VMEM_LIMIT = 56 * 1024 * 1024
FFN_VMEM_LIMIT = 60 * 1024 * 1024


def _params(sem, vmem=VMEM_LIMIT):
    return pltpu.CompilerParams(dimension_semantics=sem, vmem_limit_bytes=vmem)


def _layer_norm_rows(y, g, b, scale=1.0):
    mu = jnp.mean(y, axis=-1, keepdims=True)
    d = y - mu
    var = jnp.mean(d * d, axis=-1, keepdims=True)
    return d * (scale * lax.rsqrt(scale * scale * var + LN_EPS)) * g + b


def _ffn_kernel(x_ref, wg_ref, wu_ref, wd_ref, g_ref, b_ref, o_ref, xb_ref, *, alpha):
    j = pl.program_id(1)

    @pl.when(j == 0)
    def _():
        x = x_ref[...]
        xb_ref[...] = x.astype(BF16)
        o_ref[...] = (2.0 * alpha) * x

    xb = xb_ref[...]
    gate = jnp.dot(xb, wg_ref[0].astype(BF16), preferred_element_type=F32)
    up = jnp.dot(xb, wu_ref[0].astype(BF16), preferred_element_type=F32)
    h = (gate * (1.0 / (1.0 + jnp.exp(-gate))) * up).astype(BF16)
    o_ref[...] += jnp.dot(h, wd_ref[0].astype(BF16), preferred_element_type=F32)

    @pl.when(j == pl.num_programs(1) - 1)
    def _():
        o_ref[...] = _layer_norm_rows(o_ref[...], g_ref[0], b_ref[0], scale=0.5)


def _ffn(x, w_up, w_down, ln_g, ln_b, layer, which, alpha, tm=1024, tf=256):
    n = x.shape[0]
    tm = min(tm, n)
    nf = D_FF // tf
    return pl.pallas_call(
        functools.partial(_ffn_kernel, alpha=alpha),
        grid=(n // tm, nf),
        in_specs=[
            pl.BlockSpec((tm, D_MODEL), lambda i, j: (i, 0)),
            pl.BlockSpec((1, D_MODEL, tf), lambda i, j: (layer, 0, j)),
            pl.BlockSpec((1, D_MODEL, tf), lambda i, j: (layer, 0, j + nf)),
            pl.BlockSpec((1, tf, D_MODEL), lambda i, j: (layer, j, 0)),
            pl.BlockSpec((1, 1, D_MODEL), lambda i, j: (3 * layer + which, 0, 0)),
            pl.BlockSpec((1, 1, D_MODEL), lambda i, j: (3 * layer + which, 0, 0)),
        ],
        out_specs=pl.BlockSpec((tm, D_MODEL), lambda i, j: (i, 0)),
        out_shape=jax.ShapeDtypeStruct((n, D_MODEL), F32),
        scratch_shapes=[pltpu.VMEM((tm, D_MODEL), BF16)],
        compiler_params=_params(("parallel", "arbitrary"), FFN_VMEM_LIMIT),
        name="ffn",
    )(x, w_up, w_up, w_down, ln_g, ln_b)


def _proj_kernel(x_ref, w_ref, o_ref, xb_ref, *, w_is_transposed):
    @pl.when(pl.program_id(1) == 0)
    def _():
        xb_ref[...] = x_ref[...].astype(BF16)

    contract_w = 1 if w_is_transposed else 0
    o_ref[...] = lax.dot_general(xb_ref[...], w_ref[0].astype(BF16), (((1,), (contract_w,)), ((), ())),
                                 preferred_element_type=F32).astype(o_ref.dtype)


def _proj(x, w, layer, tm, tn, name, w_is_transposed=False):
    n, k = x.shape
    tm = min(tm, n)
    cols = w.shape[1] if w_is_transposed else w.shape[2]
    w_spec = (pl.BlockSpec((1, tn, k), lambda i, j: (layer, j, 0)) if w_is_transposed
              else pl.BlockSpec((1, k, tn), lambda i, j: (layer, 0, j)))
    return pl.pallas_call(
        functools.partial(_proj_kernel, w_is_transposed=w_is_transposed),
        grid=(n // tm, cols // tn),
        in_specs=[pl.BlockSpec((tm, k), lambda i, j: (i, 0)), w_spec],
        out_specs=pl.BlockSpec((tm, tn), lambda i, j: (i, j)),
        out_shape=jax.ShapeDtypeStruct((n, cols), BF16),
        scratch_shapes=[pltpu.VMEM((tm, k), BF16)],
        compiler_params=_params(("parallel", "arbitrary")),
        name=name,
    )(x, w)


def _dsa_kernel(aq_ref, iq_ref, iw_ref, ik_ref, ak_ref, av_ref, bias_ref, o_ref,
                skey_ref, hi_ref, lo_ref, ika_ref, ikb_ref, vat_ref, m_ref, acc_ref,
                lg0_ref, lg1_ref, p_ref, al_ref, *, k_sel, seq):
    i = pl.program_id(1)
    blk = BLOCK
    lg_refs = (lg0_ref, lg1_ref)

    @pl.when(i == 0)
    def _():
        ik = ik_ref[0]
        lane = lax.broadcasted_iota(I32, ik.shape, 1)
        zero = jnp.zeros_like(ik)
        ika_ref[...] = jnp.where(lane < IDX_DIM, ik, zero)
        ikb_ref[...] = jnp.where(lane >= IDX_DIM, ik, zero)
        for c in range(seq // blk):
            vat_ref[c, 0:HEAD_DIM, :] = av_ref[0, c * blk:(c + 1) * blk, :].astype(F32).T.astype(BF16)
            vat_ref[c, HEAD_DIM:VT_ROWS, :] = jnp.ones((ONES_ROWS, blk), BF16)

    iqt = iq_ref[0].astype(F32).T.astype(BF16)
    wt = iw_ref[0].astype(F32).T * (IDX_HEADS * IDX_DIM) ** -0.5

    srow = lax.broadcasted_iota(I32, (blk, blk), 0)
    tcol = lax.broadcasted_iota(I32, (blk, blk), 1)

    def score_chunk(c):
        s0 = pl.multiple_of(c * blk, blk)
        ka = ika_ref[pl.ds(s0, blk), :]
        kb = ikb_ref[pl.ds(s0, blk), :]
        sc = jnp.zeros((blk, blk), F32)
        for j in range(IDX_HEADS // 2):
            rhs = iqt[j * LANES:(j + 1) * LANES, :]
            sc = sc + wt[2 * j:2 * j + 1, :] * jnp.maximum(jnp.dot(ka, rhs, preferred_element_type=F32), 0.0)
            sc = sc + wt[2 * j + 1:2 * j + 2, :] * jnp.maximum(jnp.dot(kb, rhs, preferred_element_type=F32), 0.0)
        bits = pltpu.bitcast(sc, I32)
        key = bits ^ ((bits >> 31) & jnp.int32(0x7FFFFFFF))
        causal = srow <= tcol + jnp.where(c < i, jnp.int32(blk), jnp.int32(0))
        key = jnp.where(causal, key, jnp.int32(INT_MIN))
        skey_ref[pl.ds(s0, blk), :] = key
        hi_ref[pl.ds(s0, blk), :] = (key >> 16).astype(I16)
        lo_ref[pl.ds(s0, blk), :] = ((key & jnp.int32(0xFFFF)) - HALF_RANGE).astype(I16)

    def score_pair(j, carry):
        score_chunk(2 * j)
        score_chunk(2 * j + 1)
        return carry

    lax.fori_loop(0, (i + 1) // 2, score_pair, 0)

    @pl.when((i + 1) % 2 == 1)
    def _():
        score_chunk(i)

    n_acc = 4
    rows16 = 2 * SUBLANES

    def count_at_least(ref, trial):
        trial16 = jnp.broadcast_to(trial.astype(I16), (rows16, blk))

        def cnt_body(c, accs):
            vals = ref[pl.ds(pl.multiple_of(c * blk, blk), blk), :]
            accs = list(accs)
            for r in range(blk // rows16):
                rows = vals[r * rows16:(r + 1) * rows16, :]
                accs[r % n_acc] = jnp.where(rows >= trial16, accs[r % n_acc] + jnp.int16(1), accs[r % n_acc])
            return tuple(accs)

        zero = jnp.zeros((rows16, blk), I16)
        accs = lax.fori_loop(0, i + 1, cnt_body, (zero,) * n_acc)
        total = (accs[0] + accs[1]) + (accs[2] + accs[3])
        return jnp.sum(total.astype(I32), axis=0, keepdims=True)

    def largest_with_count(ref, need):
        def bit_body(it, u):
            trial_u = u | jnp.left_shift(jnp.int32(1), jnp.int32(15) - it)
            return jnp.where(count_at_least(ref, trial_u - HALF_RANGE) >= need, trial_u, u)

        return lax.fori_loop(0, 16, bit_body, jnp.zeros((1, blk), I32))

    hi_thr = largest_with_count(hi_ref, k_sel) - HALF_RANGE
    above = jnp.where(hi_thr == HALF_RANGE - 1, 0, count_at_least(hi_ref, jnp.minimum(hi_thr + 1, HALF_RANGE - 1)))
    hi_thr16 = jnp.broadcast_to(hi_thr.astype(I16), (blk, blk))

    def keep_candidates(c, carry):
        rows = pl.ds(pl.multiple_of(c * blk, blk), blk)
        lo_ref[rows, :] = jnp.where(hi_ref[rows, :] == hi_thr16, lo_ref[rows, :], jnp.int16(-HALF_RANGE))
        return carry

    lax.fori_loop(0, i + 1, keep_candidates, 0)
    lo_thr = largest_with_count(lo_ref, k_sel - above)
    thr = hi_thr * (2 * HALF_RANGE) + lo_thr

    def count_keys_at_least(trial):
        trial8 = jnp.broadcast_to(trial, (SUBLANES, blk))

        def cnt_body(c, acc):
            keys = skey_ref[pl.ds(pl.multiple_of(c * blk, blk), blk), :]
            for r in range(blk // SUBLANES):
                acc = acc + jnp.where(keys[r * SUBLANES:(r + 1) * SUBLANES, :] >= trial8, 1, 0)
            return acc

        acc = lax.fori_loop(0, i + 1, cnt_body, jnp.zeros((SUBLANES, blk), I32))
        return jnp.sum(acc, axis=0, keepdims=True)

    live = thr != jnp.int32(INT_MIN)
    surplus = jnp.where(live, count_keys_at_least(thr) - k_sel, 0)

    @pl.when(jnp.max(surplus.astype(F32)) > 0.0)
    def _():
        above_thr = jnp.where(thr == jnp.int32(-INT_MIN - 1), 0,
                              count_keys_at_least(jnp.minimum(thr, jnp.int32(-INT_MIN - 2)) + 1))
        allowed = (k_sel - above_thr).astype(F32)
        prefix_ones = jnp.where(tcol <= srow, 1.0, 0.0).astype(BF16)

        def demote(c, seen):
            rows = pl.ds(pl.multiple_of(c * blk, blk), blk)
            keys = skey_ref[rows, :]
            tied = jnp.logical_and(keys == thr, live)
            rank = seen + jnp.dot(prefix_ones, jnp.where(tied, 1.0, 0.0).astype(BF16), preferred_element_type=F32)
            skey_ref[rows, :] = jnp.where(jnp.logical_and(tied, rank > allowed), jnp.int32(INT_MIN), keys)
            return rank[blk - 1:blk, :]

        lax.fori_loop(0, i + 1, demote, jnp.zeros((1, blk), F32))

    qt = (aq_ref[0].astype(F32) * (HEAD_DIM ** -0.5 * LOG2E)).T.astype(BF16)

    m_ref[...] = jnp.full_like(m_ref, -jnp.inf)
    acc_ref[...] = jnp.zeros_like(acc_ref)

    def logits(c):
        ka = ak_ref[0, pl.ds(pl.multiple_of(c * blk, blk), blk), :]
        return [jnp.dot(ka, qt[h * LANES:(h + 1) * LANES, :], preferred_element_type=F32) for h in range(A_HEADS)]

    def accumulate(c, slot):
        vt = vat_ref[c]
        for h in range(A_HEADS):
            pv = jnp.dot(vt, p_ref[slot, h], preferred_element_type=F32)
            acc_ref[h] = acc_ref[h] * al_ref[slot, h] + pv

    p_ref[1] = jnp.zeros_like(p_ref[1])
    al_ref[1] = jnp.ones_like(al_ref[1])
    first = logits(0)
    for h in range(A_HEADS):
        lg0_ref[h] = first[h]

    def stage(c, slot, has_next=True):
        if has_next:
            nxt = logits(jnp.minimum(c + 1, i))
            for h in range(A_HEADS):
                lg_refs[1 - slot][h] = nxt[h]
        accumulate(jnp.maximum(c - 1, 0), 1 - slot)
        kind = jnp.clip(c - i + 2, 0, 2)
        sel = skey_ref[pl.ds(pl.multiple_of(c * blk, blk), blk), :] >= thr
        for h in range(A_HEADS):
            lg = jnp.where(sel, lg_refs[slot][h] + bias_ref[kind, h], -jnp.inf)
            m_old = m_ref[h]
            m_new = jnp.maximum(m_old, jnp.max(lg, axis=0, keepdims=True))
            m_use = jnp.where(m_new == -jnp.inf, 0.0, m_new)
            alpha = jnp.exp2(m_old - m_use)
            m_ref[h] = m_new
            p_ref[slot, h] = jnp.exp2(lg - m_use).astype(BF16)
            al_ref[slot, h] = alpha

    def pair_body(j, carry):
        stage(2 * j, 0)
        stage(2 * j + 1, 1)
        return carry

    n_chunks = i + 1
    lax.fori_loop(0, n_chunks // 2, pair_body, 0)

    @pl.when(n_chunks % 2 == 1)
    def _():
        stage(i, 0, has_next=False)

    accumulate(i, i % 2)

    for h in range(A_HEADS):
        acc = acc_ref[h]
        o_ref[0, :, h * LANES:(h + 1) * LANES] = (acc[:HEAD_DIM] / acc[HEAD_DIM:HEAD_DIM + 1]).T.astype(BF16)


def _dsa(proj3, bias_tab, k_sel):
    bsz, seq, _ = proj3.shape
    blk = BLOCK
    return pl.pallas_call(
        functools.partial(_dsa_kernel, k_sel=k_sel, seq=seq),
        grid=(bsz, seq // blk),
        in_specs=[
            pl.BlockSpec((1, blk, A_WIDTH), lambda b, i: (b, i, 0)),
            pl.BlockSpec((1, blk, IDX_HEADS * IDX_DIM), lambda b, i: (b, i, 1)),
            pl.BlockSpec((1, blk, LANES), lambda b, i: (b, i, BLK_IW)),
            pl.BlockSpec((1, seq, LANES), lambda b, i: (b, 0, BLK_IK)),
            pl.BlockSpec((1, seq, LANES), lambda b, i: (b, 0, BLK_AK)),
            pl.BlockSpec((1, seq, LANES), lambda b, i: (b, 0, BLK_AV)),
            pl.BlockSpec(bias_tab.shape, lambda b, i: (0, 0, 0, 0)),
        ],
        out_specs=pl.BlockSpec((1, blk, A_WIDTH), lambda b, i: (b, i, 0)),
        out_shape=jax.ShapeDtypeStruct((bsz, seq, A_WIDTH), BF16),
        scratch_shapes=[
            pltpu.VMEM((seq, blk), I32),
            pltpu.VMEM((seq, blk), I16),
            pltpu.VMEM((seq, blk), I16),
            pltpu.VMEM((seq, LANES), BF16),
            pltpu.VMEM((seq, LANES), BF16),
            pltpu.VMEM((seq // blk, VT_ROWS, blk), BF16),
            pltpu.VMEM((A_HEADS, 1, blk), F32),
            pltpu.VMEM((A_HEADS, VT_ROWS, blk), F32),
            pltpu.VMEM((A_HEADS, blk, blk), F32),
            pltpu.VMEM((A_HEADS, blk, blk), F32),
            pltpu.VMEM((2, A_HEADS, blk, blk), BF16),
            pltpu.VMEM((2, A_HEADS, 1, blk), F32),
        ],
        compiler_params=_params(("parallel", "arbitrary")),
        name="dsa",
    )(proj3, proj3, proj3, proj3, proj3, proj3, bias_tab)


def _rms_rows(x, g):
    return x * lax.rsqrt(jnp.mean(x * x, axis=-1, keepdims=True) + RMS_EPS) * g


def _mla_proj_kernel(cq_ref, ckv_ref, kra_ref, krb_ref, cc_ref, ss_ref, qn_ref, kvn_ref, wq_ref, wkv_ref,
                     qt_ref, k_ref, vt_ref):
    tm = cq_ref.shape[0]
    cc = cc_ref[...]
    ss = ss_ref[...]
    qscale = (NOPE_DIM + ROPE_DIM) ** -0.5 * LOG2E
    cq = _rms_rows(cq_ref[...].astype(F32), qn_ref[0]).astype(BF16)
    q3 = jnp.dot(cq, wq_ref[0], preferred_element_type=F32)
    for h in range(B_HEADS):
        base = 3 * LANES * h
        rot = q3[:, base + LANES:base + 2 * LANES] * cc + q3[:, base + 2 * LANES:base + 3 * LANES] * ss
        qt_ref[h, 0:LANES, :] = (q3[:, base:base + LANES] * qscale).T.astype(BF16)
        qt_ref[h, LANES:MLA_QK, :] = (rot * qscale).T.astype(BF16)
    ckv = _rms_rows(ckv_ref[...].astype(F32), kvn_ref[0]).astype(BF16)
    kv = jnp.dot(ckv, wkv_ref[0], preferred_element_type=F32)
    krot = (kra_ref[...].astype(F32) * cc + krb_ref[...].astype(F32) * ss).astype(BF16)
    for h in range(B_HEADS):
        k_ref[:, MLA_QK * h:MLA_QK * h + LANES] = kv[:, h * LANES:(h + 1) * LANES].astype(BF16)
        k_ref[:, MLA_QK * h + LANES:MLA_QK * (h + 1)] = krot
        v_h = kv[:, B_HEADS * NOPE_DIM + h * V_DIM:B_HEADS * NOPE_DIM + (h + 1) * V_DIM]
        for c in range(tm // BLOCK):
            vt_ref[h, c, 0:V_DIM, :] = v_h[c * BLOCK:(c + 1) * BLOCK, :].T.astype(BF16)
            vt_ref[h, c, V_DIM:VT_ROWS, :] = jnp.ones((ONES_ROWS, BLOCK), BF16)


def _mla_proj(proj, cc, ss, qn, kvn, wq, wkv, layer, tm=512):
    n = proj.shape[0]
    per_layer = lambda i: (layer, 0, 0)
    return pl.pallas_call(
        _mla_proj_kernel,
        grid=(n // tm,),
        in_specs=[
            pl.BlockSpec((tm, Q_LORA), lambda i: (i, BLK512_CQ)),
            pl.BlockSpec((tm, KV_LORA), lambda i: (i, BLK512_CKV)),
            pl.BlockSpec((tm, LANES), lambda i: (i, BLK_KRA)),
            pl.BlockSpec((tm, LANES), lambda i: (i, BLK_KRB)),
            pl.BlockSpec((tm, LANES), lambda i: (i, 0)),
            pl.BlockSpec((tm, LANES), lambda i: (i, 0)),
            pl.BlockSpec((1, 1, Q_LORA), per_layer),
            pl.BlockSpec((1, 1, KV_LORA), per_layer),
            pl.BlockSpec((1,) + wq.shape[1:], per_layer),
            pl.BlockSpec((1,) + wkv.shape[1:], per_layer),
        ],
        out_specs=[
            pl.BlockSpec((B_HEADS, MLA_QK, tm), lambda i: (0, 0, i)),
            pl.BlockSpec((tm, B_HEADS * MLA_QK), lambda i: (i, 0)),
            pl.BlockSpec((B_HEADS, tm // BLOCK, VT_ROWS, BLOCK), lambda i: (0, i, 0, 0)),
        ],
        out_shape=[jax.ShapeDtypeStruct((B_HEADS, MLA_QK, n), BF16),
                   jax.ShapeDtypeStruct((n, B_HEADS * MLA_QK), BF16),
                   jax.ShapeDtypeStruct((B_HEADS, n // BLOCK, VT_ROWS, BLOCK), BF16)],
        compiler_params=_params(("parallel",)),
        name="mla_proj",
    )(proj, proj, proj, proj, cc, ss, qn, kvn, wq, wkv)


def _mla_attn_kernel(qt_ref, k_ref, vt_ref, o_ref, lg0_ref, lg1_ref, p_ref, al_ref, m_ref, acc_ref):
    lg_refs = (lg0_ref, lg1_ref)
    qi = pl.program_id(1)
    blk = BLOCK
    srow = lax.broadcasted_iota(I32, (blk, blk), 0)
    tcol = lax.broadcasted_iota(I32, (blk, blk), 1)
    causal_bias = jnp.where(srow <= tcol, 0.0, -jnp.inf).astype(F32)

    def logits(c):
        s0 = pl.multiple_of(c * blk, blk)
        return [jnp.dot(k_ref[0, pl.ds(s0, blk), h * MLA_QK:(h + 1) * MLA_QK], qt_ref[h],
                        preferred_element_type=F32) for h in range(B_HEADS)]

    def accumulate(c, slot):
        for h in range(B_HEADS):
            pv = jnp.dot(vt_ref[h, c], p_ref[slot, h], preferred_element_type=F32)
            acc_ref[h] = acc_ref[h] * al_ref[slot, h] + pv

    m_ref[...] = jnp.full_like(m_ref, -jnp.inf)
    acc_ref[...] = jnp.zeros_like(acc_ref)
    p_ref[1] = jnp.zeros_like(p_ref[1])
    al_ref[1] = jnp.ones_like(al_ref[1])
    first = logits(0)
    for h in range(B_HEADS):
        lg0_ref[h] = first[h]
    def stage(c, slot, has_next=True):
        if has_next:
            nxt = logits(jnp.minimum(c + 1, qi))
            for h in range(B_HEADS):
                lg_refs[1 - slot][h] = nxt[h]
        accumulate(jnp.maximum(c - 1, 0), 1 - slot)
        mask = jnp.where(c == qi, causal_bias, 0.0)
        for h in range(B_HEADS):
            lg = lg_refs[slot][h] + mask
            m_old = m_ref[h]
            m_new = jnp.maximum(m_old, jnp.max(lg, axis=0, keepdims=True))
            m_ref[h] = m_new
            p_ref[slot, h] = jnp.exp2(lg - m_new).astype(BF16)
            al_ref[slot, h] = jnp.exp2(m_old - m_new)

    def pair_body(j, carry):
        stage(2 * j, 0)
        stage(2 * j + 1, 1)
        return carry

    n_chunks = qi + 1
    lax.fori_loop(0, n_chunks // 2, pair_body, 0)

    @pl.when(n_chunks % 2 == 1)
    def _():
        stage(qi, 0, has_next=False)

    accumulate(qi, qi % 2)
    for h in range(B_HEADS):
        acc = acc_ref[h]
        o_ref[0, :, h * V_DIM:(h + 1) * V_DIM] = (acc[:V_DIM] / acc[V_DIM:V_DIM + 1]).T.astype(BF16)


def _mla_attn(qt, k3, vt):
    bsz, seq, _ = k3.shape
    blk = BLOCK
    nq = seq // blk
    return pl.pallas_call(
        _mla_attn_kernel,
        grid=(bsz, nq),
        in_specs=[
            pl.BlockSpec((B_HEADS, MLA_QK, blk), lambda b, i: (0, 0, b * nq + i)),
            pl.BlockSpec((1, seq, B_HEADS * MLA_QK), lambda b, i: (b, 0, 0)),
            pl.BlockSpec((B_HEADS, nq, VT_ROWS, blk), lambda b, i: (0, b, 0, 0)),
        ],
        out_specs=pl.BlockSpec((1, blk, B_WIDTH), lambda b, i: (b, i, 0)),
        out_shape=jax.ShapeDtypeStruct((bsz, seq, B_WIDTH), BF16),
        scratch_shapes=[
            pltpu.VMEM((B_HEADS, blk, blk), F32),
            pltpu.VMEM((B_HEADS, blk, blk), F32),
            pltpu.VMEM((2, B_HEADS, blk, blk), BF16),
            pltpu.VMEM((2, B_HEADS, 1, blk), F32),
            pltpu.VMEM((B_HEADS, 1, blk), F32),
            pltpu.VMEM((B_HEADS, VT_ROWS, blk), F32),
        ],
        compiler_params=_params(("parallel", "arbitrary")),
        name="mla_attn",
    )(qt, k3, vt)


def _mem_attn_kernel(q_ref, kv_ref, o_ref):
    scale = HEAD_DIM ** -0.5
    for h in range(C_HEADS):
        q = q_ref[0, :, h * LANES:(h + 1) * LANES]
        k = kv_ref[0, :, h * LANES:(h + 1) * LANES]
        v = kv_ref[0, :, C_WIDTH + h * LANES:C_WIDTH + (h + 1) * LANES]
        s = lax.dot_general(q, k, (((1,), (1,)), ((), ())), preferred_element_type=F32) * scale
        p = jnp.exp(s - jnp.max(s, axis=-1, keepdims=True))
        l = jnp.sum(p, axis=-1, keepdims=True)
        o = jnp.dot(p.astype(BF16), v, preferred_element_type=F32) / l
        o_ref[0, :, h * LANES:(h + 1) * LANES] = o.astype(BF16)


def _mem_attn(proj3, mkv3, tq=512):
    bsz, seq, _ = proj3.shape
    mlen = mkv3.shape[1]
    return pl.pallas_call(
        _mem_attn_kernel,
        grid=(bsz, seq // tq),
        in_specs=[pl.BlockSpec((1, tq, C_WIDTH), lambda b, i: (b, i, BLK512_C)),
                  pl.BlockSpec((1, mlen, 2 * C_WIDTH), lambda b, i: (b, 0, 0))],
        out_specs=pl.BlockSpec((1, tq, C_WIDTH), lambda b, i: (b, i, 0)),
        out_shape=jax.ShapeDtypeStruct((bsz, seq, C_WIDTH), BF16),
        compiler_params=_params(("parallel", "arbitrary")),
        name="mem_attn",
    )(proj3, mkv3)


def _mix_out_kernel(oa_ref, ob_ref, oc_ref, ga_ref, gb_ref, gc_ref, x_ref, wb_ref, wo_ref, g_ref, b_ref,
                    o_ref, *, alpha):
    def gated(o_blk, gate_ref, r0, r1):
        y = jnp.dot(o_blk, wb_ref[0, r0:r1, :], preferred_element_type=F32)
        return y * (1.0 / (1.0 + jnp.exp(-gate_ref[...].astype(F32))))

    y = gated(oa_ref[...], ga_ref, 0, A_WIDTH)
    y = y + gated(ob_ref[...], gb_ref, A_WIDTH, A_WIDTH + B_WIDTH)
    y = y + gated(oc_ref[...], gc_ref, A_WIDTH + B_WIDTH, A_WIDTH + B_WIDTH + C_WIDTH)
    mix = jnp.dot(y.astype(BF16), wo_ref[0], preferred_element_type=F32)
    o_ref[...] = _layer_norm_rows(alpha * x_ref[...] + mix, g_ref[0], b_ref[0])


def _mix_out(oa, ob, oc, proj, x, wb, wo, ln_g, ln_b, layer, alpha, tm=256):
    n = x.shape[0]
    per_layer = lambda i: (layer, 0, 0)
    ln_row = lambda i: (3 * layer + 1, 0, 0)
    return pl.pallas_call(
        functools.partial(_mix_out_kernel, alpha=alpha),
        grid=(n // tm,),
        in_specs=[
            pl.BlockSpec((tm, A_WIDTH), lambda i: (i, 0)),
            pl.BlockSpec((tm, B_WIDTH), lambda i: (i, 0)),
            pl.BlockSpec((tm, C_WIDTH), lambda i: (i, 0)),
            pl.BlockSpec((tm, D_MODEL), lambda i: (i, BLK2048_GATE0)),
            pl.BlockSpec((tm, D_MODEL), lambda i: (i, BLK2048_GATE0 + 1)),
            pl.BlockSpec((tm, D_MODEL), lambda i: (i, BLK2048_GATE0 + 2)),
            pl.BlockSpec((tm, D_MODEL), lambda i: (i, 0)),
            pl.BlockSpec((1,) + wb.shape[1:], per_layer),
            pl.BlockSpec((1,) + wo.shape[1:], per_layer),
            pl.BlockSpec((1, 1, D_MODEL), ln_row),
            pl.BlockSpec((1, 1, D_MODEL), ln_row),
        ],
        out_specs=pl.BlockSpec((tm, D_MODEL), lambda i: (i, 0)),
        out_shape=jax.ShapeDtypeStruct((n, D_MODEL), F32),
        compiler_params=_params(("parallel",)),
        name="mix_out",
    )(oa, ob, oc, proj, proj, proj, x, wb, wo, ln_g, ln_b)


_IN_OFFS = tuple(sum(IN_SIZES[:k]) for k in range(len(IN_SIZES) + 1))


def _pack_w_in_t(w):
    wt = jnp.swapaxes(w, 1, 2)
    o = _IN_OFFS
    seg = lambda k: wt[:, o[k]:o[k + 1]]
    zeros = lambda r: jnp.zeros((w.shape[0], r, w.shape[1]), w.dtype)
    ik, iw, kr = seg(4), seg(5), seg(8)
    half = ROPE_DIM // 2
    packed = jnp.concatenate([
        wt[:, :o[4]],
        ik, ik,
        iw, zeros(LANES - IDX_HEADS),
        kr, zeros(LANES - ROPE_DIM),
        kr[:, half:], kr[:, :half], zeros(LANES - ROPE_DIM),
        seg(6), seg(7), seg(9), seg(10),
    ], axis=1)
    assert packed.shape[1] == PROJ_COLS
    return packed.astype(BF16)


def _pack_w_uq(w):
    half = ROPE_DIM // 2
    zeros = jnp.zeros(w.shape[:-1] + (LANES - ROPE_DIM,), w.dtype)
    cols = []
    for h in range(B_HEADS):
        base = h * (NOPE_DIM + ROPE_DIM)
        r = w[..., base + NOPE_DIM:base + NOPE_DIM + ROPE_DIM]
        cols += [w[..., base:base + NOPE_DIM], r, zeros, r[..., half:], r[..., :half], zeros]
    return jnp.concatenate(cols, axis=-1).astype(BF16)


def _pack_w_ukv(w):
    step = NOPE_DIM + V_DIM
    ks = [w[..., h * step:h * step + NOPE_DIM] for h in range(B_HEADS)]
    vs = [w[..., h * step + NOPE_DIM:(h + 1) * step] for h in range(B_HEADS)]
    return jnp.concatenate(ks + vs, axis=-1).astype(BF16)


def _rel_bucket(dist):
    n = jnp.maximum(dist, 0)
    max_exact = REL_BUCKETS // 2
    nf = jnp.maximum(n, 1).astype(F32)
    large = max_exact + (jnp.log(nf / max_exact) / math.log(REL_MAX_DIST / max_exact)
                         * (REL_BUCKETS - max_exact)).astype(I32)
    large = jnp.minimum(large, REL_BUCKETS - 1)
    return jnp.where(n < max_exact, n, large)


def _bias_tables(rel_bias):
    assert BLOCK + 1 >= REL_MAX_DIST
    s = jnp.arange(BLOCK)[:, None]
    t = jnp.arange(BLOCK)[None, :]
    tiles = [jnp.zeros((A_HEADS, BLOCK, BLOCK), F32)]
    for block_gap in (1, 0):
        dist = t - s + block_gap * BLOCK
        onehot = (_rel_bucket(dist)[:, :, None] == jnp.arange(REL_BUCKETS)).astype(F32)
        looked_up = jnp.einsum("stb,bh->sth", onehot, rel_bias, precision=lax.Precision.HIGHEST)
        tile = (looked_up - rel_bias[REL_BUCKETS - 1]) * LOG2E
        tile = jnp.where((dist >= 0)[:, :, None], tile, -jnp.inf)
        tiles.append(jnp.transpose(tile, (2, 0, 1)))
    return jnp.stack(tiles).astype(F32)


def kernel(x, mem, positions, rel_bias, ln_g, ln_b, ffn1_up, ffn1_down, w_in, q_norm, kv_norm, w_uq, w_ukv,
           w_mem_kv, w_branch, w_out, ffn2_up, ffn2_down):
    bsz, seq, d = x.shape
    depth = ffn1_up.shape[0]
    n = bsz * seq
    alpha = (2 * depth) ** 0.25
    k_sel = min(TOPK_MAX, seq // 4)
    assert d == D_MODEL and seq % (2 * BLOCK) == 0

    inv_freq = ROPE_THETA ** (-jnp.arange(0, ROPE_DIM, 2, dtype=F32) / ROPE_DIM)
    ang = positions.astype(F32)[..., None] * inv_freq
    cos, sin = jnp.cos(ang).reshape(n, -1), jnp.sin(ang).reshape(n, -1)
    pad = jnp.zeros((n, LANES - ROPE_DIM), F32)
    cc = jnp.concatenate([cos, cos, pad], axis=1)
    ss = jnp.concatenate([-sin, sin, pad], axis=1)
    bias_tab = _bias_tables(rel_bias)

    xf = x.reshape(n, d)
    memf = mem.reshape(bsz * mem.shape[1], d)
    ln_g3 = ln_g.reshape(depth * 3, 1, d)
    ln_b3 = ln_b.reshape(depth * 3, 1, d)
    qn3 = q_norm.reshape(depth, 1, Q_LORA)
    kvn3 = kv_norm.reshape(depth, 1, KV_LORA)
    w_in_p = _pack_w_in_t(w_in)
    w_uq_p = _pack_w_uq(w_uq.astype(BF16))
    w_ukv_p = _pack_w_ukv(w_ukv.astype(BF16))
    w_branch_b = w_branch.astype(BF16)
    w_out_b = w_out.astype(BF16)

    for l in range(depth):
        xf = _ffn(xf, ffn1_up, ffn1_down, ln_g3, ln_b3, l, 0, alpha)
        proj = _proj(xf, w_in_p, l, 1024, 1024, "in_proj", w_is_transposed=True)
        proj3 = proj.reshape(bsz, seq, PROJ_COLS)
        o_a = _dsa(proj3, bias_tab, k_sel).reshape(n, A_WIDTH)
        qt, k, vt = _mla_proj(proj, cc, ss, qn3, kvn3, w_uq_p, w_ukv_p, l)
        o_b = _mla_attn(qt, k.reshape(bsz, seq, -1), vt).reshape(n, B_WIDTH)
        mkv = _proj(memf, w_mem_kv, l, 512, 1024, "mem_proj")
        o_c = _mem_attn(proj3, mkv.reshape(bsz, mem.shape[1], 2 * C_WIDTH)).reshape(n, C_WIDTH)
        xf = _mix_out(o_a, o_b, o_c, proj, xf, w_branch_b, w_out_b, ln_g3, ln_b3, l, alpha)
        xf = _ffn(xf, ffn2_up, ffn2_down, ln_g3, ln_b3, l, 2, alpha)
    return xf.reshape(bsz, seq, d)
```

```python
import functools
import math

import jax
import jax.numpy as jnp
from jax import lax
from jax.experimental import pallas as pl
from jax.experimental.pallas import tpu as pltpu

F32 = jnp.float32
BF16 = jnp.bfloat16
I32 = jnp.int32
I16 = jnp.int16

D_MODEL = 2048
D_FF = 5632
HEAD_DIM = 128
A_HEADS = 6
IDX_HEADS = 16
IDX_DIM = 64
TOPK_MAX = 256
B_HEADS = 6
Q_LORA = 512
KV_LORA = 512
NOPE_DIM = 128
ROPE_DIM = 64
V_DIM = 128
ROPE_THETA = 10000.0
C_HEADS = 4
REL_BUCKETS = 32
REL_MAX_DIST = 128
LN_EPS = 1e-5
RMS_EPS = 1e-6
A_WIDTH = A_HEADS * HEAD_DIM
B_WIDTH = B_HEADS * V_DIM
C_WIDTH = C_HEADS * HEAD_DIM
IN_SIZES = (A_WIDTH, HEAD_DIM, HEAD_DIM, IDX_HEADS * IDX_DIM, IDX_DIM, IDX_HEADS,
            Q_LORA, KV_LORA, ROPE_DIM, C_WIDTH, 3 * D_MODEL)

LANES = 128
SUBLANES = 8
BLOCK = 256
MLA_QK = 2 * LANES
ONES_ROWS = 16
VT_ROWS = HEAD_DIM + ONES_ROWS
INT_MIN = -2 ** 31
HALF_RANGE = 2 ** 15
LOG2E = 1.4426950408889634

PROJ_COLS = 10240
BLK_AK, BLK_AV = 6, 7
BLK_IK, BLK_IW, BLK_KRA, BLK_KRB = 16, 17, 18, 19
BLK512_CQ, BLK512_CKV, BLK512_C = 5, 6, 7
BLK2048_GATE0 = 2

VMEM_LIMIT = 56 * 1024 * 1024
FFN_VMEM_LIMIT = 60 * 1024 * 1024


def _params(sem, vmem=VMEM_LIMIT):
    return pltpu.CompilerParams(dimension_semantics=sem, vmem_limit_bytes=vmem)


def _layer_norm_rows(y, g, b, scale=1.0):
    mu = jnp.mean(y, axis=-1, keepdims=True)
    d = y - mu
    var = jnp.mean(d * d, axis=-1, keepdims=True)
    return d * (scale * lax.rsqrt(scale * scale * var + LN_EPS)) * g + b


def _ffn_kernel(x_ref, wg_ref, wu_ref, wd_ref, g_ref, b_ref, o_ref, xb_ref, *, alpha):
    j = pl.program_id(1)

    @pl.when(j == 0)
    def _():
        x = x_ref[...]
        xb_ref[...] = x.astype(BF16)
        o_ref[...] = (2.0 * alpha) * x

    xb = xb_ref[...]
    gate = jnp.dot(xb, wg_ref[0].astype(BF16), preferred_element_type=F32)
    up = jnp.dot(xb, wu_ref[0].astype(BF16), preferred_element_type=F32)
    h = (gate * (1.0 / (1.0 + jnp.exp(-gate))) * up).astype(BF16)
    o_ref[...] += jnp.dot(h, wd_ref[0].astype(BF16), preferred_element_type=F32)

    @pl.when(j == pl.num_programs(1) - 1)
    def _():
        o_ref[...] = _layer_norm_rows(o_ref[...], g_ref[0], b_ref[0], scale=0.5)


def _ffn(x, w_up, w_down, ln_g, ln_b, layer, which, alpha, tm=1024, tf=256):
    n = x.shape[0]
    tm = min(tm, n)
    nf = D_FF // tf
    return pl.pallas_call(
        functools.partial(_ffn_kernel, alpha=alpha),
        grid=(n // tm, nf),
        in_specs=[
            pl.BlockSpec((tm, D_MODEL), lambda i, j: (i, 0)),
            pl.BlockSpec((1, D_MODEL, tf), lambda i, j: (layer, 0, j)),
            pl.BlockSpec((1, D_MODEL, tf), lambda i, j: (layer, 0, j + nf)),
            pl.BlockSpec((1, tf, D_MODEL), lambda i, j: (layer, j, 0)),
            pl.BlockSpec((1, 1, D_MODEL), lambda i, j: (3 * layer + which, 0, 0)),
            pl.BlockSpec((1, 1, D_MODEL), lambda i, j: (3 * layer + which, 0, 0)),
        ],
        out_specs=pl.BlockSpec((tm, D_MODEL), lambda i, j: (i, 0)),
        out_shape=jax.ShapeDtypeStruct((n, D_MODEL), F32),
        scratch_shapes=[pltpu.VMEM((tm, D_MODEL), BF16)],
        compiler_params=_params(("parallel", "arbitrary"), FFN_VMEM_LIMIT),
        name="ffn",
    )(x, w_up, w_up, w_down, ln_g, ln_b)


def _proj_kernel(x_ref, w_ref, o_ref, xb_ref, *, w_is_transposed):
    @pl.when(pl.program_id(1) == 0)
    def _():
        xb_ref[...] = x_ref[...].astype(BF16)

    contract_w = 1 if w_is_transposed else 0
    o_ref[...] = lax.dot_general(xb_ref[...], w_ref[0].astype(BF16), (((1,), (contract_w,)), ((), ())),
                                 preferred_element_type=F32).astype(o_ref.dtype)


def _proj(x, w, layer, tm, tn, name, w_is_transposed=False):
    n, k = x.shape
    tm = min(tm, n)
    cols = w.shape[1] if w_is_transposed else w.shape[2]
    w_spec = (pl.BlockSpec((1, tn, k), lambda i, j: (layer, j, 0)) if w_is_transposed
              else pl.BlockSpec((1, k, tn), lambda i, j: (layer, 0, j)))
    return pl.pallas_call(
        functools.partial(_proj_kernel, w_is_transposed=w_is_transposed),
        grid=(n // tm, cols // tn),
        in_specs=[pl.BlockSpec((tm, k), lambda i, j: (i, 0)), w_spec],
        out_specs=pl.BlockSpec((tm, tn), lambda i, j: (i, j)),
        out_shape=jax.ShapeDtypeStruct((n, cols), BF16),
        scratch_shapes=[pltpu.VMEM((tm, k), BF16)],
        compiler_params=_params(("parallel", "arbitrary")),
        name=name,
    )(x, w)


def _dsa_kernel(aq_ref, iq_ref, iw_ref, ik_ref, ak_ref, av_ref, bias_ref, o_ref,
                skey_ref, hi_ref, lo_ref, ika_ref, ikb_ref, vat_ref, m_ref, acc_ref,
                lg0_ref, lg1_ref, p_ref, al_ref, *, k_sel, seq):
    i = pl.program_id(1)
    blk = BLOCK
    lg_refs = (lg0_ref, lg1_ref)

    @pl.when(i == 0)
    def _():
        ik = ik_ref[0]
        lane = lax.broadcasted_iota(I32, ik.shape, 1)
        zero = jnp.zeros_like(ik)
        ika_ref[...] = jnp.where(lane < IDX_DIM, ik, zero)
        ikb_ref[...] = jnp.where(lane >= IDX_DIM, ik, zero)
        for c in range(seq // blk):
            vat_ref[c, 0:HEAD_DIM, :] = av_ref[0, c * blk:(c + 1) * blk, :].astype(F32).T.astype(BF16)
            vat_ref[c, HEAD_DIM:VT_ROWS, :] = jnp.ones((ONES_ROWS, blk), BF16)

    iqt = iq_ref[0].astype(F32).T.astype(BF16)
    wt = iw_ref[0].astype(F32).T * (IDX_HEADS * IDX_DIM) ** -0.5

    srow = lax.broadcasted_iota(I32, (blk, blk), 0)
    tcol = lax.broadcasted_iota(I32, (blk, blk), 1)

    def score_chunk(c):
        s0 = pl.multiple_of(c * blk, blk)
        ka = ika_ref[pl.ds(s0, blk), :]
        kb = ikb_ref[pl.ds(s0, blk), :]
        sc = jnp.zeros((blk, blk), F32)
        for j in range(IDX_HEADS // 2):
            rhs = iqt[j * LANES:(j + 1) * LANES, :]
            sc = sc + wt[2 * j:2 * j + 1, :] * jnp.maximum(jnp.dot(ka, rhs, preferred_element_type=F32), 0.0)
            sc = sc + wt[2 * j + 1:2 * j + 2, :] * jnp.maximum(jnp.dot(kb, rhs, preferred_element_type=F32), 0.0)
        bits = pltpu.bitcast(sc, I32)
        key = bits ^ ((bits >> 31) & jnp.int32(0x7FFFFFFF))
        causal = srow <= tcol + jnp.where(c < i, jnp.int32(blk), jnp.int32(0))
        key = jnp.where(causal, key, jnp.int32(INT_MIN))
        skey_ref[pl.ds(s0, blk), :] = key
        hi_ref[pl.ds(s0, blk), :] = (key >> 16).astype(I16)
        lo_ref[pl.ds(s0, blk), :] = ((key & jnp.int32(0xFFFF)) - HALF_RANGE).astype(I16)

    def score_pair(j, carry):
        score_chunk(2 * j)
        score_chunk(2 * j + 1)
        return carry

    lax.fori_loop(0, (i + 1) // 2, score_pair, 0)

    @pl.when((i + 1) % 2 == 1)
    def _():
        score_chunk(i)

    n_acc = 4
    rows16 = 2 * SUBLANES

    def count_at_least(ref, trial):
        trial16 = jnp.broadcast_to(trial.astype(I16), (rows16, blk))

        def cnt_body(c, accs):
            vals = ref[pl.ds(pl.multiple_of(c * blk, blk), blk), :]
            accs = list(accs)
            for r in range(blk // rows16):
                rows = vals[r * rows16:(r + 1) * rows16, :]
                accs[r % n_acc] = jnp.where(rows >= trial16, accs[r % n_acc] + jnp.int16(1), accs[r % n_acc])
            return tuple(accs)

        zero = jnp.zeros((rows16, blk), I16)
        accs = lax.fori_loop(0, i + 1, cnt_body, (zero,) * n_acc)
        total = (accs[0] + accs[1]) + (accs[2] + accs[3])
        return jnp.sum(total.astype(I32), axis=0, keepdims=True)

    def largest_with_count(ref, need):
        def bit_body(it, u):
            trial_u = u | jnp.left_shift(jnp.int32(1), jnp.int32(15) - it)
            return jnp.where(count_at_least(ref, trial_u - HALF_RANGE) >= need, trial_u, u)

        return lax.fori_loop(0, 16, bit_body, jnp.zeros((1, blk), I32))

    hi_thr = largest_with_count(hi_ref, k_sel) - HALF_RANGE
    above = jnp.where(hi_thr == HALF_RANGE - 1, 0, count_at_least(hi_ref, jnp.minimum(hi_thr + 1, HALF_RANGE - 1)))
    hi_thr16 = jnp.broadcast_to(hi_thr.astype(I16), (blk, blk))

    def keep_candidates(c, carry):
        rows = pl.ds(pl.multiple_of(c * blk, blk), blk)
        lo_ref[rows, :] = jnp.where(hi_ref[rows, :] == hi_thr16, lo_ref[rows, :], jnp.int16(-HALF_RANGE))
        return carry

    lax.fori_loop(0, i + 1, keep_candidates, 0)
    lo_thr = largest_with_count(lo_ref, k_sel - above)
    thr = hi_thr * (2 * HALF_RANGE) + lo_thr

    def count_keys_at_least(trial):
        trial8 = jnp.broadcast_to(trial, (SUBLANES, blk))

        def cnt_body(c, acc):
            keys = skey_ref[pl.ds(pl.multiple_of(c * blk, blk), blk), :]
            for r in range(blk // SUBLANES):
                acc = acc + jnp.where(keys[r * SUBLANES:(r + 1) * SUBLANES, :] >= trial8, 1, 0)
            return acc

        acc = lax.fori_loop(0, i + 1, cnt_body, jnp.zeros((SUBLANES, blk), I32))
        return jnp.sum(acc, axis=0, keepdims=True)

    live = thr != jnp.int32(INT_MIN)
    surplus = jnp.where(live, count_keys_at_least(thr) - k_sel, 0)

    @pl.when(jnp.max(surplus.astype(F32)) > 0.0)
    def _():
        above_thr = jnp.where(thr == jnp.int32(-INT_MIN - 1), 0,
                              count_keys_at_least(jnp.minimum(thr, jnp.int32(-INT_MIN - 2)) + 1))
        allowed = (k_sel - above_thr).astype(F32)
        prefix_ones = jnp.where(tcol <= srow, 1.0, 0.0).astype(BF16)

        def demote(c, seen):
            rows = pl.ds(pl.multiple_of(c * blk, blk), blk)
            keys = skey_ref[rows, :]
            tied = jnp.logical_and(keys == thr, live)
            rank = seen + jnp.dot(prefix_ones, jnp.where(tied, 1.0, 0.0).astype(BF16), preferred_element_type=F32)
            skey_ref[rows, :] = jnp.where(jnp.logical_and(tied, rank > allowed), jnp.int32(INT_MIN), keys)
            return rank[blk - 1:blk, :]

        lax.fori_loop(0, i + 1, demote, jnp.zeros((1, blk), F32))

    qt = (aq_ref[0].astype(F32) * (HEAD_DIM ** -0.5 * LOG2E)).T.astype(BF16)

    m_ref[...] = jnp.full_like(m_ref, -jnp.inf)
    acc_ref[...] = jnp.zeros_like(acc_ref)

    def logits(c):
        ka = ak_ref[0, pl.ds(pl.multiple_of(c * blk, blk), blk), :]
        return [jnp.dot(ka, qt[h * LANES:(h + 1) * LANES, :], preferred_element_type=F32) for h in range(A_HEADS)]

    def accumulate(c, slot):
        vt = vat_ref[c]
        for h in range(A_HEADS):
            pv = jnp.dot(vt, p_ref[slot, h], preferred_element_type=F32)
            acc_ref[h] = acc_ref[h] * al_ref[slot, h] + pv

    p_ref[1] = jnp.zeros_like(p_ref[1])
    al_ref[1] = jnp.ones_like(al_ref[1])
    first = logits(0)
    for h in range(A_HEADS):
        lg0_ref[h] = first[h]

    def stage(c, slot, has_next=True):
        if has_next:
            nxt = logits(jnp.minimum(c + 1, i))
            for h in range(A_HEADS):
                lg_refs[1 - slot][h] = nxt[h]
        accumulate(jnp.maximum(c - 1, 0), 1 - slot)
        kind = jnp.clip(c - i + 2, 0, 2)
        sel = skey_ref[pl.ds(pl.multiple_of(c * blk, blk), blk), :] >= thr
        for h in range(A_HEADS):
            lg = jnp.where(sel, lg_refs[slot][h] + bias_ref[kind, h], -jnp.inf)
            m_old = m_ref[h]
            m_new = jnp.maximum(m_old, jnp.max(lg, axis=0, keepdims=True))
            m_use = jnp.where(m_new == -jnp.inf, 0.0, m_new)
            alpha = jnp.exp2(m_old - m_use)
            m_ref[h] = m_new
            p_ref[slot, h] = jnp.exp2(lg - m_use).astype(BF16)
            al_ref[slot, h] = alpha

    def pair_body(j, carry):
        stage(2 * j, 0)
        stage(2 * j + 1, 1)
        return carry

    n_chunks = i + 1
    lax.fori_loop(0, n_chunks // 2, pair_body, 0)

    @pl.when(n_chunks % 2 == 1)
    def _():
        stage(i, 0, has_next=False)

    accumulate(i, i % 2)

    for h in range(A_HEADS):
        acc = acc_ref[h]
        o_ref[0, :, h * LANES:(h + 1) * LANES] = (acc[:HEAD_DIM] / acc[HEAD_DIM:HEAD_DIM + 1]).T.astype(BF16)


def _dsa(proj3, bias_tab, k_sel):
    bsz, seq, _ = proj3.shape
    blk = BLOCK
    return pl.pallas_call(
        functools.partial(_dsa_kernel, k_sel=k_sel, seq=seq),
        grid=(bsz, seq // blk),
        in_specs=[
            pl.BlockSpec((1, blk, A_WIDTH), lambda b, i: (b, i, 0)),
            pl.BlockSpec((1, blk, IDX_HEADS * IDX_DIM), lambda b, i: (b, i, 1)),
            pl.BlockSpec((1, blk, LANES), lambda b, i: (b, i, BLK_IW)),
            pl.BlockSpec((1, seq, LANES), lambda b, i: (b, 0, BLK_IK)),
            pl.BlockSpec((1, seq, LANES), lambda b, i: (b, 0, BLK_AK)),
            pl.BlockSpec((1, seq, LANES), lambda b, i: (b, 0, BLK_AV)),
            pl.BlockSpec(bias_tab.shape, lambda b, i: (0, 0, 0, 0)),
        ],
        out_specs=pl.BlockSpec((1, blk, A_WIDTH), lambda b, i: (b, i, 0)),
        out_shape=jax.ShapeDtypeStruct((bsz, seq, A_WIDTH), BF16),
        scratch_shapes=[
            pltpu.VMEM((seq, blk), I32),
            pltpu.VMEM((seq, blk), I16),
            pltpu.VMEM((seq, blk), I16),
            pltpu.VMEM((seq, LANES), BF16),
            pltpu.VMEM((seq, LANES), BF16),
            pltpu.VMEM((seq // blk, VT_ROWS, blk), BF16),
            pltpu.VMEM((A_HEADS, 1, blk), F32),
            pltpu.VMEM((A_HEADS, VT_ROWS, blk), F32),
            pltpu.VMEM((A_HEADS, blk, blk), F32),
            pltpu.VMEM((A_HEADS, blk, blk), F32),
            pltpu.VMEM((2, A_HEADS, blk, blk), BF16),
            pltpu.VMEM((2, A_HEADS, 1, blk), F32),
        ],
        compiler_params=_params(("parallel", "arbitrary")),
        name="dsa",
    )(proj3, proj3, proj3, proj3, proj3, proj3, bias_tab)


def _rms_rows(x, g):
    return x * lax.rsqrt(jnp.mean(x * x, axis=-1, keepdims=True) + RMS_EPS) * g


def _mla_proj_kernel(cq_ref, ckv_ref, kra_ref, krb_ref, cc_ref, ss_ref, qn_ref, kvn_ref, wq_ref, wkv_ref,
                     qt_ref, k_ref, vt_ref):
    tm = cq_ref.shape[0]
    cc = cc_ref[...]
    ss = ss_ref[...]
    qscale = (NOPE_DIM + ROPE_DIM) ** -0.5 * LOG2E
    cq = _rms_rows(cq_ref[...].astype(F32), qn_ref[0]).astype(BF16)
    q3 = jnp.dot(cq, wq_ref[0], preferred_element_type=F32)
    for h in range(B_HEADS):
        base = 3 * LANES * h
        rot = q3[:, base + LANES:base + 2 * LANES] * cc + q3[:, base + 2 * LANES:base + 3 * LANES] * ss
        qt_ref[h, 0:LANES, :] = (q3[:, base:base + LANES] * qscale).T.astype(BF16)
        qt_ref[h, LANES:MLA_QK, :] = (rot * qscale).T.astype(BF16)
    ckv = _rms_rows(ckv_ref[...].astype(F32), kvn_ref[0]).astype(BF16)
    kv = jnp.dot(ckv, wkv_ref[0], preferred_element_type=F32)
    krot = (kra_ref[...].astype(F32) * cc + krb_ref[...].astype(F32) * ss).astype(BF16)
    for h in range(B_HEADS):
        k_ref[:, MLA_QK * h:MLA_QK * h + LANES] = kv[:, h * LANES:(h + 1) * LANES].astype(BF16)
        k_ref[:, MLA_QK * h + LANES:MLA_QK * (h + 1)] = krot
        v_h = kv[:, B_HEADS * NOPE_DIM + h * V_DIM:B_HEADS * NOPE_DIM + (h + 1) * V_DIM]
        for c in range(tm // BLOCK):
            vt_ref[h, c, 0:V_DIM, :] = v_h[c * BLOCK:(c + 1) * BLOCK, :].T.astype(BF16)
            vt_ref[h, c, V_DIM:VT_ROWS, :] = jnp.ones((ONES_ROWS, BLOCK), BF16)


def _mla_proj(proj, cc, ss, qn, kvn, wq, wkv, layer, tm=512):
    n = proj.shape[0]
    per_layer = lambda i: (layer, 0, 0)
    return pl.pallas_call(
        _mla_proj_kernel,
        grid=(n // tm,),
        in_specs=[
            pl.BlockSpec((tm, Q_LORA), lambda i: (i, BLK512_CQ)),
            pl.BlockSpec((tm, KV_LORA), lambda i: (i, BLK512_CKV)),
            pl.BlockSpec((tm, LANES), lambda i: (i, BLK_KRA)),
            pl.BlockSpec((tm, LANES), lambda i: (i, BLK_KRB)),
            pl.BlockSpec((tm, LANES), lambda i: (i, 0)),
            pl.BlockSpec((tm, LANES), lambda i: (i, 0)),
            pl.BlockSpec((1, 1, Q_LORA), per_layer),
            pl.BlockSpec((1, 1, KV_LORA), per_layer),
            pl.BlockSpec((1,) + wq.shape[1:], per_layer),
            pl.BlockSpec((1,) + wkv.shape[1:], per_layer),
        ],
        out_specs=[
            pl.BlockSpec((B_HEADS, MLA_QK, tm), lambda i: (0, 0, i)),
            pl.BlockSpec((tm, B_HEADS * MLA_QK), lambda i: (i, 0)),
            pl.BlockSpec((B_HEADS, tm // BLOCK, VT_ROWS, BLOCK), lambda i: (0, i, 0, 0)),
        ],
        out_shape=[jax.ShapeDtypeStruct((B_HEADS, MLA_QK, n), BF16),
                   jax.ShapeDtypeStruct((n, B_HEADS * MLA_QK), BF16),
                   jax.ShapeDtypeStruct((B_HEADS, n // BLOCK, VT_ROWS, BLOCK), BF16)],
        compiler_params=_params(("parallel",)),
        name="mla_proj",
    )(proj, proj, proj, proj, cc, ss, qn, kvn, wq, wkv)


def _mla_attn_kernel(qt_ref, k_ref, vt_ref, o_ref, lg0_ref, lg1_ref, p_ref, al_ref, m_ref, acc_ref):
    lg_refs = (lg0_ref, lg1_ref)
    qi = pl.program_id(1)
    blk = BLOCK
    srow = lax.broadcasted_iota(I32, (blk, blk), 0)
    tcol = lax.broadcasted_iota(I32, (blk, blk), 1)
    causal_bias = jnp.where(srow <= tcol, 0.0, -jnp.inf).astype(F32)

    def logits(c):
        s0 = pl.multiple_of(c * blk, blk)
        return [jnp.dot(k_ref[0, pl.ds(s0, blk), h * MLA_QK:(h + 1) * MLA_QK], qt_ref[h],
                        preferred_element_type=F32) for h in range(B_HEADS)]

    def accumulate(c, slot):
        for h in range(B_HEADS):
            pv = jnp.dot(vt_ref[h, c], p_ref[slot, h], preferred_element_type=F32)
            acc_ref[h] = acc_ref[h] * al_ref[slot, h] + pv

    m_ref[...] = jnp.full_like(m_ref, -jnp.inf)
    acc_ref[...] = jnp.zeros_like(acc_ref)
    p_ref[1] = jnp.zeros_like(p_ref[1])
    al_ref[1] = jnp.ones_like(al_ref[1])
    first = logits(0)
    for h in range(B_HEADS):
        lg0_ref[h] = first[h]
    def stage(c, slot, has_next=True):
        if has_next:
            nxt = logits(jnp.minimum(c + 1, qi))
            for h in range(B_HEADS):
                lg_refs[1 - slot][h] = nxt[h]
        accumulate(jnp.maximum(c - 1, 0), 1 - slot)
        mask = jnp.where(c == qi, causal_bias, 0.0)
        for h in range(B_HEADS):
            lg = lg_refs[slot][h] + mask
            m_old = m_ref[h]
            m_new = jnp.maximum(m_old, jnp.max(lg, axis=0, keepdims=True))
            m_ref[h] = m_new
            p_ref[slot, h] = jnp.exp2(lg - m_new).astype(BF16)
            al_ref[slot, h] = jnp.exp2(m_old - m_new)

    def pair_body(j, carry):
        stage(2 * j, 0)
        stage(2 * j + 1, 1)
        return carry

    n_chunks = qi + 1
    lax.fori_loop(0, n_chunks // 2, pair_body, 0)

    @pl.when(n_chunks % 2 == 1)
    def _():
        stage(qi, 0, has_next=False)

    accumulate(qi, qi % 2)
    for h in range(B_HEADS):
        acc = acc_ref[h]
        o_ref[0, :, h * V_DIM:(h + 1) * V_DIM] = (acc[:V_DIM] / acc[V_DIM:V_DIM + 1]).T.astype(BF16)


def _mla_attn(qt, k3, vt):
    bsz, seq, _ = k3.shape
    blk = BLOCK
    nq = seq // blk
    return pl.pallas_call(
        _mla_attn_kernel,
        grid=(bsz, nq),
        in_specs=[
            pl.BlockSpec((B_HEADS, MLA_QK, blk), lambda b, i: (0, 0, b * nq + i)),
            pl.BlockSpec((1, seq, B_HEADS * MLA_QK), lambda b, i: (b, 0, 0)),
            pl.BlockSpec((B_HEADS, nq, VT_ROWS, blk), lambda b, i: (0, b, 0, 0)),
        ],
        out_specs=pl.BlockSpec((1, blk, B_WIDTH), lambda b, i: (b, i, 0)),
        out_shape=jax.ShapeDtypeStruct((bsz, seq, B_WIDTH), BF16),
        scratch_shapes=[
            pltpu.VMEM((B_HEADS, blk, blk), F32),
            pltpu.VMEM((B_HEADS, blk, blk), F32),
            pltpu.VMEM((2, B_HEADS, blk, blk), BF16),
            pltpu.VMEM((2, B_HEADS, 1, blk), F32),
            pltpu.VMEM((B_HEADS, 1, blk), F32),
            pltpu.VMEM((B_HEADS, VT_ROWS, blk), F32),
        ],
        compiler_params=_params(("parallel", "arbitrary")),
        name="mla_attn",
    )(qt, k3, vt)


def _mem_attn_kernel(q_ref, kv_ref, o_ref):
    scale = HEAD_DIM ** -0.5
    for h in range(C_HEADS):
        q = q_ref[0, :, h * LANES:(h + 1) * LANES]
        k = kv_ref[0, :, h * LANES:(h + 1) * LANES]
        v = kv_ref[0, :, C_WIDTH + h * LANES:C_WIDTH + (h + 1) * LANES]
        s = lax.dot_general(q, k, (((1,), (1,)), ((), ())), preferred_element_type=F32) * scale
        p = jnp.exp(s - jnp.max(s, axis=-1, keepdims=True))
        l = jnp.sum(p, axis=-1, keepdims=True)
        o = jnp.dot(p.astype(BF16), v, preferred_element_type=F32) / l
        o_ref[0, :, h * LANES:(h + 1) * LANES] = o.astype(BF16)


def _mem_attn(proj3, mkv3, tq=512):
    bsz, seq, _ = proj3.shape
    mlen = mkv3.shape[1]
    return pl.pallas_call(
        _mem_attn_kernel,
        grid=(bsz, seq // tq),
        in_specs=[pl.BlockSpec((1, tq, C_WIDTH), lambda b, i: (b, i, BLK512_C)),
                  pl.BlockSpec((1, mlen, 2 * C_WIDTH), lambda b, i: (b, 0, 0))],
        out_specs=pl.BlockSpec((1, tq, C_WIDTH), lambda b, i: (b, i, 0)),
        out_shape=jax.ShapeDtypeStruct((bsz, seq, C_WIDTH), BF16),
        compiler_params=_params(("parallel", "arbitrary")),
        name="mem_attn",
    )(proj3, mkv3)


def _mix_out_kernel(oa_ref, ob_ref, oc_ref, ga_ref, gb_ref, gc_ref, x_ref, wb_ref, wo_ref, g_ref, b_ref,
                    o_ref, *, alpha):
    def gated(o_blk, gate_ref, r0, r1):
        y = jnp.dot(o_blk, wb_ref[0, r0:r1, :], preferred_element_type=F32)
        return y * (1.0 / (1.0 + jnp.exp(-gate_ref[...].astype(F32))))

    y = gated(oa_ref[...], ga_ref, 0, A_WIDTH)
    y = y + gated(ob_ref[...], gb_ref, A_WIDTH, A_WIDTH + B_WIDTH)
    y = y + gated(oc_ref[...], gc_ref, A_WIDTH + B_WIDTH, A_WIDTH + B_WIDTH + C_WIDTH)
    mix = jnp.dot(y.astype(BF16), wo_ref[0], preferred_element_type=F32)
    o_ref[...] = _layer_norm_rows(alpha * x_ref[...] + mix, g_ref[0], b_ref[0])


def _mix_out(oa, ob, oc, proj, x, wb, wo, ln_g, ln_b, layer, alpha, tm=256):
    n = x.shape[0]
    per_layer = lambda i: (layer, 0, 0)
    ln_row = lambda i: (3 * layer + 1, 0, 0)
    return pl.pallas_call(
        functools.partial(_mix_out_kernel, alpha=alpha),
        grid=(n // tm,),
        in_specs=[
            pl.BlockSpec((tm, A_WIDTH), lambda i: (i, 0)),
            pl.BlockSpec((tm, B_WIDTH), lambda i: (i, 0)),
            pl.BlockSpec((tm, C_WIDTH), lambda i: (i, 0)),
            pl.BlockSpec((tm, D_MODEL), lambda i: (i, BLK2048_GATE0)),
            pl.BlockSpec((tm, D_MODEL), lambda i: (i, BLK2048_GATE0 + 1)),
            pl.BlockSpec((tm, D_MODEL), lambda i: (i, BLK2048_GATE0 + 2)),
            pl.BlockSpec((tm, D_MODEL), lambda i: (i, 0)),
            pl.BlockSpec((1,) + wb.shape[1:], per_layer),
            pl.BlockSpec((1,) + wo.shape[1:], per_layer),
            pl.BlockSpec((1, 1, D_MODEL), ln_row),
            pl.BlockSpec((1, 1, D_MODEL), ln_row),
        ],
        out_specs=pl.BlockSpec((tm, D_MODEL), lambda i: (i, 0)),
        out_shape=jax.ShapeDtypeStruct((n, D_MODEL), F32),
        compiler_params=_params(("parallel",)),
        name="mix_out",
    )(oa, ob, oc, proj, proj, proj, x, wb, wo, ln_g, ln_b)


_IN_OFFS = tuple(sum(IN_SIZES[:k]) for k in range(len(IN_SIZES) + 1))


def _pack_w_in_t(w):
    wt = jnp.swapaxes(w, 1, 2)
    o = _IN_OFFS
    seg = lambda k: wt[:, o[k]:o[k + 1]]
    zeros = lambda r: jnp.zeros((w.shape[0], r, w.shape[1]), w.dtype)
    ik, iw, kr = seg(4), seg(5), seg(8)
    half = ROPE_DIM // 2
    packed = jnp.concatenate([
        wt[:, :o[4]],
        ik, ik,
        iw, zeros(LANES - IDX_HEADS),
        kr, zeros(LANES - ROPE_DIM),
        kr[:, half:], kr[:, :half], zeros(LANES - ROPE_DIM),
        seg(6), seg(7), seg(9), seg(10),
    ], axis=1)
    assert packed.shape[1] == PROJ_COLS
    return packed.astype(BF16)


def _pack_w_uq(w):
    half = ROPE_DIM // 2
    zeros = jnp.zeros(w.shape[:-1] + (LANES - ROPE_DIM,), w.dtype)
    cols = []
    for h in range(B_HEADS):
        base = h * (NOPE_DIM + ROPE_DIM)
        r = w[..., base + NOPE_DIM:base + NOPE_DIM + ROPE_DIM]
        cols += [w[..., base:base + NOPE_DIM], r, zeros, r[..., half:], r[..., :half], zeros]
    return jnp.concatenate(cols, axis=-1).astype(BF16)


def _pack_w_ukv(w):
    step = NOPE_DIM + V_DIM
    ks = [w[..., h * step:h * step + NOPE_DIM] for h in range(B_HEADS)]
    vs = [w[..., h * step + NOPE_DIM:(h + 1) * step] for h in range(B_HEADS)]
    return jnp.concatenate(ks + vs, axis=-1).astype(BF16)


def _rel_bucket(dist):
    n = jnp.maximum(dist, 0)
    max_exact = REL_BUCKETS // 2
    nf = jnp.maximum(n, 1).astype(F32)
    large = max_exact + (jnp.log(nf / max_exact) / math.log(REL_MAX_DIST / max_exact)
                         * (REL_BUCKETS - max_exact)).astype(I32)
    large = jnp.minimum(large, REL_BUCKETS - 1)
    return jnp.where(n < max_exact, n, large)


def _bias_tables(rel_bias):
    assert BLOCK + 1 >= REL_MAX_DIST
    s = jnp.arange(BLOCK)[:, None]
    t = jnp.arange(BLOCK)[None, :]
    tiles = [jnp.zeros((A_HEADS, BLOCK, BLOCK), F32)]
    for block_gap in (1, 0):
        dist = t - s + block_gap * BLOCK
        onehot = (_rel_bucket(dist)[:, :, None] == jnp.arange(REL_BUCKETS)).astype(F32)
        looked_up = jnp.einsum("stb,bh->sth", onehot, rel_bias, precision=lax.Precision.HIGHEST)
        tile = (looked_up - rel_bias[REL_BUCKETS - 1]) * LOG2E
        tile = jnp.where((dist >= 0)[:, :, None], tile, -jnp.inf)
        tiles.append(jnp.transpose(tile, (2, 0, 1)))
    return jnp.stack(tiles).astype(F32)


def kernel(x, mem, positions, rel_bias, ln_g, ln_b, ffn1_up, ffn1_down, w_in, q_norm, kv_norm, w_uq, w_ukv,
           w_mem_kv, w_branch, w_out, ffn2_up, ffn2_down):
    bsz, seq, d = x.shape
    depth = ffn1_up.shape[0]
    n = bsz * seq
    alpha = (2 * depth) ** 0.25
    k_sel = min(TOPK_MAX, seq // 4)
    assert d == D_MODEL and seq % (2 * BLOCK) == 0

    inv_freq = ROPE_THETA ** (-jnp.arange(0, ROPE_DIM, 2, dtype=F32) / ROPE_DIM)
    ang = positions.astype(F32)[..., None] * inv_freq
    cos, sin = jnp.cos(ang).reshape(n, -1), jnp.sin(ang).reshape(n, -1)
    pad = jnp.zeros((n, LANES - ROPE_DIM), F32)
    cc = jnp.concatenate([cos, cos, pad], axis=1)
    ss = jnp.concatenate([-sin, sin, pad], axis=1)
    bias_tab = _bias_tables(rel_bias)

    xf = x.reshape(n, d)
    memf = mem.reshape(bsz * mem.shape[1], d)
    ln_g3 = ln_g.reshape(depth * 3, 1, d)
    ln_b3 = ln_b.reshape(depth * 3, 1, d)
    qn3 = q_norm.reshape(depth, 1, Q_LORA)
    kvn3 = kv_norm.reshape(depth, 1, KV_LORA)
    w_in_p = _pack_w_in_t(w_in)
    w_uq_p = _pack_w_uq(w_uq.astype(BF16))
    w_ukv_p = _pack_w_ukv(w_ukv.astype(BF16))
    w_branch_b = w_branch.astype(BF16)
    w_out_b = w_out.astype(BF16)

    for l in range(depth):
        xf = _ffn(xf, ffn1_up, ffn1_down, ln_g3, ln_b3, l, 0, alpha)
        proj = _proj(xf, w_in_p, l, 1024, 2048, "in_proj", w_is_transposed=True)
        proj3 = proj.reshape(bsz, seq, PROJ_COLS)
        o_a = _dsa(proj3, bias_tab, k_sel).reshape(n, A_WIDTH)
        qt, k, vt = _mla_proj(proj, cc, ss, qn3, kvn3, w_uq_p, w_ukv_p, l)
        o_b = _mla_attn(qt, k.reshape(bsz, seq, -1), vt).reshape(n, B_WIDTH)
        mkv = _proj(memf, w_mem_kv, l, 512, 1024, "mem_proj")
        o_c = _mem_attn(proj3, mkv.reshape(bsz, mem.shape[1], 2 * C_WIDTH)).reshape(n, C_WIDTH)
        xf = _mix_out(o_a, o_b, o_c, proj, xf, w_branch_b, w_out_b, ln_g3, ln_b3, l, alpha)
        xf = _ffn(xf, ffn2_up, ffn2_down, ln_g3, ln_b3, l, 2, alpha)
    return xf.reshape(bsz, seq, d)
```

```python
import functools
import math

import jax
import jax.numpy as jnp
from jax import lax
from jax.experimental import pallas as pl
from jax.experimental.pallas import tpu as pltpu

F32 = jnp.float32
BF16 = jnp.bfloat16
I32 = jnp.int32
I16 = jnp.int16

D_MODEL = 2048
D_FF = 5632
HEAD_DIM = 128
A_HEADS = 6
IDX_HEADS = 16
IDX_DIM = 64
TOPK_MAX = 256
B_HEADS = 6
Q_LORA = 512
KV_LORA = 512
NOPE_DIM = 128
ROPE_DIM = 64
V_DIM = 128
ROPE_THETA = 10000.0
C_HEADS = 4
REL_BUCKETS = 32
REL_MAX_DIST = 128
LN_EPS = 1e-5
RMS_EPS = 1e-6
A_WIDTH = A_HEADS * HEAD_DIM
B_WIDTH = B_HEADS * V_DIM
C_WIDTH = C_HEADS * HEAD_DIM
IN_SIZES = (A_WIDTH, HEAD_DIM, HEAD_DIM, IDX_HEADS * IDX_DIM, IDX_DIM, IDX_HEADS,
            Q_LORA, KV_LORA, ROPE_DIM, C_WIDTH, 3 * D_MODEL)

LANES = 128
SUBLANES = 8
BLOCK = 256
MLA_QK = 2 * LANES
ONES_ROWS = 16
VT_ROWS = HEAD_DIM + ONES_ROWS
INT_MIN = -2 ** 31
HALF_RANGE = 2 ** 15
LOG2E = 1.4426950408889634

PROJ_COLS = 10240
BLK_AK, BLK_AV = 6, 7
BLK_IK, BLK_IW, BLK_KRA, BLK_KRB = 16, 17, 18, 19
BLK512_CQ, BLK512_CKV, BLK512_C = 5, 6, 7
BLK2048_GATE0 = 2

VMEM_LIMIT = 56 * 1024 * 1024
FFN_VMEM_LIMIT = 62 * 1024 * 1024


def _params(sem, vmem=VMEM_LIMIT):
    return pltpu.CompilerParams(dimension_semantics=sem, vmem_limit_bytes=vmem)


def _layer_norm_rows(y, g, b, scale=1.0):
    mu = jnp.mean(y, axis=-1, keepdims=True)
    d = y - mu
    var = jnp.mean(d * d, axis=-1, keepdims=True)
    return d * (scale * lax.rsqrt(scale * scale * var + LN_EPS)) * g + b


def _ffn_kernel(*refs, alpha, casts_next):
    if casts_next:
        x_ref, wg_ref, wu_ref, wd_ref, g_ref, b_ref, next_up_ref, next_down_ref, o_ref, up_b_ref, down_b_ref, xb_ref = refs
    else:
        x_ref, wg_ref, wu_ref, wd_ref, g_ref, b_ref, o_ref, xb_ref = refs
    j = pl.program_id(1)

    @pl.when(j == 0)
    def _():
        x = x_ref[...]
        xb_ref[...] = x.astype(BF16)
        o_ref[...] = (2.0 * alpha) * x

    xb = xb_ref[...]
    gate = jnp.dot(xb, wg_ref[...], preferred_element_type=F32)
    up = jnp.dot(xb, wu_ref[...], preferred_element_type=F32)
    h = (gate * (1.0 / (1.0 + jnp.exp(-gate))) * up).astype(BF16)
    o_ref[...] += jnp.dot(h, wd_ref[...], preferred_element_type=F32)

    if casts_next:
        up_b_ref[...] = next_up_ref[0].astype(BF16)
        down_b_ref[...] = next_down_ref[0].astype(BF16)

    @pl.when(j == pl.num_programs(1) - 1)
    def _():
        o_ref[...] = _layer_norm_rows(o_ref[...], g_ref[0], b_ref[0], scale=0.5)


def _ffn(x, w_up_b, w_down_b, ln_g, ln_b, layer, which, alpha, next_weights=None, tm=1024, tf=512):
    n = x.shape[0]
    tm = min(tm, n)
    nf = D_FF // tf
    ni = n // tm
    casts_next = next_weights is not None
    in_specs = [
        pl.BlockSpec((tm, D_MODEL), lambda i, j: (i, 0)),
        pl.BlockSpec((D_MODEL, tf), lambda i, j: (0, j)),
        pl.BlockSpec((D_MODEL, tf), lambda i, j: (0, j + nf)),
        pl.BlockSpec((tf, D_MODEL), lambda i, j: (j, 0)),
        pl.BlockSpec((1, 1, D_MODEL), lambda i, j: (3 * layer + which, 0, 0)),
        pl.BlockSpec((1, 1, D_MODEL), lambda i, j: (3 * layer + which, 0, 0)),
    ]
    out_specs = [pl.BlockSpec((tm, D_MODEL), lambda i, j: (i, 0))]
    out_shape = [jax.ShapeDtypeStruct((n, D_MODEL), F32)]
    operands = [x, w_up_b, w_up_b, w_down_b, ln_g, ln_b]
    if casts_next:
        next_up, next_down, next_layer = next_weights
        up_slab = (D_MODEL // ni, 2 * D_FF // nf)
        down_slab = (D_FF // nf, D_MODEL // ni)
        in_specs += [pl.BlockSpec((1,) + up_slab, lambda i, j: (next_layer, i, j)),
                     pl.BlockSpec((1,) + down_slab, lambda i, j: (next_layer, j, i))]
        out_specs += [pl.BlockSpec(up_slab, lambda i, j: (i, j)), pl.BlockSpec(down_slab, lambda i, j: (j, i))]
        out_shape += [jax.ShapeDtypeStruct(next_up.shape[1:], BF16), jax.ShapeDtypeStruct(next_down.shape[1:], BF16)]
        operands += [next_up, next_down]
    outs = pl.pallas_call(
        functools.partial(_ffn_kernel, alpha=alpha, casts_next=casts_next),
        grid=(ni, nf),
        in_specs=in_specs,
        out_specs=out_specs,
        out_shape=out_shape,
        scratch_shapes=[pltpu.VMEM((tm, D_MODEL), BF16)],
        compiler_params=_params(("parallel", "arbitrary"), FFN_VMEM_LIMIT),
        name="ffn",
    )(*operands)
    return outs if casts_next else (outs[0], None, None)


def _proj_kernel(x_ref, w_ref, o_ref, xb_ref, *, w_is_transposed):
    @pl.when(pl.program_id(1) == 0)
    def _():
        xb_ref[...] = x_ref[...].astype(BF16)

    contract_w = 1 if w_is_transposed else 0
    o_ref[...] = lax.dot_general(xb_ref[...], w_ref[0].astype(BF16), (((1,), (contract_w,)), ((), ())),
                                 preferred_element_type=F32).astype(o_ref.dtype)


def _proj(x, w, layer, tm, tn, name, w_is_transposed=False):
    n, k = x.shape
    tm = min(tm, n)
    cols = w.shape[1] if w_is_transposed else w.shape[2]
    w_spec = (pl.BlockSpec((1, tn, k), lambda i, j: (layer, j, 0)) if w_is_transposed
              else pl.BlockSpec((1, k, tn), lambda i, j: (layer, 0, j)))
    return pl.pallas_call(
        functools.partial(_proj_kernel, w_is_transposed=w_is_transposed),
        grid=(n // tm, cols // tn),
        in_specs=[pl.BlockSpec((tm, k), lambda i, j: (i, 0)), w_spec],
        out_specs=pl.BlockSpec((tm, tn), lambda i, j: (i, j)),
        out_shape=jax.ShapeDtypeStruct((n, cols), BF16),
        scratch_shapes=[pltpu.VMEM((tm, k), BF16)],
        compiler_params=_params(("parallel", "arbitrary")),
        name=name,
    )(x, w)


def _dsa_kernel(aq_ref, iq_ref, iw_ref, ik_ref, ak_ref, av_ref, bias_ref, o_ref,
                skey_ref, hi_ref, lo_ref, ika_ref, ikb_ref, vat_ref, m_ref, acc_ref,
                lg0_ref, lg1_ref, p_ref, al_ref, *, k_sel, seq):
    i = pl.program_id(1)
    blk = BLOCK
    lg_refs = (lg0_ref, lg1_ref)

    @pl.when(i == 0)
    def _():
        ik = ik_ref[0]
        lane = lax.broadcasted_iota(I32, ik.shape, 1)
        zero = jnp.zeros_like(ik)
        ika_ref[...] = jnp.where(lane < IDX_DIM, ik, zero)
        ikb_ref[...] = jnp.where(lane >= IDX_DIM, ik, zero)
        for c in range(seq // blk):
            vat_ref[c, 0:HEAD_DIM, :] = av_ref[0, c * blk:(c + 1) * blk, :].astype(F32).T.astype(BF16)
            vat_ref[c, HEAD_DIM:VT_ROWS, :] = jnp.ones((ONES_ROWS, blk), BF16)

    iqt = iq_ref[0].astype(F32).T.astype(BF16)
    wt = iw_ref[0].astype(F32).T * (IDX_HEADS * IDX_DIM) ** -0.5

    srow = lax.broadcasted_iota(I32, (blk, blk), 0)
    tcol = lax.broadcasted_iota(I32, (blk, blk), 1)

    def score_chunk(c):
        s0 = pl.multiple_of(c * blk, blk)
        ka = ika_ref[pl.ds(s0, blk), :]
        kb = ikb_ref[pl.ds(s0, blk), :]
        sc = jnp.zeros((blk, blk), F32)
        for j in range(IDX_HEADS // 2):
            rhs = iqt[j * LANES:(j + 1) * LANES, :]
            sc = sc + wt[2 * j:2 * j + 1, :] * jnp.maximum(jnp.dot(ka, rhs, preferred_element_type=F32), 0.0)
            sc = sc + wt[2 * j + 1:2 * j + 2, :] * jnp.maximum(jnp.dot(kb, rhs, preferred_element_type=F32), 0.0)
        bits = pltpu.bitcast(sc, I32)
        key = bits ^ ((bits >> 31) & jnp.int32(0x7FFFFFFF))
        causal = srow <= tcol + jnp.where(c < i, jnp.int32(blk), jnp.int32(0))
        key = jnp.where(causal, key, jnp.int32(INT_MIN))
        skey_ref[pl.ds(s0, blk), :] = key
        hi_ref[pl.ds(s0, blk), :] = (key >> 16).astype(I16)
        lo_ref[pl.ds(s0, blk), :] = ((key & jnp.int32(0xFFFF)) - HALF_RANGE).astype(I16)

    def score_pair(j, carry):
        score_chunk(2 * j)
        score_chunk(2 * j + 1)
        return carry

    lax.fori_loop(0, (i + 1) // 2, score_pair, 0)

    @pl.when((i + 1) % 2 == 1)
    def _():
        score_chunk(i)

    n_acc = 4
    rows16 = 2 * SUBLANES

    def count_at_least(ref, trial):
        trial16 = jnp.broadcast_to(trial.astype(I16), (rows16, blk))

        def cnt_body(c, accs):
            vals = ref[pl.ds(pl.multiple_of(c * blk, blk), blk), :]
            accs = list(accs)
            for r in range(blk // rows16):
                rows = vals[r * rows16:(r + 1) * rows16, :]
                accs[r % n_acc] = jnp.where(rows >= trial16, accs[r % n_acc] + jnp.int16(1), accs[r % n_acc])
            return tuple(accs)

        zero = jnp.zeros((rows16, blk), I16)
        accs = lax.fori_loop(0, i + 1, cnt_body, (zero,) * n_acc)
        total = (accs[0] + accs[1]) + (accs[2] + accs[3])
        return jnp.sum(total.astype(I32), axis=0, keepdims=True)

    def largest_with_count(ref, need):
        def bit_body(it, u):
            trial_u = u | jnp.left_shift(jnp.int32(1), jnp.int32(15) - it)
            return jnp.where(count_at_least(ref, trial_u - HALF_RANGE) >= need, trial_u, u)

        return lax.fori_loop(0, 16, bit_body, jnp.zeros((1, blk), I32))

    hi_thr = largest_with_count(hi_ref, k_sel) - HALF_RANGE
    above = jnp.where(hi_thr == HALF_RANGE - 1, 0, count_at_least(hi_ref, jnp.minimum(hi_thr + 1, HALF_RANGE - 1)))
    hi_thr16 = jnp.broadcast_to(hi_thr.astype(I16), (blk, blk))

    def keep_candidates(c, carry):
        rows = pl.ds(pl.multiple_of(c * blk, blk), blk)
        lo_ref[rows, :] = jnp.where(hi_ref[rows, :] == hi_thr16, lo_ref[rows, :], jnp.int16(-HALF_RANGE))
        return carry

    lax.fori_loop(0, i + 1, keep_candidates, 0)
    lo_thr = largest_with_count(lo_ref, k_sel - above)
    thr = hi_thr * (2 * HALF_RANGE) + lo_thr

    def count_keys_at_least(trial):
        trial8 = jnp.broadcast_to(trial, (SUBLANES, blk))

        def cnt_body(c, acc):
            keys = skey_ref[pl.ds(pl.multiple_of(c * blk, blk), blk), :]
            for r in range(blk // SUBLANES):
                acc = acc + jnp.where(keys[r * SUBLANES:(r + 1) * SUBLANES, :] >= trial8, 1, 0)
            return acc

        acc = lax.fori_loop(0, i + 1, cnt_body, jnp.zeros((SUBLANES, blk), I32))
        return jnp.sum(acc, axis=0, keepdims=True)

    live = thr != jnp.int32(INT_MIN)
    surplus = jnp.where(live, count_keys_at_least(thr) - k_sel, 0)

    @pl.when(jnp.max(surplus.astype(F32)) > 0.0)
    def _():
        above_thr = jnp.where(thr == jnp.int32(-INT_MIN - 1), 0,
                              count_keys_at_least(jnp.minimum(thr, jnp.int32(-INT_MIN - 2)) + 1))
        allowed = (k_sel - above_thr).astype(F32)
        prefix_ones = jnp.where(tcol <= srow, 1.0, 0.0).astype(BF16)

        def demote(c, seen):
            rows = pl.ds(pl.multiple_of(c * blk, blk), blk)
            keys = skey_ref[rows, :]
            tied = jnp.logical_and(keys == thr, live)
            rank = seen + jnp.dot(prefix_ones, jnp.where(tied, 1.0, 0.0).astype(BF16), preferred_element_type=F32)
            skey_ref[rows, :] = jnp.where(jnp.logical_and(tied, rank > allowed), jnp.int32(INT_MIN), keys)
            return rank[blk - 1:blk, :]

        lax.fori_loop(0, i + 1, demote, jnp.zeros((1, blk), F32))

    qt = (aq_ref[0].astype(F32) * (HEAD_DIM ** -0.5 * LOG2E)).T.astype(BF16)

    m_ref[...] = jnp.full_like(m_ref, -jnp.inf)
    acc_ref[...] = jnp.zeros_like(acc_ref)

    def logits(c):
        ka = ak_ref[0, pl.ds(pl.multiple_of(c * blk, blk), blk), :]
        return [jnp.dot(ka, qt[h * LANES:(h + 1) * LANES, :], preferred_element_type=F32) for h in range(A_HEADS)]

    def accumulate(c, slot):
        vt = vat_ref[c]
        for h in range(A_HEADS):
            pv = jnp.dot(vt, p_ref[slot, h], preferred_element_type=F32)
            acc_ref[h] = acc_ref[h] * al_ref[slot, h] + pv

    p_ref[1] = jnp.zeros_like(p_ref[1])
    al_ref[1] = jnp.ones_like(al_ref[1])
    first = logits(0)
    for h in range(A_HEADS):
        lg0_ref[h] = first[h]

    def stage(c, slot, has_next=True):
        if has_next:
            nxt = logits(jnp.minimum(c + 1, i))
            for h in range(A_HEADS):
                lg_refs[1 - slot][h] = nxt[h]
        accumulate(jnp.maximum(c - 1, 0), 1 - slot)
        kind = jnp.clip(c - i + 2, 0, 2)
        sel = skey_ref[pl.ds(pl.multiple_of(c * blk, blk), blk), :] >= thr
        for h in range(A_HEADS):
            lg = jnp.where(sel, lg_refs[slot][h] + bias_ref[kind, h], -jnp.inf)
            m_old = m_ref[h]
            m_new = jnp.maximum(m_old, jnp.max(lg, axis=0, keepdims=True))
            m_use = jnp.where(m_new == -jnp.inf, 0.0, m_new)
            alpha = jnp.exp2(m_old - m_use)
            m_ref[h] = m_new
            p_ref[slot, h] = jnp.exp2(lg - m_use).astype(BF16)
            al_ref[slot, h] = alpha

    def pair_body(j, carry):
        stage(2 * j, 0)
        stage(2 * j + 1, 1)
        return carry

    n_chunks = i + 1
    lax.fori_loop(0, n_chunks // 2, pair_body, 0)

    @pl.when(n_chunks % 2 == 1)
    def _():
        stage(i, 0, has_next=False)

    accumulate(i, i % 2)

    for h in range(A_HEADS):
        acc = acc_ref[h]
        o_ref[0, :, h * LANES:(h + 1) * LANES] = (acc[:HEAD_DIM] / acc[HEAD_DIM:HEAD_DIM + 1]).T.astype(BF16)


def _dsa(proj3, bias_tab, k_sel):
    bsz, seq, _ = proj3.shape
    blk = BLOCK
    return pl.pallas_call(
        functools.partial(_dsa_kernel, k_sel=k_sel, seq=seq),
        grid=(bsz, seq // blk),
        in_specs=[
            pl.BlockSpec((1, blk, A_WIDTH), lambda b, i: (b, i, 0)),
            pl.BlockSpec((1, blk, IDX_HEADS * IDX_DIM), lambda b, i: (b, i, 1)),
            pl.BlockSpec((1, blk, LANES), lambda b, i: (b, i, BLK_IW)),
            pl.BlockSpec((1, seq, LANES), lambda b, i: (b, 0, BLK_IK)),
            pl.BlockSpec((1, seq, LANES), lambda b, i: (b, 0, BLK_AK)),
            pl.BlockSpec((1, seq, LANES), lambda b, i: (b, 0, BLK_AV)),
            pl.BlockSpec(bias_tab.shape, lambda b, i: (0, 0, 0, 0)),
        ],
        out_specs=pl.BlockSpec((1, blk, A_WIDTH), lambda b, i: (b, i, 0)),
        out_shape=jax.ShapeDtypeStruct((bsz, seq, A_WIDTH), BF16),
        scratch_shapes=[
            pltpu.VMEM((seq, blk), I32),
            pltpu.VMEM((seq, blk), I16),
            pltpu.VMEM((seq, blk), I16),
            pltpu.VMEM((seq, LANES), BF16),
            pltpu.VMEM((seq, LANES), BF16),
            pltpu.VMEM((seq // blk, VT_ROWS, blk), BF16),
            pltpu.VMEM((A_HEADS, 1, blk), F32),
            pltpu.VMEM((A_HEADS, VT_ROWS, blk), F32),
            pltpu.VMEM((A_HEADS, blk, blk), F32),
            pltpu.VMEM((A_HEADS, blk, blk), F32),
            pltpu.VMEM((2, A_HEADS, blk, blk), BF16),
            pltpu.VMEM((2, A_HEADS, 1, blk), F32),
        ],
        compiler_params=_params(("parallel", "arbitrary")),
        name="dsa",
    )(proj3, proj3, proj3, proj3, proj3, proj3, bias_tab)


def _rms_rows(x, g):
    return x * lax.rsqrt(jnp.mean(x * x, axis=-1, keepdims=True) + RMS_EPS) * g


def _mla_proj_kernel(cq_ref, ckv_ref, kra_ref, krb_ref, cc_ref, ss_ref, qn_ref, kvn_ref, wq_ref, wkv_ref,
                     qt_ref, k_ref, vt_ref):
    tm = cq_ref.shape[0]
    cc = cc_ref[...]
    ss = ss_ref[...]
    qscale = (NOPE_DIM + ROPE_DIM) ** -0.5 * LOG2E
    cq = _rms_rows(cq_ref[...].astype(F32), qn_ref[0]).astype(BF16)
    q3 = jnp.dot(cq, wq_ref[0], preferred_element_type=F32)
    for h in range(B_HEADS):
        base = 3 * LANES * h
        rot = q3[:, base + LANES:base + 2 * LANES] * cc + q3[:, base + 2 * LANES:base + 3 * LANES] * ss
        qt_ref[h, 0:LANES, :] = (q3[:, base:base + LANES] * qscale).T.astype(BF16)
        qt_ref[h, LANES:MLA_QK, :] = (rot * qscale).T.astype(BF16)
    ckv = _rms_rows(ckv_ref[...].astype(F32), kvn_ref[0]).astype(BF16)
    kv = jnp.dot(ckv, wkv_ref[0], preferred_element_type=F32)
    krot = (kra_ref[...].astype(F32) * cc + krb_ref[...].astype(F32) * ss).astype(BF16)
    for h in range(B_HEADS):
        k_ref[:, MLA_QK * h:MLA_QK * h + LANES] = kv[:, h * LANES:(h + 1) * LANES].astype(BF16)
        k_ref[:, MLA_QK * h + LANES:MLA_QK * (h + 1)] = krot
        v_h = kv[:, B_HEADS * NOPE_DIM + h * V_DIM:B_HEADS * NOPE_DIM + (h + 1) * V_DIM]
        for c in range(tm // BLOCK):
            vt_ref[h, c, 0:V_DIM, :] = v_h[c * BLOCK:(c + 1) * BLOCK, :].T.astype(BF16)
            vt_ref[h, c, V_DIM:VT_ROWS, :] = jnp.ones((ONES_ROWS, BLOCK), BF16)


def _mla_proj(proj, cc, ss, qn, kvn, wq, wkv, layer, tm=512):
    n = proj.shape[0]
    per_layer = lambda i: (layer, 0, 0)
    return pl.pallas_call(
        _mla_proj_kernel,
        grid=(n // tm,),
        in_specs=[
            pl.BlockSpec((tm, Q_LORA), lambda i: (i, BLK512_CQ)),
            pl.BlockSpec((tm, KV_LORA), lambda i: (i, BLK512_CKV)),
            pl.BlockSpec((tm, LANES), lambda i: (i, BLK_KRA)),
            pl.BlockSpec((tm, LANES), lambda i: (i, BLK_KRB)),
            pl.BlockSpec((tm, LANES), lambda i: (i, 0)),
            pl.BlockSpec((tm, LANES), lambda i: (i, 0)),
            pl.BlockSpec((1, 1, Q_LORA), per_layer),
            pl.BlockSpec((1, 1, KV_LORA), per_layer),
            pl.BlockSpec((1,) + wq.shape[1:], per_layer),
            pl.BlockSpec((1,) + wkv.shape[1:], per_layer),
        ],
        out_specs=[
            pl.BlockSpec((B_HEADS, MLA_QK, tm), lambda i: (0, 0, i)),
            pl.BlockSpec((tm, B_HEADS * MLA_QK), lambda i: (i, 0)),
            pl.BlockSpec((B_HEADS, tm // BLOCK, VT_ROWS, BLOCK), lambda i: (0, i, 0, 0)),
        ],
        out_shape=[jax.ShapeDtypeStruct((B_HEADS, MLA_QK, n), BF16),
                   jax.ShapeDtypeStruct((n, B_HEADS * MLA_QK), BF16),
                   jax.ShapeDtypeStruct((B_HEADS, n // BLOCK, VT_ROWS, BLOCK), BF16)],
        compiler_params=_params(("parallel",)),
        name="mla_proj",
    )(proj, proj, proj, proj, cc, ss, qn, kvn, wq, wkv)


def _mla_attn_kernel(qt_ref, k_ref, vt_ref, o_ref, lg0_ref, lg1_ref, p_ref, al_ref, m_ref, acc_ref):
    lg_refs = (lg0_ref, lg1_ref)
    qi = pl.program_id(1)
    blk = BLOCK
    srow = lax.broadcasted_iota(I32, (blk, blk), 0)
    tcol = lax.broadcasted_iota(I32, (blk, blk), 1)
    causal_bias = jnp.where(srow <= tcol, 0.0, -jnp.inf).astype(F32)

    def logits(c):
        s0 = pl.multiple_of(c * blk, blk)
        return [jnp.dot(k_ref[0, pl.ds(s0, blk), h * MLA_QK:(h + 1) * MLA_QK], qt_ref[h],
                        preferred_element_type=F32) for h in range(B_HEADS)]

    def accumulate(c, slot):
        for h in range(B_HEADS):
            pv = jnp.dot(vt_ref[h, c], p_ref[slot, h], preferred_element_type=F32)
            acc_ref[h] = acc_ref[h] * al_ref[slot, h] + pv

    m_ref[...] = jnp.full_like(m_ref, -jnp.inf)
    acc_ref[...] = jnp.zeros_like(acc_ref)
    p_ref[1] = jnp.zeros_like(p_ref[1])
    al_ref[1] = jnp.ones_like(al_ref[1])
    first = logits(0)
    for h in range(B_HEADS):
        lg0_ref[h] = first[h]
    def stage(c, slot, has_next=True):
        if has_next:
            nxt = logits(jnp.minimum(c + 1, qi))
            for h in range(B_HEADS):
                lg_refs[1 - slot][h] = nxt[h]
        accumulate(jnp.maximum(c - 1, 0), 1 - slot)
        mask = jnp.where(c == qi, causal_bias, 0.0)
        for h in range(B_HEADS):
            lg = lg_refs[slot][h] + mask
            m_old = m_ref[h]
            m_new = jnp.maximum(m_old, jnp.max(lg, axis=0, keepdims=True))
            m_ref[h] = m_new
            p_ref[slot, h] = jnp.exp2(lg - m_new).astype(BF16)
            al_ref[slot, h] = jnp.exp2(m_old - m_new)

    def pair_body(j, carry):
        stage(2 * j, 0)
        stage(2 * j + 1, 1)
        return carry

    n_chunks = qi + 1
    lax.fori_loop(0, n_chunks // 2, pair_body, 0)

    @pl.when(n_chunks % 2 == 1)
    def _():
        stage(qi, 0, has_next=False)

    accumulate(qi, qi % 2)
    for h in range(B_HEADS):
        acc = acc_ref[h]
        o_ref[0, :, h * V_DIM:(h + 1) * V_DIM] = (acc[:V_DIM] / acc[V_DIM:V_DIM + 1]).T.astype(BF16)


def _mla_attn(qt, k3, vt):
    bsz, seq, _ = k3.shape
    blk = BLOCK
    nq = seq // blk
    return pl.pallas_call(
        _mla_attn_kernel,
        grid=(bsz, nq),
        in_specs=[
            pl.BlockSpec((B_HEADS, MLA_QK, blk), lambda b, i: (0, 0, b * nq + i)),
            pl.BlockSpec((1, seq, B_HEADS * MLA_QK), lambda b, i: (b, 0, 0)),
            pl.BlockSpec((B_HEADS, nq, VT_ROWS, blk), lambda b, i: (0, b, 0, 0)),
        ],
        out_specs=pl.BlockSpec((1, blk, B_WIDTH), lambda b, i: (b, i, 0)),
        out_shape=jax.ShapeDtypeStruct((bsz, seq, B_WIDTH), BF16),
        scratch_shapes=[
            pltpu.VMEM((B_HEADS, blk, blk), F32),
            pltpu.VMEM((B_HEADS, blk, blk), F32),
            pltpu.VMEM((2, B_HEADS, blk, blk), BF16),
            pltpu.VMEM((2, B_HEADS, 1, blk), F32),
            pltpu.VMEM((B_HEADS, 1, blk), F32),
            pltpu.VMEM((B_HEADS, VT_ROWS, blk), F32),
        ],
        compiler_params=_params(("parallel", "arbitrary")),
        name="mla_attn",
    )(qt, k3, vt)


def _mem_attn_kernel(q_ref, kv_ref, o_ref):
    scale = HEAD_DIM ** -0.5
    for h in range(C_HEADS):
        q = q_ref[0, :, h * LANES:(h + 1) * LANES]
        k = kv_ref[0, :, h * LANES:(h + 1) * LANES]
        v = kv_ref[0, :, C_WIDTH + h * LANES:C_WIDTH + (h + 1) * LANES]
        s = lax.dot_general(q, k, (((1,), (1,)), ((), ())), preferred_element_type=F32) * scale
        p = jnp.exp(s - jnp.max(s, axis=-1, keepdims=True))
        l = jnp.sum(p, axis=-1, keepdims=True)
        o = jnp.dot(p.astype(BF16), v, preferred_element_type=F32) / l
        o_ref[0, :, h * LANES:(h + 1) * LANES] = o.astype(BF16)


def _mem_attn(proj3, mkv3, tq=512):
    bsz, seq, _ = proj3.shape
    mlen = mkv3.shape[1]
    return pl.pallas_call(
        _mem_attn_kernel,
        grid=(bsz, seq // tq),
        in_specs=[pl.BlockSpec((1, tq, C_WIDTH), lambda b, i: (b, i, BLK512_C)),
                  pl.BlockSpec((1, mlen, 2 * C_WIDTH), lambda b, i: (b, 0, 0))],
        out_specs=pl.BlockSpec((1, tq, C_WIDTH), lambda b, i: (b, i, 0)),
        out_shape=jax.ShapeDtypeStruct((bsz, seq, C_WIDTH), BF16),
        compiler_params=_params(("parallel", "arbitrary")),
        name="mem_attn",
    )(proj3, mkv3)


def _mix_out_kernel(oa_ref, ob_ref, oc_ref, ga_ref, gb_ref, gc_ref, x_ref, wb_ref, wo_ref, g_ref, b_ref,
                    o_ref, *, alpha):
    def gated(o_blk, gate_ref, r0, r1):
        y = jnp.dot(o_blk, wb_ref[0, r0:r1, :], preferred_element_type=F32)
        return y * (1.0 / (1.0 + jnp.exp(-gate_ref[...].astype(F32))))

    y = gated(oa_ref[...], ga_ref, 0, A_WIDTH)
    y = y + gated(ob_ref[...], gb_ref, A_WIDTH, A_WIDTH + B_WIDTH)
    y = y + gated(oc_ref[...], gc_ref, A_WIDTH + B_WIDTH, A_WIDTH + B_WIDTH + C_WIDTH)
    mix = jnp.dot(y.astype(BF16), wo_ref[0], preferred_element_type=F32)
    o_ref[...] = _layer_norm_rows(alpha * x_ref[...] + mix, g_ref[0], b_ref[0])


def _mix_out(oa, ob, oc, proj, x, wb, wo, ln_g, ln_b, layer, alpha, tm=256):
    n = x.shape[0]
    per_layer = lambda i: (layer, 0, 0)
    ln_row = lambda i: (3 * layer + 1, 0, 0)
    return pl.pallas_call(
        functools.partial(_mix_out_kernel, alpha=alpha),
        grid=(n // tm,),
        in_specs=[
            pl.BlockSpec((tm, A_WIDTH), lambda i: (i, 0)),
            pl.BlockSpec((tm, B_WIDTH), lambda i: (i, 0)),
            pl.BlockSpec((tm, C_WIDTH), lambda i: (i, 0)),
            pl.BlockSpec((tm, D_MODEL), lambda i: (i, BLK2048_GATE0)),
            pl.BlockSpec((tm, D_MODEL), lambda i: (i, BLK2048_GATE0 + 1)),
            pl.BlockSpec((tm, D_MODEL), lambda i: (i, BLK2048_GATE0 + 2)),
            pl.BlockSpec((tm, D_MODEL), lambda i: (i, 0)),
            pl.BlockSpec((1,) + wb.shape[1:], per_layer),
            pl.BlockSpec((1,) + wo.shape[1:], per_layer),
            pl.BlockSpec((1, 1, D_MODEL), ln_row),
            pl.BlockSpec((1, 1, D_MODEL), ln_row),
        ],
        out_specs=pl.BlockSpec((tm, D_MODEL), lambda i: (i, 0)),
        out_shape=jax.ShapeDtypeStruct((n, D_MODEL), F32),
        compiler_params=_params(("parallel",)),
        name="mix_out",
    )(oa, ob, oc, proj, proj, proj, x, wb, wo, ln_g, ln_b)


_IN_OFFS = tuple(sum(IN_SIZES[:k]) for k in range(len(IN_SIZES) + 1))


def _pack_w_in_t(w):
    wt = jnp.swapaxes(w, 1, 2)
    o = _IN_OFFS
    seg = lambda k: wt[:, o[k]:o[k + 1]]
    zeros = lambda r: jnp.zeros((w.shape[0], r, w.shape[1]), w.dtype)
    ik, iw, kr = seg(4), seg(5), seg(8)
    half = ROPE_DIM // 2
    packed = jnp.concatenate([
        wt[:, :o[4]],
        ik, ik,
        iw, zeros(LANES - IDX_HEADS),
        kr, zeros(LANES - ROPE_DIM),
        kr[:, half:], kr[:, :half], zeros(LANES - ROPE_DIM),
        seg(6), seg(7), seg(9), seg(10),
    ], axis=1)
    assert packed.shape[1] == PROJ_COLS
    return packed.astype(BF16)


def _pack_w_uq(w):
    half = ROPE_DIM // 2
    zeros = jnp.zeros(w.shape[:-1] + (LANES - ROPE_DIM,), w.dtype)
    cols = []
    for h in range(B_HEADS):
        base = h * (NOPE_DIM + ROPE_DIM)
        r = w[..., base + NOPE_DIM:base + NOPE_DIM + ROPE_DIM]
        cols += [w[..., base:base + NOPE_DIM], r, zeros, r[..., half:], r[..., :half], zeros]
    return jnp.concatenate(cols, axis=-1).astype(BF16)


def _pack_w_ukv(w):
    step = NOPE_DIM + V_DIM
    ks = [w[..., h * step:h * step + NOPE_DIM] for h in range(B_HEADS)]
    vs = [w[..., h * step + NOPE_DIM:(h + 1) * step] for h in range(B_HEADS)]
    return jnp.concatenate(ks + vs, axis=-1).astype(BF16)


def _rel_bucket(dist):
    n = jnp.maximum(dist, 0)
    max_exact = REL_BUCKETS // 2
    nf = jnp.maximum(n, 1).astype(F32)
    large = max_exact + (jnp.log(nf / max_exact) / math.log(REL_MAX_DIST / max_exact)
                         * (REL_BUCKETS - max_exact)).astype(I32)
    large = jnp.minimum(large, REL_BUCKETS - 1)
    return jnp.where(n < max_exact, n, large)


def _bias_tables(rel_bias):
    assert BLOCK + 1 >= REL_MAX_DIST
    s = jnp.arange(BLOCK)[:, None]
    t = jnp.arange(BLOCK)[None, :]
    tiles = [jnp.zeros((A_HEADS, BLOCK, BLOCK), F32)]
    for block_gap in (1, 0):
        dist = t - s + block_gap * BLOCK
        onehot = (_rel_bucket(dist)[:, :, None] == jnp.arange(REL_BUCKETS)).astype(F32)
        looked_up = jnp.einsum("stb,bh->sth", onehot, rel_bias, precision=lax.Precision.HIGHEST)
        tile = (looked_up - rel_bias[REL_BUCKETS - 1]) * LOG2E
        tile = jnp.where((dist >= 0)[:, :, None], tile, -jnp.inf)
        tiles.append(jnp.transpose(tile, (2, 0, 1)))
    return jnp.stack(tiles).astype(F32)


def kernel(x, mem, positions, rel_bias, ln_g, ln_b, ffn1_up, ffn1_down, w_in, q_norm, kv_norm, w_uq, w_ukv,
           w_mem_kv, w_branch, w_out, ffn2_up, ffn2_down):
    bsz, seq, d = x.shape
    depth = ffn1_up.shape[0]
    n = bsz * seq
    alpha = (2 * depth) ** 0.25
    k_sel = min(TOPK_MAX, seq // 4)
    assert d == D_MODEL and seq % (2 * BLOCK) == 0

    inv_freq = ROPE_THETA ** (-jnp.arange(0, ROPE_DIM, 2, dtype=F32) / ROPE_DIM)
    ang = positions.astype(F32)[..., None] * inv_freq
    cos, sin = jnp.cos(ang).reshape(n, -1), jnp.sin(ang).reshape(n, -1)
    pad = jnp.zeros((n, LANES - ROPE_DIM), F32)
    cc = jnp.concatenate([cos, cos, pad], axis=1)
    ss = jnp.concatenate([-sin, sin, pad], axis=1)
    bias_tab = _bias_tables(rel_bias)

    xf = x.reshape(n, d)
    memf = mem.reshape(bsz * mem.shape[1], d)
    ln_g3 = ln_g.reshape(depth * 3, 1, d)
    ln_b3 = ln_b.reshape(depth * 3, 1, d)
    qn3 = q_norm.reshape(depth, 1, Q_LORA)
    kvn3 = kv_norm.reshape(depth, 1, KV_LORA)
    w_in_p = _pack_w_in_t(w_in)
    w_uq_p = _pack_w_uq(w_uq.astype(BF16))
    w_ukv_p = _pack_w_ukv(w_ukv.astype(BF16))
    w_branch_b = w_branch.astype(BF16)
    w_out_b = w_out.astype(BF16)

    up_b, down_b = ffn1_up[0].astype(BF16), ffn1_down[0].astype(BF16)
    for l in range(depth):
        xf, up_b, down_b = _ffn(xf, up_b, down_b, ln_g3, ln_b3, l, 0, alpha, next_weights=(ffn2_up, ffn2_down, l))
        proj = _proj(xf, w_in_p, l, 1024, 2048, "in_proj", w_is_transposed=True)
        proj3 = proj.reshape(bsz, seq, PROJ_COLS)
        o_a = _dsa(proj3, bias_tab, k_sel).reshape(n, A_WIDTH)
        qt, k, vt = _mla_proj(proj, cc, ss, qn3, kvn3, w_uq_p, w_ukv_p, l)
        o_b = _mla_attn(qt, k.reshape(bsz, seq, -1), vt).reshape(n, B_WIDTH)
        mkv = _proj(memf, w_mem_kv, l, 512, 1024, "mem_proj")
        o_c = _mem_attn(proj3, mkv.reshape(bsz, mem.shape[1], 2 * C_WIDTH)).reshape(n, C_WIDTH)
        xf = _mix_out(o_a, o_b, o_c, proj, xf, w_branch_b, w_out_b, ln_g3, ln_b3, l, alpha)
        upcoming = (ffn1_up, ffn1_down, l + 1) if l + 1 < depth else None
        xf, up_b, down_b = _ffn(xf, up_b, down_b, ln_g3, ln_b3, l, 2, alpha, next_weights=upcoming)
    return xf.reshape(bsz, seq, d)
```

```python
import functools
import math

import jax
import jax.numpy as jnp
from jax import lax
from jax.experimental import pallas as pl
from jax.experimental.pallas import tpu as pltpu

F32 = jnp.float32
BF16 = jnp.bfloat16
I32 = jnp.int32
I16 = jnp.int16

D_MODEL = 2048
D_FF = 5632
HEAD_DIM = 128
A_HEADS = 6
IDX_HEADS = 16
IDX_DIM = 64
TOPK_MAX = 256
B_HEADS = 6
Q_LORA = 512
KV_LORA = 512
NOPE_DIM = 128
ROPE_DIM = 64
V_DIM = 128
ROPE_THETA = 10000.0
C_HEADS = 4
REL_BUCKETS = 32
REL_MAX_DIST = 128
LN_EPS = 1e-5
RMS_EPS = 1e-6
A_WIDTH = A_HEADS * HEAD_DIM
B_WIDTH = B_HEADS * V_DIM
C_WIDTH = C_HEADS * HEAD_DIM
IN_SIZES = (A_WIDTH, HEAD_DIM, HEAD_DIM, IDX_HEADS * IDX_DIM, IDX_DIM, IDX_HEADS,
            Q_LORA, KV_LORA, ROPE_DIM, C_WIDTH, 3 * D_MODEL)

LANES = 128
SUBLANES = 8
BLOCK = 256
MLA_QK = 2 * LANES
ONES_ROWS = 16
VT_ROWS = HEAD_DIM + ONES_ROWS
INT_MIN = -2 ** 31
HALF_RANGE = 2 ** 15
LOG2E = 1.4426950408889634

PROJ_COLS = 10240
BLK_AK, BLK_AV = 6, 7
BLK_IK, BLK_IW, BLK_KRA, BLK_KRB = 16, 17, 18, 19
BLK512_CQ, BLK512_CKV, BLK512_C = 5, 6, 7
BLK2048_GATE0 = 2

VMEM_LIMIT = 56 * 1024 * 1024
FFN_VMEM_LIMIT = 62 * 1024 * 1024


def _params(sem, vmem=VMEM_LIMIT):
    return pltpu.CompilerParams(dimension_semantics=sem, vmem_limit_bytes=vmem)


def _layer_norm_rows(y, g, b, scale=1.0):
    mu = jnp.mean(y, axis=-1, keepdims=True)
    d = y - mu
    var = jnp.mean(d * d, axis=-1, keepdims=True)
    return d * (scale * lax.rsqrt(scale * scale * var + LN_EPS)) * g + b


def _ffn_kernel(*refs, alpha, casts_next):
    if casts_next:
        x_ref, wg_ref, wu_ref, wd_ref, g_ref, b_ref, next_up_ref, next_down_ref, o_ref, up_b_ref, down_b_ref, xb_ref = refs
    else:
        x_ref, wg_ref, wu_ref, wd_ref, g_ref, b_ref, o_ref, xb_ref = refs
    j = pl.program_id(1)

    @pl.when(j == 0)
    def _():
        x = x_ref[...]
        xb_ref[...] = x.astype(BF16)
        o_ref[...] = (2.0 * alpha) * x

    xb = xb_ref[...]
    gate = jnp.dot(xb, wg_ref[...], preferred_element_type=F32)
    up = jnp.dot(xb, wu_ref[...], preferred_element_type=F32)
    h = (gate * (1.0 / (1.0 + jnp.exp(-gate))) * up).astype(BF16)
    o_ref[...] += jnp.dot(h, wd_ref[...], preferred_element_type=F32)

    if casts_next:
        up_b_ref[...] = next_up_ref[0].astype(BF16)
        down_b_ref[...] = next_down_ref[0].astype(BF16)

    @pl.when(j == pl.num_programs(1) - 1)
    def _():
        o_ref[...] = _layer_norm_rows(o_ref[...], g_ref[0], b_ref[0], scale=0.5)


def _ffn(x, w_up_b, w_down_b, ln_g, ln_b, layer, which, alpha, next_weights=None, tm=1024, tf=512):
    n = x.shape[0]
    tm = min(tm, n)
    nf = D_FF // tf
    ni = n // tm
    casts_next = next_weights is not None
    in_specs = [
        pl.BlockSpec((tm, D_MODEL), lambda i, j: (i, 0)),
        pl.BlockSpec((D_MODEL, tf), lambda i, j: (0, j)),
        pl.BlockSpec((D_MODEL, tf), lambda i, j: (0, j + nf)),
        pl.BlockSpec((tf, D_MODEL), lambda i, j: (j, 0)),
        pl.BlockSpec((1, 1, D_MODEL), lambda i, j: (3 * layer + which, 0, 0)),
        pl.BlockSpec((1, 1, D_MODEL), lambda i, j: (3 * layer + which, 0, 0)),
    ]
    out_specs = [pl.BlockSpec((tm, D_MODEL), lambda i, j: (i, 0))]
    out_shape = [jax.ShapeDtypeStruct((n, D_MODEL), F32)]
    operands = [x, w_up_b, w_up_b, w_down_b, ln_g, ln_b]
    if casts_next:
        next_up, next_down, next_layer = next_weights
        up_slab = (D_MODEL // ni, 2 * D_FF // nf)
        down_slab = (D_FF // nf, D_MODEL // ni)
        in_specs += [pl.BlockSpec((1,) + up_slab, lambda i, j: (next_layer, i, j)),
                     pl.BlockSpec((1,) + down_slab, lambda i, j: (next_layer, j, i))]
        out_specs += [pl.BlockSpec(up_slab, lambda i, j: (i, j)), pl.BlockSpec(down_slab, lambda i, j: (j, i))]
        out_shape += [jax.ShapeDtypeStruct(next_up.shape[1:], BF16), jax.ShapeDtypeStruct(next_down.shape[1:], BF16)]
        operands += [next_up, next_down]
    outs = pl.pallas_call(
        functools.partial(_ffn_kernel, alpha=alpha, casts_next=casts_next),
        grid=(ni, nf),
        in_specs=in_specs,
        out_specs=out_specs,
        out_shape=out_shape,
        scratch_shapes=[pltpu.VMEM((tm, D_MODEL), BF16)],
        compiler_params=_params(("parallel", "arbitrary"), FFN_VMEM_LIMIT),
        name="ffn",
    )(*operands)
    return outs if casts_next else (outs[0], None, None)


def _proj_kernel(x_ref, w_ref, o_ref, xb_ref, *, w_is_transposed):
    @pl.when(pl.program_id(1) == 0)
    def _():
        xb_ref[...] = x_ref[...].astype(BF16)

    contract_w = 1 if w_is_transposed else 0
    o_ref[...] = lax.dot_general(xb_ref[...], w_ref[0].astype(BF16), (((1,), (contract_w,)), ((), ())),
                                 preferred_element_type=F32).astype(o_ref.dtype)


def _proj(x, w, layer, tm, tn, name, w_is_transposed=False):
    n, k = x.shape
    tm = min(tm, n)
    cols = w.shape[1] if w_is_transposed else w.shape[2]
    w_spec = (pl.BlockSpec((1, tn, k), lambda i, j: (layer, j, 0)) if w_is_transposed
              else pl.BlockSpec((1, k, tn), lambda i, j: (layer, 0, j)))
    return pl.pallas_call(
        functools.partial(_proj_kernel, w_is_transposed=w_is_transposed),
        grid=(n // tm, cols // tn),
        in_specs=[pl.BlockSpec((tm, k), lambda i, j: (i, 0)), w_spec],
        out_specs=pl.BlockSpec((tm, tn), lambda i, j: (i, j)),
        out_shape=jax.ShapeDtypeStruct((n, cols), BF16),
        scratch_shapes=[pltpu.VMEM((tm, k), BF16)],
        compiler_params=_params(("parallel", "arbitrary")),
        name=name,
    )(x, w)


def _dsa_kernel(aq_ref, iq_ref, iw_ref, ik_ref, ak_ref, av_ref, bias_ref, o_ref,
                skey_ref, hi_ref, lo_ref, ika_ref, ikb_ref, vat_ref, m_ref, acc_ref,
                lg0_ref, lg1_ref, p_ref, al_ref, *, k_sel, seq):
    i = pl.program_id(1)
    blk = BLOCK
    lg_refs = (lg0_ref, lg1_ref)

    @pl.when(i == 0)
    def _():
        ik = ik_ref[0]
        lane = lax.broadcasted_iota(I32, ik.shape, 1)
        zero = jnp.zeros_like(ik)
        ika_ref[...] = jnp.where(lane < IDX_DIM, ik, zero)
        ikb_ref[...] = jnp.where(lane >= IDX_DIM, ik, zero)
        for c in range(seq // blk):
            vat_ref[c, 0:HEAD_DIM, :] = av_ref[0, c * blk:(c + 1) * blk, :].astype(F32).T.astype(BF16)
            vat_ref[c, HEAD_DIM:VT_ROWS, :] = jnp.ones((ONES_ROWS, blk), BF16)

    iqt = iq_ref[0].astype(F32).T.astype(BF16)
    wt = iw_ref[0].astype(F32).T * (IDX_HEADS * IDX_DIM) ** -0.5

    srow = lax.broadcasted_iota(I32, (blk, blk), 0)
    tcol = lax.broadcasted_iota(I32, (blk, blk), 1)

    def score_chunk(c):
        s0 = pl.multiple_of(c * blk, blk)
        ka = ika_ref[pl.ds(s0, blk), :]
        kb = ikb_ref[pl.ds(s0, blk), :]
        sc = jnp.zeros((blk, blk), F32)
        for j in range(IDX_HEADS // 2):
            rhs = iqt[j * LANES:(j + 1) * LANES, :]
            sc = sc + wt[2 * j:2 * j + 1, :] * jnp.maximum(jnp.dot(ka, rhs, preferred_element_type=F32), 0.0)
            sc = sc + wt[2 * j + 1:2 * j + 2, :] * jnp.maximum(jnp.dot(kb, rhs, preferred_element_type=F32), 0.0)
        bits = pltpu.bitcast(sc, I32)
        key = bits ^ ((bits >> 31) & jnp.int32(0x7FFFFFFF))
        causal = srow <= tcol + jnp.where(c < i, jnp.int32(blk), jnp.int32(0))
        key = jnp.where(causal, key, jnp.int32(INT_MIN))
        skey_ref[pl.ds(s0, blk), :] = key
        hi_ref[pl.ds(s0, blk), :] = (key >> 16).astype(I16)
        lo_ref[pl.ds(s0, blk), :] = ((key & jnp.int32(0xFFFF)) - HALF_RANGE).astype(I16)

    def score_pair(j, carry):
        score_chunk(2 * j)
        score_chunk(2 * j + 1)
        return carry

    lax.fori_loop(0, (i + 1) // 2, score_pair, 0)

    @pl.when((i + 1) % 2 == 1)
    def _():
        score_chunk(i)

    n_acc = 4
    rows16 = 2 * SUBLANES

    def count_at_least(ref, trial):
        trial16 = jnp.broadcast_to(trial.astype(I16), (rows16, blk))

        def cnt_body(c, accs):
            vals = ref[pl.ds(pl.multiple_of(c * blk, blk), blk), :]
            accs = list(accs)
            for r in range(blk // rows16):
                rows = vals[r * rows16:(r + 1) * rows16, :]
                accs[r % n_acc] = jnp.where(rows >= trial16, accs[r % n_acc] + jnp.int16(1), accs[r % n_acc])
            return tuple(accs)

        zero = jnp.zeros((rows16, blk), I16)
        accs = lax.fori_loop(0, i + 1, cnt_body, (zero,) * n_acc)
        total = (accs[0] + accs[1]) + (accs[2] + accs[3])
        return jnp.sum(total.astype(I32), axis=0, keepdims=True)

    def largest_with_count(ref, need):
        def bit_body(it, u):
            trial_u = u | jnp.left_shift(jnp.int32(1), jnp.int32(15) - it)
            return jnp.where(count_at_least(ref, trial_u - HALF_RANGE) >= need, trial_u, u)

        return lax.fori_loop(0, 16, bit_body, jnp.zeros((1, blk), I32))

    hi_thr = largest_with_count(hi_ref, k_sel) - HALF_RANGE
    above = jnp.where(hi_thr == HALF_RANGE - 1, 0, count_at_least(hi_ref, jnp.minimum(hi_thr + 1, HALF_RANGE - 1)))
    hi_thr16 = jnp.broadcast_to(hi_thr.astype(I16), (blk, blk))

    def keep_candidates(c, carry):
        rows = pl.ds(pl.multiple_of(c * blk, blk), blk)
        lo_ref[rows, :] = jnp.where(hi_ref[rows, :] == hi_thr16, lo_ref[rows, :], jnp.int16(-HALF_RANGE))
        return carry

    lax.fori_loop(0, i + 1, keep_candidates, 0)
    lo_thr = largest_with_count(lo_ref, k_sel - above)
    thr = hi_thr * (2 * HALF_RANGE) + lo_thr

    def count_keys_at_least(trial):
        trial8 = jnp.broadcast_to(trial, (SUBLANES, blk))

        def cnt_body(c, acc):
            keys = skey_ref[pl.ds(pl.multiple_of(c * blk, blk), blk), :]
            for r in range(blk // SUBLANES):
                acc = acc + jnp.where(keys[r * SUBLANES:(r + 1) * SUBLANES, :] >= trial8, 1, 0)
            return acc

        acc = lax.fori_loop(0, i + 1, cnt_body, jnp.zeros((SUBLANES, blk), I32))
        return jnp.sum(acc, axis=0, keepdims=True)

    live = thr != jnp.int32(INT_MIN)
    surplus = jnp.where(live, count_keys_at_least(thr) - k_sel, 0)

    @pl.when(jnp.max(surplus.astype(F32)) > 0.0)
    def _():
        above_thr = jnp.where(thr == jnp.int32(-INT_MIN - 1), 0,
                              count_keys_at_least(jnp.minimum(thr, jnp.int32(-INT_MIN - 2)) + 1))
        allowed = (k_sel - above_thr).astype(F32)
        prefix_ones = jnp.where(tcol <= srow, 1.0, 0.0).astype(BF16)

        def demote(c, seen):
            rows = pl.ds(pl.multiple_of(c * blk, blk), blk)
            keys = skey_ref[rows, :]
            tied = jnp.logical_and(keys == thr, live)
            rank = seen + jnp.dot(prefix_ones, jnp.where(tied, 1.0, 0.0).astype(BF16), preferred_element_type=F32)
            skey_ref[rows, :] = jnp.where(jnp.logical_and(tied, rank > allowed), jnp.int32(INT_MIN), keys)
            return rank[blk - 1:blk, :]

        lax.fori_loop(0, i + 1, demote, jnp.zeros((1, blk), F32))

    qt = (aq_ref[0].astype(F32) * (HEAD_DIM ** -0.5 * LOG2E)).T.astype(BF16)

    m_ref[...] = jnp.full_like(m_ref, -jnp.inf)
    acc_ref[...] = jnp.zeros_like(acc_ref)

    def logits(c):
        ka = ak_ref[0, pl.ds(pl.multiple_of(c * blk, blk), blk), :]
        return [jnp.dot(ka, qt[h * LANES:(h + 1) * LANES, :], preferred_element_type=F32) for h in range(A_HEADS)]

    def accumulate(c, slot):
        vt = vat_ref[c]
        for h in range(A_HEADS):
            pv = jnp.dot(vt, p_ref[slot, h], preferred_element_type=F32)
            acc_ref[h] = acc_ref[h] * al_ref[slot, h] + pv

    p_ref[1] = jnp.zeros_like(p_ref[1])
    al_ref[1] = jnp.ones_like(al_ref[1])
    first = logits(0)
    for h in range(A_HEADS):
        lg0_ref[h] = first[h]

    def stage(c, slot, has_next=True):
        if has_next:
            nxt = logits(jnp.minimum(c + 1, i))
            for h in range(A_HEADS):
                lg_refs[1 - slot][h] = nxt[h]
        accumulate(jnp.maximum(c - 1, 0), 1 - slot)
        kind = jnp.clip(c - i + 2, 0, 2)
        sel = skey_ref[pl.ds(pl.multiple_of(c * blk, blk), blk), :] >= thr
        for h in range(A_HEADS):
            lg = jnp.where(sel, lg_refs[slot][h] + bias_ref[kind, h], -jnp.inf)
            m_old = m_ref[h]
            m_new = jnp.maximum(m_old, jnp.max(lg, axis=0, keepdims=True))
            m_use = jnp.where(m_new == -jnp.inf, 0.0, m_new)
            alpha = jnp.exp2(m_old - m_use)
            m_ref[h] = m_new
            p_ref[slot, h] = jnp.exp2(lg - m_use).astype(BF16)
            al_ref[slot, h] = alpha

    def pair_body(j, carry):
        stage(2 * j, 0)
        stage(2 * j + 1, 1)
        return carry

    n_chunks = i + 1
    lax.fori_loop(0, n_chunks // 2, pair_body, 0)

    @pl.when(n_chunks % 2 == 1)
    def _():
        stage(i, 0, has_next=False)

    accumulate(i, i % 2)

    for h in range(A_HEADS):
        acc = acc_ref[h]
        o_ref[0, :, h * LANES:(h + 1) * LANES] = (acc[:HEAD_DIM] / acc[HEAD_DIM:HEAD_DIM + 1]).T.astype(BF16)


def _dsa(proj3, bias_tab, k_sel):
    bsz, seq, _ = proj3.shape
    blk = BLOCK
    return pl.pallas_call(
        functools.partial(_dsa_kernel, k_sel=k_sel, seq=seq),
        grid=(bsz, seq // blk),
        in_specs=[
            pl.BlockSpec((1, blk, A_WIDTH), lambda b, i: (b, i, 0)),
            pl.BlockSpec((1, blk, IDX_HEADS * IDX_DIM), lambda b, i: (b, i, 1)),
            pl.BlockSpec((1, blk, LANES), lambda b, i: (b, i, BLK_IW)),
            pl.BlockSpec((1, seq, LANES), lambda b, i: (b, 0, BLK_IK)),
            pl.BlockSpec((1, seq, LANES), lambda b, i: (b, 0, BLK_AK)),
            pl.BlockSpec((1, seq, LANES), lambda b, i: (b, 0, BLK_AV)),
            pl.BlockSpec(bias_tab.shape, lambda b, i: (0, 0, 0, 0)),
        ],
        out_specs=pl.BlockSpec((1, blk, A_WIDTH), lambda b, i: (b, i, 0)),
        out_shape=jax.ShapeDtypeStruct((bsz, seq, A_WIDTH), BF16),
        scratch_shapes=[
            pltpu.VMEM((seq, blk), I32),
            pltpu.VMEM((seq, blk), I16),
            pltpu.VMEM((seq, blk), I16),
            pltpu.VMEM((seq, LANES), BF16),
            pltpu.VMEM((seq, LANES), BF16),
            pltpu.VMEM((seq // blk, VT_ROWS, blk), BF16),
            pltpu.VMEM((A_HEADS, 1, blk), F32),
            pltpu.VMEM((A_HEADS, VT_ROWS, blk), F32),
            pltpu.VMEM((A_HEADS, blk, blk), F32),
            pltpu.VMEM((A_HEADS, blk, blk), F32),
            pltpu.VMEM((2, A_HEADS, blk, blk), BF16),
            pltpu.VMEM((2, A_HEADS, 1, blk), F32),
        ],
        compiler_params=_params(("parallel", "arbitrary")),
        name="dsa",
    )(proj3, proj3, proj3, proj3, proj3, proj3, bias_tab)


def _rms_rows(x, g):
    return x * lax.rsqrt(jnp.mean(x * x, axis=-1, keepdims=True) + RMS_EPS) * g


def _mla_proj_kernel(cq_ref, ckv_ref, kra_ref, krb_ref, cc_ref, ss_ref, qn_ref, kvn_ref, wq_ref, wkv_ref,
                     qt_ref, k_ref, vt_ref):
    tm = cq_ref.shape[0]
    cc = cc_ref[...]
    ss = ss_ref[...]
    qscale = (NOPE_DIM + ROPE_DIM) ** -0.5 * LOG2E
    cq = _rms_rows(cq_ref[...].astype(F32), qn_ref[0]).astype(BF16)
    q3 = jnp.dot(cq, wq_ref[0], preferred_element_type=F32)
    for h in range(B_HEADS):
        base = 3 * LANES * h
        rot = q3[:, base + LANES:base + 2 * LANES] * cc + q3[:, base + 2 * LANES:base + 3 * LANES] * ss
        qt_ref[h, 0:LANES, :] = (q3[:, base:base + LANES] * qscale).T.astype(BF16)
        qt_ref[h, LANES:MLA_QK, :] = (rot * qscale).T.astype(BF16)
    ckv = _rms_rows(ckv_ref[...].astype(F32), kvn_ref[0]).astype(BF16)
    kv = jnp.dot(ckv, wkv_ref[0], preferred_element_type=F32)
    krot = (kra_ref[...].astype(F32) * cc + krb_ref[...].astype(F32) * ss).astype(BF16)
    for h in range(B_HEADS):
        k_ref[:, MLA_QK * h:MLA_QK * h + LANES] = kv[:, h * LANES:(h + 1) * LANES].astype(BF16)
        k_ref[:, MLA_QK * h + LANES:MLA_QK * (h + 1)] = krot
        v_h = kv[:, B_HEADS * NOPE_DIM + h * V_DIM:B_HEADS * NOPE_DIM + (h + 1) * V_DIM]
        for c in range(tm // BLOCK):
            vt_ref[h, c, 0:V_DIM, :] = v_h[c * BLOCK:(c + 1) * BLOCK, :].T.astype(BF16)
            vt_ref[h, c, V_DIM:VT_ROWS, :] = jnp.ones((ONES_ROWS, BLOCK), BF16)


def _mla_proj(proj, cc, ss, qn, kvn, wq, wkv, layer, tm=512):
    n = proj.shape[0]
    per_layer = lambda i: (layer, 0, 0)
    return pl.pallas_call(
        _mla_proj_kernel,
        grid=(n // tm,),
        in_specs=[
            pl.BlockSpec((tm, Q_LORA), lambda i: (i, BLK512_CQ)),
            pl.BlockSpec((tm, KV_LORA), lambda i: (i, BLK512_CKV)),
            pl.BlockSpec((tm, LANES), lambda i: (i, BLK_KRA)),
            pl.BlockSpec((tm, LANES), lambda i: (i, BLK_KRB)),
            pl.BlockSpec((tm, LANES), lambda i: (i, 0)),
            pl.BlockSpec((tm, LANES), lambda i: (i, 0)),
            pl.BlockSpec((1, 1, Q_LORA), per_layer),
            pl.BlockSpec((1, 1, KV_LORA), per_layer),
            pl.BlockSpec((1,) + wq.shape[1:], per_layer),
            pl.BlockSpec((1,) + wkv.shape[1:], per_layer),
        ],
        out_specs=[
            pl.BlockSpec((B_HEADS, MLA_QK, tm), lambda i: (0, 0, i)),
            pl.BlockSpec((tm, B_HEADS * MLA_QK), lambda i: (i, 0)),
            pl.BlockSpec((B_HEADS, tm // BLOCK, VT_ROWS, BLOCK), lambda i: (0, i, 0, 0)),
        ],
        out_shape=[jax.ShapeDtypeStruct((B_HEADS, MLA_QK, n), BF16),
                   jax.ShapeDtypeStruct((n, B_HEADS * MLA_QK), BF16),
                   jax.ShapeDtypeStruct((B_HEADS, n // BLOCK, VT_ROWS, BLOCK), BF16)],
        compiler_params=_params(("parallel",)),
        name="mla_proj",
    )(proj, proj, proj, proj, cc, ss, qn, kvn, wq, wkv)


def _mla_attn_kernel(qt_ref, k_ref, vt_ref, o_ref, lg0_ref, lg1_ref, p_ref, al_ref, m_ref, acc_ref):
    lg_refs = (lg0_ref, lg1_ref)
    qi = pl.program_id(1)
    blk = BLOCK
    srow = lax.broadcasted_iota(I32, (blk, blk), 0)
    tcol = lax.broadcasted_iota(I32, (blk, blk), 1)
    causal_bias = jnp.where(srow <= tcol, 0.0, -jnp.inf).astype(F32)

    def logits(c):
        s0 = pl.multiple_of(c * blk, blk)
        return [jnp.dot(k_ref[0, pl.ds(s0, blk), h * MLA_QK:(h + 1) * MLA_QK], qt_ref[h],
                        preferred_element_type=F32) for h in range(B_HEADS)]

    def accumulate(c, slot):
        for h in range(B_HEADS):
            pv = jnp.dot(vt_ref[h, c], p_ref[slot, h], preferred_element_type=F32)
            acc_ref[h] = acc_ref[h] * al_ref[slot, h] + pv

    m_ref[...] = jnp.full_like(m_ref, -jnp.inf)
    acc_ref[...] = jnp.zeros_like(acc_ref)
    p_ref[1] = jnp.zeros_like(p_ref[1])
    al_ref[1] = jnp.ones_like(al_ref[1])
    first = logits(0)
    for h in range(B_HEADS):
        lg0_ref[h] = first[h]
    def stage(c, slot, has_next=True):
        if has_next:
            nxt = logits(jnp.minimum(c + 1, qi))
            for h in range(B_HEADS):
                lg_refs[1 - slot][h] = nxt[h]
        accumulate(jnp.maximum(c - 1, 0), 1 - slot)
        mask = jnp.where(c == qi, causal_bias, 0.0)
        for h in range(B_HEADS):
            lg = lg_refs[slot][h] + mask
            m_old = m_ref[h]
            m_new = jnp.maximum(m_old, jnp.max(lg, axis=0, keepdims=True))
            m_ref[h] = m_new
            p_ref[slot, h] = jnp.exp2(lg - m_new).astype(BF16)
            al_ref[slot, h] = jnp.exp2(m_old - m_new)

    def pair_body(j, carry):
        stage(2 * j, 0)
        stage(2 * j + 1, 1)
        return carry

    n_chunks = qi + 1
    lax.fori_loop(0, n_chunks // 2, pair_body, 0)

    @pl.when(n_chunks % 2 == 1)
    def _():
        stage(qi, 0, has_next=False)

    accumulate(qi, qi % 2)
    for h in range(B_HEADS):
        acc = acc_ref[h]
        o_ref[0, :, h * V_DIM:(h + 1) * V_DIM] = (acc[:V_DIM] / acc[V_DIM:V_DIM + 1]).T.astype(BF16)


def _mla_attn(qt, k3, vt):
    bsz, seq, _ = k3.shape
    blk = BLOCK
    nq = seq // blk
    return pl.pallas_call(
        _mla_attn_kernel,
        grid=(bsz, nq),
        in_specs=[
            pl.BlockSpec((B_HEADS, MLA_QK, blk), lambda b, i: (0, 0, b * nq + i)),
            pl.BlockSpec((1, seq, B_HEADS * MLA_QK), lambda b, i: (b, 0, 0)),
            pl.BlockSpec((B_HEADS, nq, VT_ROWS, blk), lambda b, i: (0, b, 0, 0)),
        ],
        out_specs=pl.BlockSpec((1, blk, B_WIDTH), lambda b, i: (b, i, 0)),
        out_shape=jax.ShapeDtypeStruct((bsz, seq, B_WIDTH), BF16),
        scratch_shapes=[
            pltpu.VMEM((B_HEADS, blk, blk), F32),
            pltpu.VMEM((B_HEADS, blk, blk), F32),
            pltpu.VMEM((2, B_HEADS, blk, blk), BF16),
            pltpu.VMEM((2, B_HEADS, 1, blk), F32),
            pltpu.VMEM((B_HEADS, 1, blk), F32),
            pltpu.VMEM((B_HEADS, VT_ROWS, blk), F32),
        ],
        compiler_params=_params(("parallel", "arbitrary")),
        name="mla_attn",
    )(qt, k3, vt)


def _mem_attn_rows(q_ref, kv_ref):
    scale = HEAD_DIM ** -0.5
    heads = []
    for h in range(C_HEADS):
        q = q_ref[:, h * LANES:(h + 1) * LANES]
        k = kv_ref[0, :, h * LANES:(h + 1) * LANES]
        v = kv_ref[0, :, C_WIDTH + h * LANES:C_WIDTH + (h + 1) * LANES]
        s = lax.dot_general(q, k, (((1,), (1,)), ((), ())), preferred_element_type=F32) * scale
        p = jnp.exp(s - jnp.max(s, axis=-1, keepdims=True))
        l = jnp.sum(p, axis=-1, keepdims=True)
        heads.append((jnp.dot(p.astype(BF16), v, preferred_element_type=F32) / l).astype(BF16))
    return jnp.concatenate(heads, axis=1)


def _mix_out_kernel(oa_ref, ob_ref, cq_ref, mkv_ref, ga_ref, gb_ref, gc_ref, x_ref, wb_ref, wo_ref, g_ref, b_ref,
                    o_ref, *, alpha):
    def gated(o_blk, gate_ref, r0, r1):
        y = jnp.dot(o_blk, wb_ref[0, r0:r1, :], preferred_element_type=F32)
        return y * (1.0 / (1.0 + jnp.exp(-gate_ref[...].astype(F32))))

    y = gated(oa_ref[...], ga_ref, 0, A_WIDTH)
    y = y + gated(ob_ref[...], gb_ref, A_WIDTH, A_WIDTH + B_WIDTH)
    y = y + gated(_mem_attn_rows(cq_ref, mkv_ref), gc_ref, A_WIDTH + B_WIDTH, A_WIDTH + B_WIDTH + C_WIDTH)
    mix = jnp.dot(y.astype(BF16), wo_ref[0], preferred_element_type=F32)
    o_ref[...] = _layer_norm_rows(alpha * x_ref[...] + mix, g_ref[0], b_ref[0])


def _mix_out(oa, ob, mkv3, proj, x, wb, wo, ln_g, ln_b, layer, alpha, tm=256):
    n = x.shape[0]
    tiles_per_seq = n // mkv3.shape[0] // tm
    per_layer = lambda i: (layer, 0, 0)
    ln_row = lambda i: (3 * layer + 1, 0, 0)
    return pl.pallas_call(
        functools.partial(_mix_out_kernel, alpha=alpha),
        grid=(n // tm,),
        in_specs=[
            pl.BlockSpec((tm, A_WIDTH), lambda i: (i, 0)),
            pl.BlockSpec((tm, B_WIDTH), lambda i: (i, 0)),
            pl.BlockSpec((tm, C_WIDTH), lambda i: (i, BLK512_C)),
            pl.BlockSpec((1,) + mkv3.shape[1:], lambda i: (i // tiles_per_seq, 0, 0)),
            pl.BlockSpec((tm, D_MODEL), lambda i: (i, BLK2048_GATE0)),
            pl.BlockSpec((tm, D_MODEL), lambda i: (i, BLK2048_GATE0 + 1)),
            pl.BlockSpec((tm, D_MODEL), lambda i: (i, BLK2048_GATE0 + 2)),
            pl.BlockSpec((tm, D_MODEL), lambda i: (i, 0)),
            pl.BlockSpec((1,) + wb.shape[1:], per_layer),
            pl.BlockSpec((1,) + wo.shape[1:], per_layer),
            pl.BlockSpec((1, 1, D_MODEL), ln_row),
            pl.BlockSpec((1, 1, D_MODEL), ln_row),
        ],
        out_specs=pl.BlockSpec((tm, D_MODEL), lambda i: (i, 0)),
        out_shape=jax.ShapeDtypeStruct((n, D_MODEL), F32),
        compiler_params=_params(("parallel",)),
        name="mix_out",
    )(oa, ob, proj, mkv3, proj, proj, proj, x, wb, wo, ln_g, ln_b)


_IN_OFFS = tuple(sum(IN_SIZES[:k]) for k in range(len(IN_SIZES) + 1))


def _pack_w_in_t(w):
    wt = jnp.swapaxes(w, 1, 2)
    o = _IN_OFFS
    seg = lambda k: wt[:, o[k]:o[k + 1]]
    zeros = lambda r: jnp.zeros((w.shape[0], r, w.shape[1]), w.dtype)
    ik, iw, kr = seg(4), seg(5), seg(8)
    half = ROPE_DIM // 2
    packed = jnp.concatenate([
        wt[:, :o[4]],
        ik, ik,
        iw, zeros(LANES - IDX_HEADS),
        kr, zeros(LANES - ROPE_DIM),
        kr[:, half:], kr[:, :half], zeros(LANES - ROPE_DIM),
        seg(6), seg(7), seg(9), seg(10),
    ], axis=1)
    assert packed.shape[1] == PROJ_COLS
    return packed.astype(BF16)


def _pack_w_uq(w):
    half = ROPE_DIM // 2
    zeros = jnp.zeros(w.shape[:-1] + (LANES - ROPE_DIM,), w.dtype)
    cols = []
    for h in range(B_HEADS):
        base = h * (NOPE_DIM + ROPE_DIM)
        r = w[..., base + NOPE_DIM:base + NOPE_DIM + ROPE_DIM]
        cols += [w[..., base:base + NOPE_DIM], r, zeros, r[..., half:], r[..., :half], zeros]
    return jnp.concatenate(cols, axis=-1).astype(BF16)


def _pack_w_ukv(w):
    step = NOPE_DIM + V_DIM
    ks = [w[..., h * step:h * step + NOPE_DIM] for h in range(B_HEADS)]
    vs = [w[..., h * step + NOPE_DIM:(h + 1) * step] for h in range(B_HEADS)]
    return jnp.concatenate(ks + vs, axis=-1).astype(BF16)


def _rel_bucket(dist):
    n = jnp.maximum(dist, 0)
    max_exact = REL_BUCKETS // 2
    nf = jnp.maximum(n, 1).astype(F32)
    large = max_exact + (jnp.log(nf / max_exact) / math.log(REL_MAX_DIST / max_exact)
                         * (REL_BUCKETS - max_exact)).astype(I32)
    large = jnp.minimum(large, REL_BUCKETS - 1)
    return jnp.where(n < max_exact, n, large)


def _bias_tables(rel_bias):
    assert BLOCK + 1 >= REL_MAX_DIST
    s = jnp.arange(BLOCK)[:, None]
    t = jnp.arange(BLOCK)[None, :]
    tiles = [jnp.zeros((A_HEADS, BLOCK, BLOCK), F32)]
    for block_gap in (1, 0):
        dist = t - s + block_gap * BLOCK
        onehot = (_rel_bucket(dist)[:, :, None] == jnp.arange(REL_BUCKETS)).astype(F32)
        looked_up = jnp.einsum("stb,bh->sth", onehot, rel_bias, precision=lax.Precision.HIGHEST)
        tile = (looked_up - rel_bias[REL_BUCKETS - 1]) * LOG2E
        tile = jnp.where((dist >= 0)[:, :, None], tile, -jnp.inf)
        tiles.append(jnp.transpose(tile, (2, 0, 1)))
    return jnp.stack(tiles).astype(F32)


def kernel(x, mem, positions, rel_bias, ln_g, ln_b, ffn1_up, ffn1_down, w_in, q_norm, kv_norm, w_uq, w_ukv,
           w_mem_kv, w_branch, w_out, ffn2_up, ffn2_down):
    bsz, seq, d = x.shape
    depth = ffn1_up.shape[0]
    n = bsz * seq
    alpha = (2 * depth) ** 0.25
    k_sel = min(TOPK_MAX, seq // 4)
    assert d == D_MODEL and seq % (2 * BLOCK) == 0

    inv_freq = ROPE_THETA ** (-jnp.arange(0, ROPE_DIM, 2, dtype=F32) / ROPE_DIM)
    ang = positions.astype(F32)[..., None] * inv_freq
    cos, sin = jnp.cos(ang).reshape(n, -1), jnp.sin(ang).reshape(n, -1)
    pad = jnp.zeros((n, LANES - ROPE_DIM), F32)
    cc = jnp.concatenate([cos, cos, pad], axis=1)
    ss = jnp.concatenate([-sin, sin, pad], axis=1)
    bias_tab = _bias_tables(rel_bias)

    xf = x.reshape(n, d)
    memf = mem.reshape(bsz * mem.shape[1], d)
    ln_g3 = ln_g.reshape(depth * 3, 1, d)
    ln_b3 = ln_b.reshape(depth * 3, 1, d)
    qn3 = q_norm.reshape(depth, 1, Q_LORA)
    kvn3 = kv_norm.reshape(depth, 1, KV_LORA)
    w_in_p = _pack_w_in_t(w_in)
    w_uq_p = _pack_w_uq(w_uq.astype(BF16))
    w_ukv_p = _pack_w_ukv(w_ukv.astype(BF16))
    w_branch_b = w_branch.astype(BF16)
    w_out_b = w_out.astype(BF16)

    up_b, down_b = ffn1_up[0].astype(BF16), ffn1_down[0].astype(BF16)
    for l in range(depth):
        xf, up_b, down_b = _ffn(xf, up_b, down_b, ln_g3, ln_b3, l, 0, alpha, next_weights=(ffn2_up, ffn2_down, l))
        proj = _proj(xf, w_in_p, l, 1024, 2048, "in_proj", w_is_transposed=True)
        proj3 = proj.reshape(bsz, seq, PROJ_COLS)
        o_a = _dsa(proj3, bias_tab, k_sel).reshape(n, A_WIDTH)
        qt, k, vt = _mla_proj(proj, cc, ss, qn3, kvn3, w_uq_p, w_ukv_p, l)
        o_b = _mla_attn(qt, k.reshape(bsz, seq, -1), vt).reshape(n, B_WIDTH)
        mkv = _proj(memf, w_mem_kv, l, 512, 1024, "mem_proj")
        mkv3 = mkv.reshape(bsz, mem.shape[1], 2 * C_WIDTH)
        xf = _mix_out(o_a, o_b, mkv3, proj, xf, w_branch_b, w_out_b, ln_g3, ln_b3, l, alpha)
        upcoming = (ffn1_up, ffn1_down, l + 1) if l + 1 < depth else None
        xf, up_b, down_b = _ffn(xf, up_b, down_b, ln_g3, ln_b3, l, 2, alpha, next_weights=upcoming)
    return xf.reshape(bsz, seq, d)
```
